```python
import math
import jax, jax.numpy as jnp
from jax import lax
import numpy as np

D_MODEL = 4096
BATCH = 1
SEQ = 8192
DEPTH = 1
DEC_BATCH = 32
DEC_SEQ = 4
PAST_LEN = 8192
PAGE_SIZE = 128

HD = 128
H_A = 12
G_A = 4
H_I = 32
D_I = 64
DSA_TOPK = 256
H_B = 12
G_B = 4
CMP_LEN = 32
CMP_STRIDE = 16
SEL_LEN = 64
N_SEL = 16
WINDOW = 512
N_MEM = 256
H_M = 4
HD_M = 256
N_BUCKETS = 32
MAX_DIST = 128
D_FF = -(-8 * D_MODEL // (3 * 256)) * 256
Q_BLOCK = 128
EPS = 1e-6
NEG = -1e30
PROJ_SIZES = (H_A * HD, G_A * 2 * HD, H_I * D_I, H_I, D_I, H_B * HD, G_B * 2 * HD, G_B * 2 * HD, G_B * 2 * HD, 3 * H_B, H_M * HD_M, 3 * D_MODEL)
D_IN = sum(PROJ_SIZES)

kernel_name = 'hybrid_dsa_nsa_memory_decoder_step'


def rmsnorm(x, g):
    xf = x.astype(jnp.float32)
    y = xf * lax.rsqrt(jnp.mean(xf * xf, axis=-1, keepdims=True) + EPS)
    return (y * g.astype(jnp.float32)).astype(x.dtype)


def norm_key(kv, g):
    return jnp.stack([rmsnorm(kv[..., 0, :], g), kv[..., 1, :]], axis=-2)


def t5_bucket(dist):
    max_exact = N_BUCKETS // 2
    n = jnp.maximum(dist, 0)
    nf = jnp.maximum(n, 1).astype(jnp.float32)
    large = max_exact + (jnp.log(nf / max_exact) / math.log(MAX_DIST / max_exact) * (N_BUCKETS - max_exact)).astype(jnp.int32)
    return jnp.where(n < max_exact, n, jnp.minimum(large, N_BUCKETS - 1))


def masked_softmax(logits, valid):
    p = jax.nn.softmax(jnp.where(valid, logits.astype(jnp.float32), NEG), axis=-1)
    return jnp.where(valid, p, 0.0)


def to_blocks(a):
    B, T = a.shape[:2]
    return a.reshape((B, T // Q_BLOCK, Q_BLOCK) + a.shape[2:]).swapaxes(0, 1)


def from_blocks(a):
    nb, B, Q = a.shape[:3]
    return a.swapaxes(0, 1).reshape((B, nb * Q) + a.shape[3:])


def gather_seq(rows, idx, g=None):
    if g is None:
        return jax.vmap(lambda r, i: r[i])(rows, idx)
    return jax.vmap(lambda r, i: r[i, g])(rows, idx)


def gather_paged(pool, page_table, new_rows, idx, g=None):
    past_len = page_table.shape[1] * PAGE_SIZE
    ic = jnp.clip(idx, 0, past_len - 1)
    page = jax.vmap(lambda pt, i: pt[i // PAGE_SIZE])(page_table, ic)
    off = ic % PAGE_SIZE
    past = pool[page, off] if g is None else pool[page, off, g]
    new = gather_seq(new_rows, jnp.clip(idx - past_len, 0, new_rows.shape[1] - 1), g)
    is_past = (idx < past_len).reshape(idx.shape + (1,) * (past.ndim - idx.ndim))
    return jnp.where(is_past, past, new)


def in_proj(x, g_attn, w_in, qk_gain_a, qk_gain_b, qk_gain_m):
    B, T, _ = x.shape
    z = rmsnorm(x, g_attn) @ w_in
    (q_a, kv_a, qi, wi, ki, q_b, cmp_kv, slc_kv, win_kv, gates_b, q_m, mg) = jnp.split(z, np.cumsum(PROJ_SIZES)[:-1].tolist(), axis=-1)
    q_a = rmsnorm(q_a.reshape(B, T, G_A, H_A // G_A, HD), qk_gain_a[0])
    kv_a = norm_key(kv_a.reshape(B, T, G_A, 2, HD), qk_gain_a[1])
    qi = qi.reshape(B, T, H_I, D_I)
    q_b = rmsnorm(q_b.reshape(B, T, G_B, H_B // G_B, HD), qk_gain_b[0])
    cmp_kv = cmp_kv.reshape(B, T, G_B, 2, HD)
    slc_kv = norm_key(slc_kv.reshape(B, T, G_B, 2, HD), qk_gain_b[2])
    win_kv = norm_key(win_kv.reshape(B, T, G_B, 2, HD), qk_gain_b[3])
    q_m = rmsnorm(q_m.reshape(B, T, H_M, HD_M), qk_gain_m[0])
    return (q_a, kv_a, qi, wi, ki, q_b, cmp_kv, slc_kv, win_kv, gates_b, q_m, mg)


def dsa_block(q, qi, wi, qpos, kidx, gather_kv, bias_a, topk):
    B, Tq, G, R, _ = q.shape
    S = kidx.shape[1]
    sc = jax.nn.relu(jnp.einsum('bthd,bsd->bths', qi, kidx).astype(jnp.float32))
    sc = jnp.einsum('bths,bth->bts', sc, wi.astype(jnp.float32))
    causal = jnp.arange(S)[None, :] <= qpos[:, None]
    _, idx = lax.top_k(jnp.where(causal[None], sc, NEG), topk)
    valid = (idx <= qpos[None, :, None])[:, :, None, None, :]
    kv = gather_kv(idx)
    k, v = kv[..., 0, :], kv[..., 1, :]
    logits = jnp.einsum('btgrd,btkgd->btgrk', q, k).astype(jnp.float32) * HD ** -0.5
    bias = bias_a[t5_bucket(qpos[None, :, None] - idx)].reshape(B, Tq, topk, G, R).transpose(0, 1, 3, 4, 2)
    p = masked_softmax(logits + bias, valid)
    o = jnp.einsum('btgrk,btkgd->btgrd', p.astype(v.dtype), v)
    return o.reshape(B, Tq, G * R * HD)


def compress(rows, cmp_pe, cmp_w1, cmp_w2, gain_c):
    B, S = rows.shape[:2]
    r = CMP_LEN // CMP_STRIDE
    n_chunk = -(-S // CMP_STRIDE)
    nc = n_chunk - r + 1
    rows = jnp.pad(rows, ((0, 0), (0, n_chunk * CMP_STRIDE - S), (0, 0), (0, 0), (0, 0)))
    ch = rows.reshape(B, n_chunk, CMP_STRIDE, G_B, 2, HD)
    w1 = cmp_w1.reshape(2, r, CMP_STRIDE, HD, HD)
    pe = cmp_pe.reshape(r, CMP_STRIDE, 2, HD)
    hid = jnp.einsum('jscd,cjsde->ce', pe, w1)[None, None, None]
    for j in range(r):
        hid = hid + jnp.einsum('bnsgcd,csde->bngce', ch[:, j:j + nc], w1[:, j])
    out = jnp.einsum('bngce,ced->bngcd', jax.nn.silu(hid), cmp_w2)
    kc = rmsnorm(out[..., 0, :], gain_c)
    vc = out[..., 1, :]
    ends = jnp.arange(nc) * CMP_STRIDE + CMP_LEN - 1
    return kc, vc, ends


def cmp_to_sel(n_cmp, n_sel_blocks):
    cs = jnp.arange(n_cmp)[:, None] * CMP_STRIDE
    ss = jnp.arange(n_sel_blocks)[None, :] * SEL_LEN
    return ((cs < ss + SEL_LEN) & (cs + CMP_LEN > ss)).astype(jnp.float32)


def nsa_block(q, gates, qpos, kc, vc, cmp_end, sel_map, gather_slc, wkv, kwpos, bias_b):
    B, Tq, G, R, _ = q.shape
    scale = HD ** -0.5
    bias_g = bias_b.reshape(N_BUCKETS, G, R)
    lc = jnp.einsum('btgrd,bngd->btgrn', q, kc).astype(jnp.float32) * scale
    pc = masked_softmax(lc, (cmp_end[None, :] <= qpos[:, None])[None, :, None, None, :])
    oc = jnp.einsum('btgrn,bngd->btgrd', pc.astype(vc.dtype), vc)
    imp = jnp.einsum('btgrn,ns->btgs', pc, sel_map)
    blk = jnp.arange(sel_map.shape[1])
    start_ok = (blk[None, :] * SEL_LEN <= qpos[:, None])[None, :, None, :]
    cur = (qpos // SEL_LEN)[:, None]
    forced = ((blk[None, :] == 0) | (blk[None, :] == cur) | (blk[None, :] == cur - 1))[None, :, None, :]
    score = jnp.where(start_ok, jnp.where(forced, -NEG, imp), NEG)
    n_sel = min(N_SEL, sel_map.shape[1])
    _, sel = lax.top_k(score, n_sel)
    tok = (sel[..., None] * SEL_LEN + jnp.arange(SEL_LEN)).reshape(B, Tq, G, n_sel * SEL_LEN)
    skv = gather_slc(tok)
    ls = jnp.einsum('btgrd,btgmd->btgrm', q, skv[..., 0, :]).astype(jnp.float32) * scale
    bs = bias_g[t5_bucket(qpos[None, :, None, None] - tok), jnp.arange(G)[:, None]]
    ps = masked_softmax(ls + jnp.moveaxis(bs, -1, 3), (tok <= qpos[None, :, None, None])[:, :, :, None, :])
    o_s = jnp.einsum('btgrm,btgmd->btgrd', ps.astype(skv.dtype), skv[..., 1, :])
    lw = jnp.einsum('btgrd,bsgd->btgrs', q, wkv[..., 0, :]).astype(jnp.float32) * scale
    dist = qpos[:, None] - kwpos[None, :]
    wvalid = (dist >= 0) & (dist < WINDOW) & (kwpos[None, :] >= 0)
    bw = bias_g[t5_bucket(dist)].transpose(0, 2, 3, 1)[None]
    pw = masked_softmax(lw + bw, wvalid[None, :, None, None, :])
    ow = jnp.einsum('btgrs,bsgd->btgrd', pw.astype(wkv.dtype), wkv[..., 1, :])
    gt = jax.nn.sigmoid(gates.astype(jnp.float32)).reshape(B, Tq, G, R, 3)
    o = gt[..., 0:1] * oc + gt[..., 1:2] * o_s + gt[..., 2:3] * ow
    return o.reshape(B, Tq, G * R * HD).astype(q.dtype)


def dsa_prompt(q_a, qi, wi, kv_a, ki, bias_a):
    T = q_a.shape[1]
    topk = min(DSA_TOPK, T // 4)
    starts = jnp.arange(T // Q_BLOCK) * Q_BLOCK

    def body(blk):
        qb, qib, wib, t0 = blk
        return dsa_block(qb, qib, wib, t0 + jnp.arange(Q_BLOCK), ki, lambda idx: gather_seq(kv_a, idx), bias_a, topk)

    return from_blocks(lax.map(body, (to_blocks(q_a), to_blocks(qi), to_blocks(wi), starts)))


def nsa_prompt(q_b, gates_b, cmp_kv, slc_kv, win_kv, cmp_pe, cmp_w1, cmp_w2, gain_c, bias_b):
    T = q_b.shape[1]
    kc, vc, ends = compress(cmp_kv, cmp_pe, cmp_w1, cmp_w2, gain_c)
    sel_map = cmp_to_sel(kc.shape[1], -(-T // SEL_LEN))
    win_pad = jnp.pad(win_kv, ((0, 0), (WINDOW, 0), (0, 0), (0, 0), (0, 0)))
    g_idx = jnp.arange(G_B)[:, None]
    starts = jnp.arange(T // Q_BLOCK) * Q_BLOCK

    def body(blk):
        qb, gb, t0 = blk
        wkv = lax.dynamic_slice_in_dim(win_pad, t0, WINDOW + Q_BLOCK, axis=1)
        kwpos = t0 - WINDOW + jnp.arange(WINDOW + Q_BLOCK)
        return nsa_block(qb, gb, t0 + jnp.arange(Q_BLOCK), kc, vc, ends, sel_map, lambda idx: gather_seq(slc_kv, idx, g_idx), wkv, kwpos, bias_b)

    return from_blocks(lax.map(body, (to_blocks(q_b), to_blocks(gates_b), starts)))


def memory_kv(mem, g_mem, w_mem_kv, gain_k):
    B, N, _ = mem.shape
    kv = (rmsnorm(mem, g_mem) @ w_mem_kv).reshape(B, N, H_M, 2, HD_M)
    return norm_key(kv, gain_k)


def mem_attend(q, mkv):
    B, T = q.shape[:2]
    l = jnp.einsum('bthd,bnhd->bthn', q, mkv[..., 0, :]).astype(jnp.float32) * HD_M ** -0.5
    p = jax.nn.softmax(l, axis=-1).astype(mkv.dtype)
    return jnp.einsum('bthn,bnhd->bthd', p, mkv[..., 1, :]).reshape(B, T, H_M * HD_M)


def merge_ffn(x, o_a, o_b, o_m, mg, w_up_a, w_up_b, w_up_m, w_out, g_ffn, w_ffn_in, w_ffn_out):
    B, T, D = x.shape
    g = jax.nn.sigmoid(mg.astype(jnp.float32)).reshape(B, T, 3, D)
    mix = g[:, :, 0] * (o_a @ w_up_a) + g[:, :, 1] * (o_b @ w_up_b) + g[:, :, 2] * (o_m @ w_up_m)
    x = x + mix.astype(x.dtype) @ w_out
    a, u = jnp.split(rmsnorm(x, g_ffn) @ w_ffn_in, 2, axis=-1)
    return x + (jax.nn.silu(a) * u) @ w_ffn_out


def setup_inputs(seed: int = 0) -> dict:
    key = jax.random.key(seed)
    ks = list(jax.random.split(key, 32))

    def nrm(i, shape, scale=1.0):
        return jax.random.normal(ks[i], shape, jnp.float32) * scale

    def gain(i, shape):
        return 1.0 + 0.05 * jax.random.normal(ks[i], shape, jnp.float32)

    n_pages = PAST_LEN // PAGE_SIZE
    n_used = DEC_BATCH * n_pages
    n_pool = n_used + max(1, n_used // 4)
    w_eff = min(WINDOW, PAST_LEN)
    page_table = jax.random.permutation(ks[0], n_pool)[:n_used].reshape(DEC_BATCH, n_pages).astype(jnp.int32)
    return {
        'x_prompt': nrm(1, (BATCH, SEQ, D_MODEL)),
        'x_sample': nrm(2, (DEC_BATCH, DEC_SEQ, D_MODEL)),
        'cache_a_kv': nrm(3, (DEPTH, n_pool, PAGE_SIZE, G_A, 2, HD)),
        'cache_a_idx': nrm(4, (DEPTH, n_pool, PAGE_SIZE, D_I)),
        'cache_b_cmp': nrm(5, (DEPTH, n_pool, PAGE_SIZE, G_B, 2, HD)),
        'cache_b_slc': nrm(6, (DEPTH, n_pool, PAGE_SIZE, G_B, 2, HD)),
        'cache_b_win': nrm(7, (DEPTH, DEC_BATCH, w_eff, G_B, 2, HD)),
        'cache_mem_kv': nrm(8, (DEPTH, DEC_BATCH, N_MEM, H_M, 2, HD_M)),
        'page_table': page_table,
        'mem_prompt': nrm(9, (BATCH, N_MEM, D_MODEL)),
        'g_attn': gain(10, (DEPTH, D_MODEL)),
        'w_in': nrm(11, (DEPTH, D_MODEL, D_IN), D_MODEL ** -0.5),
        'qk_gain_a': gain(12, (DEPTH, 2, HD)),
        'qk_gain_b': gain(13, (DEPTH, 4, HD)),
        'qk_gain_m': gain(14, (DEPTH, 2, HD_M)),
        'cmp_pe': nrm(15, (DEPTH, CMP_LEN, 2, HD), 0.2),
        'cmp_w1': nrm(16, (DEPTH, 2, CMP_LEN * HD, HD), (CMP_LEN * HD) ** -0.5),
        'cmp_w2': nrm(17, (DEPTH, 2, HD, HD), HD ** -0.5),
        'rel_bias': nrm(18, (N_BUCKETS, H_A + H_B), 0.5),
        'g_mem': gain(19, (DEPTH, D_MODEL)),
        'w_mem_kv': nrm(20, (DEPTH, D_MODEL, H_M * 2 * HD_M), D_MODEL ** -0.5),
        'w_up_a': nrm(21, (DEPTH, H_A * HD, D_MODEL), (H_A * HD) ** -0.5),
        'w_up_b': nrm(22, (DEPTH, H_B * HD, D_MODEL), (H_B * HD) ** -0.5),
        'w_up_m': nrm(23, (DEPTH, H_M * HD_M, D_MODEL), (H_M * HD_M) ** -0.5),
        'w_out': nrm(24, (DEPTH, D_MODEL, D_MODEL), D_MODEL ** -0.5),
        'g_ffn': gain(25, (DEPTH, D_MODEL)),
        'w_ffn_in': nrm(26, (DEPTH, D_MODEL, 2 * D_FF), D_MODEL ** -0.5),
        'w_ffn_out': nrm(27, (DEPTH, D_FF, D_MODEL), D_FF ** -0.5),
    }


def reference(x_prompt, x_sample, cache_a_kv, cache_a_idx, cache_b_cmp, cache_b_slc, cache_b_win, cache_mem_kv, page_table, mem_prompt, g_attn, w_in, qk_gain_a, qk_gain_b, qk_gain_m, cmp_pe, cmp_w1, cmp_w2, rel_bias, g_mem, w_mem_kv, w_up_a, w_up_b, w_up_m, w_out, g_ffn, w_ffn_in, w_ffn_out):
    bias_a = rel_bias[:, :H_A]
    bias_b = rel_bias[:, H_A:]
    past_len = page_table.shape[1] * PAGE_SIZE
    DB, DS = x_sample.shape[:2]
    s_len = past_len + DS
    qpos_s = past_len + jnp.arange(DS)
    w_eff = cache_b_win.shape[2]
    kwpos_s = past_len - w_eff + jnp.arange(w_eff + DS)
    g_idx = jnp.arange(G_B)[:, None]
    h_p, h_s = x_prompt, x_sample
    p_akv, p_aidx, p_bcmp, p_bslc, p_bwin, p_mkv = [], [], [], [], [], []
    s_akv, s_aidx, s_bcmp, s_bslc, s_bwin = [], [], [], [], []
    for l in range(DEPTH):
        (q_a, kv_a, qi, wi, ki, q_b, cmp_kv, slc_kv, win_kv, gates_b, q_m, mg) = in_proj(h_p, g_attn[l], w_in[l], qk_gain_a[l], qk_gain_b[l], qk_gain_m[l])
        T = h_p.shape[1]
        o_a = dsa_prompt(q_a, qi, wi, kv_a, ki, bias_a)
        o_b = nsa_prompt(q_b, gates_b, cmp_kv, slc_kv, win_kv, cmp_pe[l], cmp_w1[l], cmp_w2[l], qk_gain_b[l, 1], bias_b)
        mkv = memory_kv(mem_prompt, g_mem[l], w_mem_kv[l], qk_gain_m[l, 1])
        o_m = mem_attend(q_m, mkv)
        h_p = merge_ffn(h_p, o_a, o_b, o_m, mg, w_up_a[l], w_up_b[l], w_up_m[l], w_out[l], g_ffn[l], w_ffn_in[l], w_ffn_out[l])
        p_akv.append(kv_a)
        p_aidx.append(ki)
        p_bcmp.append(cmp_kv)
        p_bslc.append(slc_kv)
        p_bwin.append(win_kv[:, T - min(WINDOW, T):])
        p_mkv.append(mkv)
        (q_a, kv_a, qi, wi, ki, q_b, cmp_kv, slc_kv, win_kv, gates_b, q_m, mg) = in_proj(h_s, g_attn[l], w_in[l], qk_gain_a[l], qk_gain_b[l], qk_gain_m[l])
        pool_a = cache_a_kv[l]
        ki_all = jnp.concatenate([cache_a_idx[l][page_table].reshape(DB, past_len, D_I), ki], axis=1)
        o_a = dsa_block(q_a, qi, wi, qpos_s, ki_all, lambda idx: gather_paged(pool_a, page_table, kv_a, idx), bias_a, min(DSA_TOPK, s_len // 4))
        cmp_all = jnp.concatenate([cache_b_cmp[l][page_table].reshape(DB, past_len, G_B, 2, HD), cmp_kv], axis=1)
        kc, vc, ends = compress(cmp_all, cmp_pe[l], cmp_w1[l], cmp_w2[l], qk_gain_b[l, 1])
        sel_map = cmp_to_sel(kc.shape[1], -(-s_len // SEL_LEN))
        pool_s = cache_b_slc[l]
        win_all = jnp.concatenate([cache_b_win[l], win_kv], axis=1)
        o_b = nsa_block(q_b, gates_b, qpos_s, kc, vc, ends, sel_map, lambda idx: gather_paged(pool_s, page_table, slc_kv, idx, g_idx), win_all, kwpos_s, bias_b)
        o_m = mem_attend(q_m, cache_mem_kv[l])
        h_s = merge_ffn(h_s, o_a, o_b, o_m, mg, w_up_a[l], w_up_b[l], w_up_m[l], w_out[l], g_ffn[l], w_ffn_in[l], w_ffn_out[l])
        s_akv.append(kv_a)
        s_aidx.append(ki)
        s_bcmp.append(cmp_kv)
        s_bslc.append(slc_kv)
        s_bwin.append(win_all[:, win_all.shape[1] - w_eff:])
    return (h_p, h_s, jnp.stack(p_akv), jnp.stack(p_aidx), jnp.stack(p_bcmp), jnp.stack(p_bslc), jnp.stack(p_bwin), jnp.stack(p_mkv), jnp.stack(s_akv), jnp.stack(s_aidx), jnp.stack(s_bcmp), jnp.stack(s_bslc), jnp.stack(s_bwin))
```

```python
import functools
import math

import numpy as np
import jax
import jax.numpy as jnp
from jax import lax
from jax.experimental import pallas as pl
from jax.experimental.pallas import tpu as pltpu

D_MODEL = 4096
PAGE_SIZE = 128
HD = 128
H_A = 12
G_A = 4
H_I = 32
D_I = 64
DSA_TOPK = 256
H_B = 12
G_B = 4
CMP_LEN = 32
CMP_STRIDE = 16
SEL_LEN = 64
N_SEL = 16
WINDOW = 512
N_MEM = 256
H_M = 4
HD_M = 256
N_BUCKETS = 32
MAX_DIST = 128
D_FF = -(-8 * D_MODEL // (3 * 256)) * 256
Q_BLOCK = 128
EPS = 1e-6
NEG = -1e30
PROJ_SIZES = (H_A * HD, G_A * 2 * HD, H_I * D_I, H_I, D_I, H_B * HD, G_B * 2 * HD, G_B * 2 * HD, G_B * 2 * HD, 3 * H_B, H_M * HD_M, 3 * D_MODEL)
D_MAIN = sum(PROJ_SIZES[:-1])

VMEM_LIMIT_BYTES = 56 * 1024 * 1024
LANE = 128
COL_TILE = 512


def _round_up(n, m):
    return -(-n // m) * m


def _cparams(*sem):
    return pltpu.CompilerParams(dimension_semantics=sem, vmem_limit_bytes=VMEM_LIMIT_BYTES)


def _rms_cast_kernel(x_ref, g_ref, o_ref):
    x = x_ref[...]
    ms = jnp.mean(x * x, axis=-1, keepdims=True)
    o_ref[...] = ((x * lax.rsqrt(ms + EPS)) * g_ref[...]).astype(o_ref.dtype)


def rms_cast(x, g, tm):
    M, D = x.shape
    return pl.pallas_call(
        _rms_cast_kernel,
        out_shape=jax.ShapeDtypeStruct((M, D), jnp.bfloat16),
        grid=(M // tm,),
        in_specs=[pl.BlockSpec((tm, D), lambda i: (i, 0)), pl.BlockSpec((1, D), lambda i: (0, 0))],
        out_specs=pl.BlockSpec((tm, D), lambda i: (i, 0)),
        compiler_params=_cparams("parallel"),
        name="rms_cast",
    )(x, g.reshape(1, D))


def _mm_kernel(a_ref, w_ref, o_ref):
    o_ref[...] = jnp.dot(a_ref[...], w_ref[...], preferred_element_type=jnp.float32).astype(o_ref.dtype)


def matmul(a, w, tm, tn, out_dtype=jnp.float32, name="matmul"):
    M, K = a.shape
    N = w.shape[1]
    return pl.pallas_call(
        _mm_kernel,
        out_shape=jax.ShapeDtypeStruct((M, N), out_dtype),
        grid=(M // tm, N // tn),
        in_specs=[pl.BlockSpec((tm, K), lambda i, j: (i, 0)), pl.BlockSpec((K, tn), lambda i, j: (0, j))],
        out_specs=pl.BlockSpec((tm, tn), lambda i, j: (i, j)),
        compiler_params=_cparams("parallel", "arbitrary"),
        name=name,
    )(a, w)


def _upmix_kernel(oa_ref, ob_ref, om_ref, wa_ref, wb_ref, wm_ref, g0_ref, g1_ref, g2_ref, o_ref):
    ya = jnp.dot(oa_ref[...], wa_ref[...], preferred_element_type=jnp.float32)
    yb = jnp.dot(ob_ref[...], wb_ref[...], preferred_element_type=jnp.float32)
    ym = jnp.dot(om_ref[...], wm_ref[...], preferred_element_type=jnp.float32)
    mix = jax.nn.sigmoid(g0_ref[...]) * ya + jax.nn.sigmoid(g1_ref[...]) * yb + jax.nn.sigmoid(g2_ref[...]) * ym
    o_ref[...] = mix.astype(o_ref.dtype)


def upmix(o_a, o_b, o_m, w_a, w_b, w_m, mg, tm, tn):
    M = o_a.shape[0]
    D = w_a.shape[1]
    nj = D // tn
    a_spec = lambda k: pl.BlockSpec((tm, k), lambda i, j: (i, 0))
    w_spec = lambda k: pl.BlockSpec((k, tn), lambda i, j: (0, j))
    g_spec = lambda c: pl.BlockSpec((tm, tn), lambda i, j: (i, j + c * nj))
    return pl.pallas_call(
        _upmix_kernel,
        out_shape=jax.ShapeDtypeStruct((M, D), jnp.bfloat16),
        grid=(M // tm, nj),
        in_specs=[a_spec(o_a.shape[1]), a_spec(o_b.shape[1]), a_spec(o_m.shape[1]),
                  w_spec(w_a.shape[0]), w_spec(w_b.shape[0]), w_spec(w_m.shape[0]),
                  g_spec(0), g_spec(1), g_spec(2)],
        out_specs=pl.BlockSpec((tm, tn), lambda i, j: (i, j)),
        compiler_params=_cparams("parallel", "arbitrary"),
        name="upmix",
    )(o_a, o_b, o_m, w_a, w_b, w_m, mg, mg, mg)


def _mm_res_kernel(a_ref, w_ref, r_ref, o_ref):
    o_ref[...] = r_ref[...] + jnp.dot(a_ref[...], w_ref[...], preferred_element_type=jnp.float32)


def matmul_residual(a, w, r, tm, tn):
    M, K = a.shape
    N = w.shape[1]
    return pl.pallas_call(
        _mm_res_kernel,
        out_shape=jax.ShapeDtypeStruct((M, N), jnp.float32),
        grid=(M // tm, N // tn),
        in_specs=[pl.BlockSpec((tm, K), lambda i, j: (i, 0)), pl.BlockSpec((K, tn), lambda i, j: (0, j)),
                  pl.BlockSpec((tm, tn), lambda i, j: (i, j))],
        out_specs=pl.BlockSpec((tm, tn), lambda i, j: (i, j)),
        compiler_params=_cparams("parallel", "arbitrary"),
        name="matmul_residual",
    )(a, w, r)


def _swiglu_kernel(h_ref, wa_ref, wu_ref, o_ref):
    h = h_ref[...]
    a = jnp.dot(h, wa_ref[...], preferred_element_type=jnp.float32)
    u = jnp.dot(h, wu_ref[...], preferred_element_type=jnp.float32)
    o_ref[...] = (a * jax.nn.sigmoid(a) * u).astype(o_ref.dtype)


def swiglu_in(h, w_a, w_u, tm, tn):
    M, K = h.shape
    N = w_a.shape[1]
    return pl.pallas_call(
        _swiglu_kernel,
        out_shape=jax.ShapeDtypeStruct((M, N), jnp.bfloat16),
        grid=(M // tm, N // tn),
        in_specs=[pl.BlockSpec((tm, K), lambda i, j: (i, 0)), pl.BlockSpec((K, tn), lambda i, j: (0, j)),
                  pl.BlockSpec((K, tn), lambda i, j: (0, j))],
        out_specs=pl.BlockSpec((tm, tn), lambda i, j: (i, j)),
        compiler_params=_cparams("parallel", "arbitrary"),
        name="swiglu_in",
    )(h, w_a, w_u)


def _mm_res_acc_kernel(a_ref, w_ref, r_ref, o_ref, acc_ref):
    k = pl.program_id(2)

    @pl.when(k == 0)
    def _():
        acc_ref[...] = r_ref[...]

    acc_ref[...] += jnp.dot(a_ref[...], w_ref[...], preferred_element_type=jnp.float32)

    @pl.when(k == pl.num_programs(2) - 1)
    def _():
        o_ref[...] = acc_ref[...]


def matmul_residual_ksplit(a, w, r, tm, tn, tk):
    M, K = a.shape
    N = w.shape[1]
    return pl.pallas_call(
        _mm_res_acc_kernel,
        out_shape=jax.ShapeDtypeStruct((M, N), jnp.float32),
        grid=(M // tm, N // tn, K // tk),
        in_specs=[pl.BlockSpec((tm, tk), lambda i, j, k: (i, k)), pl.BlockSpec((tk, tn), lambda i, j, k: (k, j)),
                  pl.BlockSpec((tm, tn), lambda i, j, k: (i, j))],
        out_specs=pl.BlockSpec((tm, tn), lambda i, j, k: (i, j)),
        scratch_shapes=[pltpu.VMEM((tm, tn), jnp.float32)],
        compiler_params=_cparams("parallel", "arbitrary", "arbitrary"),
        name="matmul_residual_ksplit",
    )(a, w, r)


def rmsnorm(x, g):
    xf = x.astype(jnp.float32)
    y = xf * lax.rsqrt(jnp.mean(xf * xf, axis=-1, keepdims=True) + EPS)
    return (y * g.astype(jnp.float32)).astype(x.dtype)


def norm_key(kv, g):
    return jnp.stack([rmsnorm(kv[..., 0, :], g), kv[..., 1, :]], axis=-2)


def t5_bucket(dist):
    max_exact = N_BUCKETS // 2
    n = jnp.maximum(dist, 0)
    nf = jnp.maximum(n, 1).astype(jnp.float32)
    large = max_exact + (jnp.log(nf / max_exact) / math.log(MAX_DIST / max_exact) * (N_BUCKETS - max_exact)).astype(jnp.int32)
    return jnp.where(n < max_exact, n, jnp.minimum(large, N_BUCKETS - 1))


def masked_softmax(logits, valid):
    p = jax.nn.softmax(jnp.where(valid, logits.astype(jnp.float32), NEG), axis=-1)
    return jnp.where(valid, p, 0.0)


def to_blocks(a):
    B, T = a.shape[:2]
    return a.reshape((B, T // Q_BLOCK, Q_BLOCK) + a.shape[2:]).swapaxes(0, 1)


def from_blocks(a):
    nb, B, Q = a.shape[:3]
    return a.swapaxes(0, 1).reshape((B, nb * Q) + a.shape[3:])


def gather_seq(rows, idx, g=None):
    if g is None:
        return jax.vmap(lambda r, i: r[i])(rows, idx)
    return jax.vmap(lambda r, i: r[i, g])(rows, idx)


def gather_paged(pool, page_table, new_rows, idx, g=None):
    past_len = page_table.shape[1] * PAGE_SIZE
    ic = jnp.clip(idx, 0, past_len - 1)
    page = jax.vmap(lambda pt, i: pt[i // PAGE_SIZE])(page_table, ic)
    off = ic % PAGE_SIZE
    past = pool[page, off] if g is None else pool[page, off, g]
    new = gather_seq(new_rows, jnp.clip(idx - past_len, 0, new_rows.shape[1] - 1), g)
    is_past = (idx < past_len).reshape(idx.shape + (1,) * (past.ndim - idx.ndim))
    return jnp.where(is_past, past, new)


def split_main(z, B, T, qk_gain_a, qk_gain_b, qk_gain_m):
    (q_a, kv_a, qi, wi, ki, q_b, cmp_kv, slc_kv, win_kv, gates_b, q_m) = jnp.split(
        z[:, :D_MAIN].reshape(B, T, D_MAIN), np.cumsum(PROJ_SIZES[:-2]).tolist(), axis=-1)
    q_a = rmsnorm(q_a.reshape(B, T, G_A, H_A // G_A, HD), qk_gain_a[0])
    kv_a = norm_key(kv_a.reshape(B, T, G_A, 2, HD), qk_gain_a[1])
    qi = qi.reshape(B, T, H_I, D_I)
    q_b = rmsnorm(q_b.reshape(B, T, G_B, H_B // G_B, HD), qk_gain_b[0])
    cmp_kv = cmp_kv.reshape(B, T, G_B, 2, HD)
    slc_kv = norm_key(slc_kv.reshape(B, T, G_B, 2, HD), qk_gain_b[2])
    win_kv = norm_key(win_kv.reshape(B, T, G_B, 2, HD), qk_gain_b[3])
    q_m = rmsnorm(q_m.reshape(B, T, H_M, HD_M), qk_gain_m[0])
    return (q_a, kv_a, qi, wi, ki, q_b, cmp_kv, slc_kv, win_kv, gates_b, q_m)


def dsa_block(q, qi, wi, qpos, kidx, gather_kv, bias_a, topk):
    B, Tq, G, R, _ = q.shape
    S = kidx.shape[1]
    sc = jax.nn.relu(jnp.einsum('bthd,bsd->bths', qi, kidx).astype(jnp.float32))
    sc = jnp.einsum('bths,bth->bts', sc, wi.astype(jnp.float32))
    causal = jnp.arange(S)[None, :] <= qpos[:, None]
    _, idx = lax.top_k(jnp.where(causal[None], sc, NEG), topk)
    valid = (idx <= qpos[None, :, None])[:, :, None, None, :]
    kv = gather_kv(idx)
    k, v = kv[..., 0, :], kv[..., 1, :]
    logits = jnp.einsum('btgrd,btkgd->btgrk', q, k).astype(jnp.float32) * HD ** -0.5
    bias = bias_a[t5_bucket(qpos[None, :, None] - idx)].reshape(B, Tq, topk, G, R).transpose(0, 1, 3, 4, 2)
    p = masked_softmax(logits + bias, valid)
    o = jnp.einsum('btgrk,btkgd->btgrd', p.astype(v.dtype), v)
    return o.reshape(B, Tq, G * R * HD)


def compress(rows, cmp_pe, cmp_w1, cmp_w2, gain_c):
    B, S = rows.shape[:2]
    r = CMP_LEN // CMP_STRIDE
    n_chunk = -(-S // CMP_STRIDE)
    nc = n_chunk - r + 1
    rows = jnp.pad(rows, ((0, 0), (0, n_chunk * CMP_STRIDE - S), (0, 0), (0, 0), (0, 0)))
    ch = rows.reshape(B, n_chunk, CMP_STRIDE, G_B, 2, HD)
    w1 = cmp_w1.reshape(2, r, CMP_STRIDE, HD, HD)
    pe = cmp_pe.reshape(r, CMP_STRIDE, 2, HD)
    hid = jnp.einsum('jscd,cjsde->ce', pe, w1)[None, None, None]
    for j in range(r):
        hid = hid + jnp.einsum('bnsgcd,csde->bngce', ch[:, j:j + nc], w1[:, j])
    out = jnp.einsum('bngce,ced->bngcd', jax.nn.silu(hid), cmp_w2)
    kc = rmsnorm(out[..., 0, :], gain_c)
    vc = out[..., 1, :]
    ends = jnp.arange(nc) * CMP_STRIDE + CMP_LEN - 1
    return kc, vc, ends


def cmp_to_sel(n_cmp, n_sel_blocks):
    cs = jnp.arange(n_cmp)[:, None] * CMP_STRIDE
    ss = jnp.arange(n_sel_blocks)[None, :] * SEL_LEN
    return ((cs < ss + SEL_LEN) & (cs + CMP_LEN > ss)).astype(jnp.float32)


def nsa_block(q, gates, qpos, kc, vc, cmp_end, sel_map, gather_slc, wkv, kwpos, bias_b):
    B, Tq, G, R, _ = q.shape
    scale = HD ** -0.5
    bias_g = bias_b.reshape(N_BUCKETS, G, R)
    lc = jnp.einsum('btgrd,bngd->btgrn', q, kc).astype(jnp.float32) * scale
    pc = masked_softmax(lc, (cmp_end[None, :] <= qpos[:, None])[None, :, None, None, :])
    oc = jnp.einsum('btgrn,bngd->btgrd', pc.astype(vc.dtype), vc)
    imp = jnp.einsum('btgrn,ns->btgs', pc, sel_map)
    blk = jnp.arange(sel_map.shape[1])
    start_ok = (blk[None, :] * SEL_LEN <= qpos[:, None])[None, :, None, :]
    cur = (qpos // SEL_LEN)[:, None]
    forced = ((blk[None, :] == 0) | (blk[None, :] == cur) | (blk[None, :] == cur - 1))[None, :, None, :]
    score = jnp.where(start_ok, jnp.where(forced, -NEG, imp), NEG)
    n_sel = min(N_SEL, sel_map.shape[1])
    _, sel = lax.top_k(score, n_sel)
    tok = (sel[..., None] * SEL_LEN + jnp.arange(SEL_LEN)).reshape(B, Tq, G, n_sel * SEL_LEN)
    skv = gather_slc(tok)
    ls = jnp.einsum('btgrd,btgmd->btgrm', q, skv[..., 0, :]).astype(jnp.float32) * scale
    bs = bias_g[t5_bucket(qpos[None, :, None, None] - tok), jnp.arange(G)[:, None]]
    ps = masked_softmax(ls + jnp.moveaxis(bs, -1, 3), (tok <= qpos[None, :, None, None])[:, :, :, None, :])
    o_s = jnp.einsum('btgrm,btgmd->btgrd', ps.astype(skv.dtype), skv[..., 1, :])
    lw = jnp.einsum('btgrd,bsgd->btgrs', q, wkv[..., 0, :]).astype(jnp.float32) * scale
    dist = qpos[:, None] - kwpos[None, :]
    wvalid = (dist >= 0) & (dist < WINDOW) & (kwpos[None, :] >= 0)
    bw = bias_g[t5_bucket(dist)].transpose(0, 2, 3, 1)[None]
    pw = masked_softmax(lw + bw, wvalid[None, :, None, None, :])
    ow = jnp.einsum('btgrs,bsgd->btgrd', pw.astype(wkv.dtype), wkv[..., 1, :])
    gt = jax.nn.sigmoid(gates.astype(jnp.float32)).reshape(B, Tq, G, R, 3)
    o = gt[..., 0:1] * oc + gt[..., 1:2] * o_s + gt[..., 2:3] * ow
    return o.reshape(B, Tq, G * R * HD).astype(q.dtype)


def dsa_prompt(q_a, qi, wi, kv_a, ki, bias_a):
    T = q_a.shape[1]
    topk = min(DSA_TOPK, T // 4)
    starts = jnp.arange(T // Q_BLOCK) * Q_BLOCK

    def body(blk):
        qb, qib, wib, t0 = blk
        return dsa_block(qb, qib, wib, t0 + jnp.arange(Q_BLOCK), ki, lambda idx: gather_seq(kv_a, idx), bias_a, topk)

    return from_blocks(lax.map(body, (to_blocks(q_a), to_blocks(qi), to_blocks(wi), starts)))


def nsa_prompt(q_b, gates_b, cmp_kv, slc_kv, win_kv, cmp_pe, cmp_w1, cmp_w2, gain_c, bias_b):
    T = q_b.shape[1]
    kc, vc, ends = compress(cmp_kv, cmp_pe, cmp_w1, cmp_w2, gain_c)
    sel_map = cmp_to_sel(kc.shape[1], -(-T // SEL_LEN))
    win_pad = jnp.pad(win_kv, ((0, 0), (WINDOW, 0), (0, 0), (0, 0), (0, 0)))
    g_idx = jnp.arange(G_B)[:, None]
    starts = jnp.arange(T // Q_BLOCK) * Q_BLOCK

    def body(blk):
        qb, gb, t0 = blk
        wkv = lax.dynamic_slice_in_dim(win_pad, t0, WINDOW + Q_BLOCK, axis=1)
        kwpos = t0 - WINDOW + jnp.arange(WINDOW + Q_BLOCK)
        return nsa_block(qb, gb, t0 + jnp.arange(Q_BLOCK), kc, vc, ends, sel_map, lambda idx: gather_seq(slc_kv, idx, g_idx), wkv, kwpos, bias_b)

    return from_blocks(lax.map(body, (to_blocks(q_b), to_blocks(gates_b), starts)))


def mem_attend(q, mkv):
    B, T = q.shape[:2]
    l = jnp.einsum('bthd,bnhd->bthn', q, mkv[..., 0, :]).astype(jnp.float32) * HD_M ** -0.5
    p = jax.nn.softmax(l, axis=-1).astype(mkv.dtype)
    return jnp.einsum('bthn,bnhd->bthd', p, mkv[..., 1, :]).reshape(B, T, H_M * HD_M)


def _pad_cols(w, n):
    return jnp.pad(w, ((0, 0), (0, n - w.shape[1])))


def prepare_weights(w_in, w_mem_kv, w_up_a, w_up_b, w_up_m, w_out, w_ffn_in, w_ffn_out):
    bf = jnp.bfloat16
    d_ff_pad = _round_up(D_FF, COL_TILE)
    return dict(
        w_main=_pad_cols(w_in[:, :D_MAIN], _round_up(D_MAIN, COL_TILE)).astype(bf),
        w_mg=w_in[:, D_MAIN:].astype(bf),
        w_mem_kv=w_mem_kv.astype(bf),
        w_up_a=w_up_a.astype(bf), w_up_b=w_up_b.astype(bf), w_up_m=w_up_m.astype(bf),
        w_out=w_out.astype(bf),
        w_ffn_a=_pad_cols(w_ffn_in[:, :D_FF], d_ff_pad).astype(bf),
        w_ffn_u=_pad_cols(w_ffn_in[:, D_FF:], d_ff_pad).astype(bf),
        w_ffn_out=jnp.pad(w_ffn_out, ((0, d_ff_pad - D_FF), (0, 0))).astype(bf),
    )


def project_in(x2d, g_attn, W, tm):
    h = rms_cast(x2d, g_attn, min(tm, 512))
    z_main = matmul(h, W['w_main'], tm, COL_TILE, name="in_proj_main")
    mg = matmul(h, W['w_mg'], tm, COL_TILE, name="in_proj_mg")
    return z_main, mg


def merge_ffn(x2d, o_a, o_b, o_m, mg, W, g_ffn, tm):
    bf = jnp.bfloat16
    mix = upmix(o_a.astype(bf), o_b.astype(bf), o_m.astype(bf), W['w_up_a'], W['w_up_b'], W['w_up_m'], mg, tm, COL_TILE)
    x2 = matmul_residual(mix, W['w_out'], x2d, tm, COL_TILE)
    h2 = rms_cast(x2, g_ffn, min(tm, 512))
    act = swiglu_in(h2, W['w_ffn_a'], W['w_ffn_u'], tm, COL_TILE)
    tk = act.shape[1] // 4
    return matmul_residual_ksplit(act, W['w_ffn_out'], x2, tm, COL_TILE, tk)


def kernel(x_prompt, x_sample, cache_a_kv, cache_a_idx, cache_b_cmp, cache_b_slc, cache_b_win, cache_mem_kv, page_table, mem_prompt, g_attn, w_in, qk_gain_a, qk_gain_b, qk_gain_m, cmp_pe, cmp_w1, cmp_w2, rel_bias, g_mem, w_mem_kv, w_up_a, w_up_b, w_up_m, w_out, g_ffn, w_ffn_in, w_ffn_out):
    l = 0
    bias_a = rel_bias[:, :H_A]
    bias_b = rel_bias[:, H_A:]
    past_len = page_table.shape[1] * PAGE_SIZE
    B, T, D = x_prompt.shape
    DB, DS = x_sample.shape[:2]
    s_len = past_len + DS
    qpos_s = past_len + jnp.arange(DS)
    w_eff = cache_b_win.shape[2]
    kwpos_s = past_len - w_eff + jnp.arange(w_eff + DS)
    g_idx = jnp.arange(G_B)[:, None]
    W = prepare_weights(w_in[l], w_mem_kv[l], w_up_a[l], w_up_b[l], w_up_m[l], w_out[l], w_ffn_in[l], w_ffn_out[l])

    xp = x_prompt.reshape(B * T, D)
    z_main, mg = project_in(xp, g_attn[l], W, 1024)
    (q_a, kv_a, qi, wi, ki, q_b, cmp_kv, slc_kv, win_kv, gates_b, q_m) = split_main(z_main, B, T, qk_gain_a[l], qk_gain_b[l], qk_gain_m[l])
    o_a = dsa_prompt(q_a, qi, wi, kv_a, ki, bias_a)
    o_b = nsa_prompt(q_b, gates_b, cmp_kv, slc_kv, win_kv, cmp_pe[l], cmp_w1[l], cmp_w2[l], qk_gain_b[l, 1], bias_b)
    n_mem = mem_prompt.shape[1]
    hm = rms_cast(mem_prompt.reshape(B * n_mem, D), g_mem[l], 256)
    mkv = matmul(hm, W['w_mem_kv'], 256, COL_TILE, name="mem_kv").reshape(B, n_mem, H_M, 2, HD_M)
    mkv = norm_key(mkv, qk_gain_m[l, 1])
    o_m = mem_attend(q_m, mkv)
    y_p = merge_ffn(xp, o_a.reshape(B * T, -1), o_b.reshape(B * T, -1), o_m.reshape(B * T, -1), mg, W, g_ffn[l], 1024).reshape(B, T, D)
    p_out = (kv_a[None], ki[None], cmp_kv[None], slc_kv[None], win_kv[:, T - min(WINDOW, T):][None], mkv[None])

    xs = x_sample.reshape(DB * DS, D)
    z_main, mg = project_in(xs, g_attn[l], W, DB * DS)
    (q_a, kv_a, qi, wi, ki, q_b, cmp_kv, slc_kv, win_kv, gates_b, q_m) = split_main(z_main, DB, DS, qk_gain_a[l], qk_gain_b[l], qk_gain_m[l])
    pool_a = cache_a_kv[l]
    ki_all = jnp.concatenate([cache_a_idx[l][page_table].reshape(DB, past_len, D_I), ki], axis=1)
    o_a = dsa_block(q_a, qi, wi, qpos_s, ki_all, lambda idx: gather_paged(pool_a, page_table, kv_a, idx), bias_a, min(DSA_TOPK, s_len // 4))
    cmp_all = jnp.concatenate([cache_b_cmp[l][page_table].reshape(DB, past_len, G_B, 2, HD), cmp_kv], axis=1)
    kc, vc, ends = compress(cmp_all, cmp_pe[l], cmp_w1[l], cmp_w2[l], qk_gain_b[l, 1])
    sel_map = cmp_to_sel(kc.shape[1], -(-s_len // SEL_LEN))
    pool_s = cache_b_slc[l]
    win_all = jnp.concatenate([cache_b_win[l], win_kv], axis=1)
    o_b = nsa_block(q_b, gates_b, qpos_s, kc, vc, ends, sel_map, lambda idx: gather_paged(pool_s, page_table, slc_kv, idx, g_idx), win_all, kwpos_s, bias_b)
    o_m = mem_attend(q_m, cache_mem_kv[l])
    n_s = DB * DS
    y_s = merge_ffn(xs, o_a.reshape(n_s, -1), o_b.reshape(n_s, -1), o_m.reshape(n_s, -1), mg, W, g_ffn[l], n_s).reshape(DB, DS, D)
    s_out = (kv_a[None], ki[None], cmp_kv[None], slc_kv[None], win_all[:, win_all.shape[1] - w_eff:][None])
    return (y_p, y_s) + p_out + s_out
```

```python
import functools
import math

import numpy as np
import jax
import jax.numpy as jnp
from jax import lax
from jax.experimental import pallas as pl
from jax.experimental.pallas import tpu as pltpu

D_MODEL = 4096
PAGE_SIZE = 128
HD = 128
H_A = 12
G_A = 4
H_I = 32
D_I = 64
DSA_TOPK = 256
H_B = 12
G_B = 4
CMP_LEN = 32
CMP_STRIDE = 16
SEL_LEN = 64
N_SEL = 16
WINDOW = 512
N_MEM = 256
H_M = 4
HD_M = 256
N_BUCKETS = 32
MAX_DIST = 128
D_FF = -(-8 * D_MODEL // (3 * 256)) * 256
Q_BLOCK = 128
EPS = 1e-6
NEG = -1e30
BIG = 1e30
PROJ_SIZES = (H_A * HD, G_A * 2 * HD, H_I * D_I, H_I, D_I, H_B * HD, G_B * 2 * HD, G_B * 2 * HD, G_B * 2 * HD, 3 * H_B, H_M * HD_M, 3 * D_MODEL)
PROJ_OFFSETS = tuple(int(o) for o in np.concatenate([[0], np.cumsum(PROJ_SIZES)]))
D_MAIN = PROJ_OFFSETS[-2]
N_REP = H_A // G_A

VMEM_LIMIT_BYTES = 56 * 1024 * 1024
LANE = 128
COL_TILE = 512
ATTN_TILE = 256
MXU_DTYPE = jnp.bfloat16
F32 = jnp.float32
INT_MIN = -2 ** 31

COL_QI = 0
COL_QA = COL_QI + H_I * D_I
COL_KVA = COL_QA + H_A * HD
COL_QB = COL_KVA + G_A * 2 * HD
COL_CMP = COL_QB + H_B * HD
COL_SLC = COL_CMP + G_B * 2 * HD
COL_WIN = COL_SLC + G_B * 2 * HD
COL_QM = COL_WIN + G_B * 2 * HD
COL_SMALL = COL_QM + H_M * HD_M
SMALL_W = 256
OFF_WI, OFF_KI, OFF_GATE = 0, H_I, H_I + D_I
N_MAIN = -(-(COL_SMALL + SMALL_W) // COL_TILE) * COL_TILE


def _round_up(n, m):
    return -(-n // m) * m


def _cparams(*sem):
    return pltpu.CompilerParams(dimension_semantics=sem, vmem_limit_bytes=VMEM_LIMIT_BYTES)


def _dot(a, b):
    return jnp.dot(a, b, preferred_element_type=F32)


def _dot_nt(a, b):
    return lax.dot_general(a, b, (((1,), (1,)), ((), ())), preferred_element_type=F32)


def _float_key(x):
    b = pltpu.bitcast(x, jnp.int32)
    return b ^ ((b >> 31) & jnp.int32(0x7FFFFFFF))


def _np_float_key(v):
    b = int(np.array(v, np.float32).view(np.int32))
    return b ^ ((b >> 31) & 0x7FFFFFFF)


def _rms_cast_kernel(x_ref, g_ref, o_ref):
    x = x_ref[...]
    ms = jnp.mean(x * x, axis=-1, keepdims=True)
    o_ref[...] = ((x * lax.rsqrt(ms + EPS)) * g_ref[...]).astype(o_ref.dtype)


def rms_cast(x, g, tm):
    M, D = x.shape
    return pl.pallas_call(
        _rms_cast_kernel,
        out_shape=jax.ShapeDtypeStruct((M, D), MXU_DTYPE),
        grid=(M // tm,),
        in_specs=[pl.BlockSpec((tm, D), lambda i: (i, 0)), pl.BlockSpec((1, D), lambda i: (0, 0))],
        out_specs=pl.BlockSpec((tm, D), lambda i: (i, 0)),
        compiler_params=_cparams("parallel"),
        name="rms_cast",
    )(x, g.reshape(1, D))


def _mm_kernel(a_ref, w_ref, o_ref):
    o_ref[...] = _dot(a_ref[...], w_ref[...]).astype(o_ref.dtype)


def matmul(a, w, tm, tn, out_dtype=F32, name="matmul"):
    M, K = a.shape
    N = w.shape[1]
    return pl.pallas_call(
        _mm_kernel,
        out_shape=jax.ShapeDtypeStruct((M, N), out_dtype),
        grid=(M // tm, N // tn),
        in_specs=[pl.BlockSpec((tm, K), lambda i, j: (i, 0)), pl.BlockSpec((K, tn), lambda i, j: (0, j))],
        out_specs=pl.BlockSpec((tm, tn), lambda i, j: (i, j)),
        compiler_params=_cparams("parallel", "arbitrary"),
        name=name,
    )(a, w)


def _norm_cols(z, gain, m128, m256):
    tn = z.shape[1]
    sq = z * z
    ssq = [jnp.sum(sq[:, c * LANE:(c + 1) * LANE], axis=-1, keepdims=True) for c in range(tn // LANE)]
    cols = []
    for c in range(tn // LANE):
        sl = slice(c * LANE, (c + 1) * LANE)
        ms1 = ssq[c] * (1.0 / LANE)
        ms2 = (ssq[c - c % 2] + ssq[c - c % 2 + 1]) * (1.0 / (2 * LANE))
        ms = jnp.where(m256[:, sl] > 0, ms2, ms1)
        zc = z[:, sl]
        zn = (zc * lax.rsqrt(ms + EPS)) * gain[:, sl]
        cols.append(jnp.where((m128[:, sl] + m256[:, sl]) > 0, zn, zc))
    return jnp.concatenate(cols, axis=-1)


def _mm_norm_kernel(mode_ref, a_ref, w_ref, gain_ref, m128_ref, m256_ref, o_ref):
    j = pl.program_id(1)
    z = _dot(a_ref[...], w_ref[...])

    @pl.when(mode_ref[j] == 0)
    def _():
        o_ref[...] = z

    @pl.when(mode_ref[j] != 0)
    def _():
        o_ref[...] = _norm_cols(z, gain_ref[...], m128_ref[...], m256_ref[...])


def matmul_headnorm(a, w, gain, m128, m256, tile_mode, tm, tn, name):
    M, K = a.shape
    N = w.shape[1]
    vec = pl.BlockSpec((1, tn), lambda i, j, m: (0, j))
    return pl.pallas_call(
        _mm_norm_kernel,
        out_shape=jax.ShapeDtypeStruct((M, N), F32),
        grid_spec=pltpu.PrefetchScalarGridSpec(
            num_scalar_prefetch=1,
            grid=(M // tm, N // tn),
            in_specs=[pl.BlockSpec((tm, K), lambda i, j, m: (i, 0)), pl.BlockSpec((K, tn), lambda i, j, m: (0, j)), vec, vec, vec],
            out_specs=pl.BlockSpec((tm, tn), lambda i, j, m: (i, j)),
        ),
        compiler_params=_cparams("parallel", "arbitrary"),
        name=name,
    )(tile_mode, a, w, gain, m128, m256)


def _upmix_kernel(oa_ref, ob_ref, om_ref, wa_ref, wb_ref, wm_ref, g0_ref, g1_ref, g2_ref, o_ref):
    ya = _dot(oa_ref[...].astype(MXU_DTYPE), wa_ref[...])
    yb = _dot(ob_ref[...].astype(MXU_DTYPE), wb_ref[...])
    ym = _dot(om_ref[...].astype(MXU_DTYPE), wm_ref[...])
    mix = jax.nn.sigmoid(g0_ref[...]) * ya + jax.nn.sigmoid(g1_ref[...]) * yb + jax.nn.sigmoid(g2_ref[...]) * ym
    o_ref[...] = mix.astype(o_ref.dtype)


def upmix(o_a, o_b, o_m, w_a, w_b, w_m, mg, tm, tn):
    M = o_a.shape[0]
    D = w_a.shape[1]
    nj = D // tn
    a_spec = lambda k: pl.BlockSpec((tm, k), lambda i, j: (i, 0))
    w_spec = lambda k: pl.BlockSpec((k, tn), lambda i, j: (0, j))
    g_spec = lambda c: pl.BlockSpec((tm, tn), lambda i, j: (i, j + c * nj))
    return pl.pallas_call(
        _upmix_kernel,
        out_shape=jax.ShapeDtypeStruct((M, D), MXU_DTYPE),
        grid=(M // tm, nj),
        in_specs=[a_spec(o_a.shape[1]), a_spec(o_b.shape[1]), a_spec(o_m.shape[1]),
                  w_spec(w_a.shape[0]), w_spec(w_b.shape[0]), w_spec(w_m.shape[0]),
                  g_spec(0), g_spec(1), g_spec(2)],
        out_specs=pl.BlockSpec((tm, tn), lambda i, j: (i, j)),
        compiler_params=_cparams("parallel", "arbitrary"),
        name="upmix",
    )(o_a, o_b, o_m, w_a, w_b, w_m, mg, mg, mg)


def _mm_res_kernel(a_ref, w_ref, r_ref, o_ref):
    o_ref[...] = r_ref[...] + _dot(a_ref[...], w_ref[...])


def matmul_residual(a, w, r, tm, tn):
    M, K = a.shape
    N = w.shape[1]
    return pl.pallas_call(
        _mm_res_kernel,
        out_shape=jax.ShapeDtypeStruct((M, N), F32),
        grid=(M // tm, N // tn),
        in_specs=[pl.BlockSpec((tm, K), lambda i, j: (i, 0)), pl.BlockSpec((K, tn), lambda i, j: (0, j)),
                  pl.BlockSpec((tm, tn), lambda i, j: (i, j))],
        out_specs=pl.BlockSpec((tm, tn), lambda i, j: (i, j)),
        compiler_params=_cparams("parallel", "arbitrary"),
        name="matmul_residual",
    )(a, w, r)


def _swiglu_kernel(h_ref, wa_ref, wu_ref, o_ref):
    h = h_ref[...]
    a = _dot(h, wa_ref[...])
    u = _dot(h, wu_ref[...])
    o_ref[...] = (a * jax.nn.sigmoid(a) * u).astype(o_ref.dtype)


def swiglu_in(h, w_a, w_u, tm, tn):
    M, K = h.shape
    N = w_a.shape[1]
    return pl.pallas_call(
        _swiglu_kernel,
        out_shape=jax.ShapeDtypeStruct((M, N), MXU_DTYPE),
        grid=(M // tm, N // tn),
        in_specs=[pl.BlockSpec((tm, K), lambda i, j: (i, 0)), pl.BlockSpec((K, tn), lambda i, j: (0, j)),
                  pl.BlockSpec((K, tn), lambda i, j: (0, j))],
        out_specs=pl.BlockSpec((tm, tn), lambda i, j: (i, j)),
        compiler_params=_cparams("parallel", "arbitrary"),
        name="swiglu_in",
    )(h, w_a, w_u)


def _mm_res_acc_kernel(a_ref, w_ref, r_ref, o_ref, acc_ref):
    k = pl.program_id(2)

    @pl.when(k == 0)
    def _():
        acc_ref[...] = r_ref[...]

    acc_ref[...] += _dot(a_ref[...], w_ref[...])

    @pl.when(k == pl.num_programs(2) - 1)
    def _():
        o_ref[...] = acc_ref[...]


def matmul_residual_ksplit(a, w, r, tm, tn, tk):
    M, K = a.shape
    N = w.shape[1]
    return pl.pallas_call(
        _mm_res_acc_kernel,
        out_shape=jax.ShapeDtypeStruct((M, N), F32),
        grid=(M // tm, N // tn, K // tk),
        in_specs=[pl.BlockSpec((tm, tk), lambda i, j, k: (i, k)), pl.BlockSpec((tk, tn), lambda i, j, k: (k, j)),
                  pl.BlockSpec((tm, tn), lambda i, j, k: (i, j))],
        out_specs=pl.BlockSpec((tm, tn), lambda i, j, k: (i, j)),
        scratch_shapes=[pltpu.VMEM((tm, tn), F32)],
        compiler_params=_cparams("parallel", "arbitrary", "arbitrary"),
        name="matmul_residual_ksplit",
    )(a, w, r)


def _dsa_index_kernel(q_ref, small_ref, kd_ref, sc_ref, tau_ref, key_ref, *, topk, tq):
    i = pl.program_id(0)
    nk = sc_ref.shape[0]
    q = q_ref[...].astype(MXU_DTYPE)
    w = small_ref[:, OFF_WI:OFF_WI + H_I]
    row = lax.broadcasted_iota(jnp.int32, (tq, tq), 0)
    col = lax.broadcasted_iota(jnp.int32, (tq, tq), 1)
    neg_key = jnp.int32(_np_float_key(NEG))

    def score_tile(j, _):
        kd = kd_ref[j]
        acc = jnp.zeros((tq, tq), F32)
        for p in range(H_I // 2):
            r = _dot(q[:, p * 2 * D_I:(p + 1) * 2 * D_I], kd)
            acc = acc + jnp.maximum(r[:, :tq], 0.0) * w[:, 2 * p:2 * p + 1] + jnp.maximum(r[:, tq:], 0.0) * w[:, 2 * p + 1:2 * p + 2]
        sc_ref[j] = acc
        causal = (j * tq + col) <= (i * tq + row)
        key_ref[j] = jnp.where(causal, _float_key(acc), neg_key)
        return 0

    lax.fori_loop(0, i + 1, score_tile, 0)

    def zero_tile(j, _):
        sc_ref[j] = jnp.zeros((tq, tq), F32)
        return 0

    lax.fori_loop(i + 1, nk, zero_tile, 0)

    def bit_body(b, res_u):
        cand_u = res_u | jnp.left_shift(jnp.int32(1), 31 - b)
        cand_s = cand_u ^ jnp.int32(INT_MIN)

        def cnt_body(j, c):
            hit = jnp.where(key_ref[j] >= cand_s, 1.0, 0.0)
            part = hit[:, 0:LANE]
            for t in range(1, tq // LANE):
                part = part + hit[:, t * LANE:(t + 1) * LANE]
            return c + part

        cnt = lax.fori_loop(0, i + 1, cnt_body, jnp.zeros((tq, LANE), F32))
        total = jnp.sum(cnt, axis=-1, keepdims=True)
        return jnp.where(total >= topk, cand_u, res_u)

    res_u = lax.fori_loop(0, 32, bit_body, jnp.zeros((tq, 1), jnp.int32))
    tau_key = res_u ^ jnp.int32(INT_MIN)
    tau_bits = tau_key ^ ((tau_key >> 31) & jnp.int32(0x7FFFFFFF))
    tau_ref[...] = jnp.where(res_u == 0, -jnp.inf, pltpu.bitcast(tau_bits, F32))


def dsa_index(z, kd, topk, tq):
    T = z.shape[0]
    nk = kd.shape[0]
    return pl.pallas_call(
        functools.partial(_dsa_index_kernel, topk=topk, tq=tq),
        out_shape=(jax.ShapeDtypeStruct((nk, T, tq), F32), jax.ShapeDtypeStruct((T, 1), F32)),
        grid=(T // tq,),
        in_specs=[pl.BlockSpec((tq, H_I * D_I), lambda i: (i, COL_QI // (H_I * D_I))),
                  pl.BlockSpec((tq, SMALL_W), lambda i: (i, COL_SMALL // SMALL_W)),
                  pl.BlockSpec(kd.shape, lambda i: (0, 0, 0))],
        out_specs=(pl.BlockSpec((nk, tq, tq), lambda i: (0, i, 0)), pl.BlockSpec((tq, 1), lambda i: (i, 0))),
        scratch_shapes=[pltpu.VMEM((nk, tq, tq), jnp.int32)],
        compiler_params=_cparams("parallel"),
        name="dsa_index",
    )(z, z, kd)


def _flash_tile(q, k, v, bias, masks, state):
    m, l, acc = state
    s = _dot_nt(q, k) * (HD ** -0.5) + bias
    for mask in masks:
        s = jnp.where(mask, s, NEG)
    m_new = jnp.maximum(m, jnp.max(s, axis=-1, keepdims=True))
    alpha = jnp.exp(m - m_new)
    p = jnp.exp(s - m_new)
    l = alpha * l + jnp.sum(p, axis=-1, keepdims=True)
    acc = alpha * acc + _dot(p.astype(MXU_DTYPE), v)
    return m_new, l, acc


def _pattn_kernel(bfar_ref, *refs, mode, tq):
    q_refs = refs[0:N_REP]
    k_ref, v_ref, bias_ref = refs[N_REP:N_REP + 3]
    rest = refs[N_REP + 3:]
    if mode == 'dsa':
        sc_ref, tau_ref, o_ref = rest
    elif mode == 'slc':
        sel_ref, e_ref, small_ref, prev_ref, o_ref = rest
    else:
        small_ref, prev_ref, o_ref = rest
    g = pl.program_id(0)
    i = pl.program_id(1)
    q = [r[...].astype(MXU_DTYPE) for r in q_refs]
    row = lax.broadcasted_iota(jnp.int32, (tq, tq), 0)
    col = lax.broadcasted_iota(jnp.int32, (tq, tq), 1)
    if mode == 'dsa':
        tau = tau_ref[...]
    if mode == 'slc':
        sel = sel_ref[...].astype(MXU_DTYPE)

    def process(j, state, kind):
        start = pl.multiple_of(j * tq, tq)
        kt = k_ref[pl.ds(start, tq), :].astype(MXU_DTYPE)
        vt = v_ref[pl.ds(start, tq), :].astype(MXU_DTYPE)
        masks = []
        if mode == 'dsa':
            masks.append(sc_ref[j] >= tau)
        elif mode == 'slc':
            masks.append(_dot(sel, e_ref[j]) > 0.5)
        elif kind == 'edge':
            masks.append(row < col)
        if kind == 'diag':
            masks.append(row >= col)
        out = []
        for r in range(N_REP):
            bias = bfar_ref[g * N_REP + r] if kind in ('far', 'edge') else bias_ref[0, r, 0 if kind == 'diag' else 1]
            out.append(_flash_tile(q[r], kt, vt, bias, masks, state[r]))
        return tuple(out)

    state = tuple((jnp.full((tq, 1), NEG, F32), jnp.zeros((tq, 1), F32), jnp.zeros((tq, HD), F32)) for _ in range(N_REP))
    if mode == 'win':
        n_win = WINDOW // tq
        state = lax.cond(i >= n_win, lambda s: process(i - n_win, s, 'edge'), lambda s: s, state)
        for d in range(n_win - 1, 1, -1):
            state = lax.cond(i >= d, functools.partial(lambda s, d: process(i - d, s, 'far'), d=d), lambda s: s, state)
    else:
        state = lax.fori_loop(0, i - 1, lambda j, s: process(j, s, 'far'), state)
    state = lax.cond(i >= 1, lambda s: process(i - 1, s, 'near'), lambda s: s, state)
    state = process(i, state, 'diag')

    if mode != 'dsa':
        branch = 1 if mode == 'slc' else 2
        gates = jax.nn.sigmoid(small_ref[...])
        lane = lax.broadcasted_iota(jnp.int32, gates.shape, 1)
    for r in range(N_REP):
        m, l, acc = state[r]
        o = acc / l
        if mode != 'dsa':
            gcol = OFF_GATE + (g * N_REP + r) * 3 + branch
            o = prev_ref[:, r * HD:(r + 1) * HD] + o * jnp.sum(jnp.where(lane == gcol, gates, 0.0), axis=-1, keepdims=True)
        o_ref[:, r * HD:(r + 1) * HD] = o


def prompt_attention(mode, z, q_col, kv_col, bias_near, bias_far, tq, extra):
    T = z.shape[0]
    G = G_A
    qb = q_col // HD
    kb = kv_col // HD
    q_specs = [pl.BlockSpec((tq, HD), functools.partial(lambda g, i, b, r: (i, qb + g * N_REP + r), r=r)) for r in range(N_REP)]
    in_specs = q_specs + [
        pl.BlockSpec((T, HD), lambda g, i, b: (0, kb + 2 * g)),
        pl.BlockSpec((T, HD), lambda g, i, b: (0, kb + 2 * g + 1)),
        pl.BlockSpec((1, N_REP, 2, tq, tq), lambda g, i, b: (g, 0, 0, 0, 0)),
    ]
    args = [z] * N_REP + [z, z, bias_near]
    small_spec = pl.BlockSpec((tq, SMALL_W), lambda g, i, b: (i, COL_SMALL // SMALL_W))
    prev_spec = pl.BlockSpec((tq, N_REP * HD), lambda g, i, b: (i, g))
    if mode == 'dsa':
        scores, tau = extra
        nk = scores.shape[0]
        in_specs += [pl.BlockSpec((nk, tq, tq), lambda g, i, b: (0, i, 0)), pl.BlockSpec((tq, 1), lambda g, i, b: (i, 0))]
        args += [scores, tau]
    elif mode == 'slc':
        sel, e, prev = extra
        nsb = e.shape[1]
        in_specs += [pl.BlockSpec((tq, nsb), lambda g, i, b: (i, g)), pl.BlockSpec(e.shape, lambda g, i, b: (0, 0, 0)), small_spec, prev_spec]
        args += [sel, e, z, prev]
    else:
        (prev,) = extra
        in_specs += [small_spec, prev_spec]
        args += [z, prev]
    return pl.pallas_call(
        functools.partial(_pattn_kernel, mode=mode, tq=tq),
        out_shape=jax.ShapeDtypeStruct((T, G * N_REP * HD), F32),
        grid_spec=pltpu.PrefetchScalarGridSpec(
            num_scalar_prefetch=1,
            grid=(G, T // tq),
            in_specs=in_specs,
            out_specs=pl.BlockSpec((tq, N_REP * HD), lambda g, i, b: (i, g)),
        ),
        compiler_params=_cparams("parallel", "parallel"),
        name="attn_" + mode,
    )(bias_far, *args)


def _compress_kernel(x_ref, w1_ref, pe_ref, w2_ref, gain_ref, o_ref):
    c = pl.program_id(0) % 2
    nch = x_ref.shape[0]
    w1 = w1_ref[0]
    hid0 = jnp.zeros((nch, HD), F32)
    hid1 = jnp.zeros((nch, HD), F32)
    for s in range(CMP_STRIDE):
        xs = x_ref[:, s, :].astype(MXU_DTYPE)
        hid0 = hid0 + _dot(xs, w1[0, s].astype(MXU_DTYPE))
        hid1 = hid1 + _dot(xs, w1[1, s].astype(MXU_DTYPE))
    pe_term = _dot(pe_ref[0].astype(MXU_DTYPE), w1.reshape(CMP_LEN * HD, HD).astype(MXU_DTYPE))
    hid = pe_term + hid0 + pltpu.roll(hid1, nch - 1, 0)
    out = _dot((hid * jax.nn.sigmoid(hid)).astype(MXU_DTYPE), w2_ref[0].astype(MXU_DTYPE))
    ms = jnp.mean(out * out, axis=-1, keepdims=True)
    normed = (out * lax.rsqrt(ms + EPS)) * gain_ref[...]
    o_ref[0] = jnp.where(c == 0, normed, out)


def compress_blocks(z3, col, cmp_pe, cmp_w1, cmp_w2, gain_c):
    nch = z3.shape[0]
    r = CMP_LEN // CMP_STRIDE
    w1 = cmp_w1.reshape(2, r, CMP_STRIDE, HD, HD)
    pe = cmp_pe.reshape(CMP_LEN, 2, HD).transpose(1, 0, 2).reshape(2, 1, CMP_LEN * HD)
    cb = col // HD
    return pl.pallas_call(
        _compress_kernel,
        out_shape=jax.ShapeDtypeStruct((G_B * 2, nch, HD), F32),
        grid=(G_B * 2,),
        in_specs=[pl.BlockSpec((nch, CMP_STRIDE, HD), lambda gc: (0, 0, cb + gc)),
                  pl.BlockSpec((1, r, CMP_STRIDE, HD, HD), lambda gc: (gc % 2, 0, 0, 0, 0)),
                  pl.BlockSpec((1, 1, CMP_LEN * HD), lambda gc: (gc % 2, 0, 0)),
                  pl.BlockSpec((1, HD, HD), lambda gc: (gc % 2, 0, 0)),
                  pl.BlockSpec((1, HD), lambda gc: (0, 0))],
        out_specs=pl.BlockSpec((1, nch, HD), lambda gc: (gc, 0, 0)),
        compiler_params=_cparams("parallel"),
        name="nsa_compress",
    )(z3, w1, pe, cmp_w2, gain_c.reshape(1, HD))


def _nsa_cmp_kernel(*refs, tq, n_sel):
    q_refs = refs[0:N_REP]
    kc_ref, vc_ref, map_ref, small_ref, oc_ref, sel_ref = refs[N_REP:]
    g = pl.program_id(0)
    i = pl.program_id(1)
    kc = kc_ref[0].astype(MXU_DTYPE)
    vc = vc_ref[0].astype(MXU_DTYPE)
    smap = map_ref[...]
    nch = kc.shape[0]
    nsb = smap.shape[1]
    t = i * tq + lax.broadcasted_iota(jnp.int32, (tq, nch), 0)
    n = lax.broadcasted_iota(jnp.int32, (tq, nch), 1)
    valid = (n * CMP_STRIDE + CMP_LEN - 1) <= t
    gates = jax.nn.sigmoid(small_ref[...])
    lane = lax.broadcasted_iota(jnp.int32, gates.shape, 1)
    imp = jnp.zeros((tq, nsb), F32)
    for r in range(N_REP):
        q = q_refs[r][...].astype(MXU_DTYPE)
        lm = jnp.where(valid, _dot_nt(q, kc) * (HD ** -0.5), NEG)
        m = jnp.max(lm, axis=-1, keepdims=True)
        p = jnp.where(valid, jnp.exp(lm - m), 0.0)
        den = jnp.sum(p, axis=-1, keepdims=True)
        pc = (p / jnp.where(den > 0.0, den, 1.0)).astype(MXU_DTYPE)
        imp = imp + _dot(pc, smap)
        gcol = OFF_GATE + (g * N_REP + r) * 3
        gate = jnp.sum(jnp.where(lane == gcol, gates, 0.0), axis=-1, keepdims=True)
        oc_ref[:, r * HD:(r + 1) * HD] = _dot(pc, vc) * gate

    tpos = i * tq + lax.broadcasted_iota(jnp.int32, (tq, nsb), 0)
    blk = lax.broadcasted_iota(jnp.int32, (tq, nsb), 1)
    cur = tpos // SEL_LEN
    start_ok = blk <= cur
    forced_or_imp = jnp.where(blk == 0, BIG, jnp.where(blk == cur, BIG, jnp.where(blk == cur - 1, BIG, imp)))
    score = jnp.where(start_ok, forced_or_imp, NEG)
    blk_f = blk.astype(F32)
    sel = jnp.zeros((tq, nsb), F32)
    for _ in range(n_sel):
        best = jnp.max(score, axis=-1, keepdims=True)
        first = jnp.min(jnp.where(score == best, blk_f, float(nsb)), axis=-1, keepdims=True)
        hit = blk_f == first
        sel = jnp.where(hit, 1.0, sel)
        score = jnp.where(hit, -jnp.inf, score)
    sel_ref[...] = sel


def nsa_cmp_attention(z, q_col, kcv, sel_map, tq):
    T = z.shape[0]
    nch = kcv.shape[1]
    nsb = sel_map.shape[1]
    qb = q_col // HD
    q_specs = [pl.BlockSpec((tq, HD), functools.partial(lambda g, i, r: (i, qb + g * N_REP + r), r=r)) for r in range(N_REP)]
    return pl.pallas_call(
        functools.partial(_nsa_cmp_kernel, tq=tq, n_sel=min(N_SEL, nsb)),
        out_shape=(jax.ShapeDtypeStruct((T, H_B * HD), F32), jax.ShapeDtypeStruct((T, G_B * nsb), F32)),
        grid=(G_B, T // tq),
        in_specs=q_specs + [pl.BlockSpec((1, nch, HD), lambda g, i: (2 * g, 0, 0)),
                            pl.BlockSpec((1, nch, HD), lambda g, i: (2 * g + 1, 0, 0)),
                            pl.BlockSpec((nch, nsb), lambda g, i: (0, 0)),
                            pl.BlockSpec((tq, SMALL_W), lambda g, i: (i, COL_SMALL // SMALL_W))],
        out_specs=(pl.BlockSpec((tq, N_REP * HD), lambda g, i: (i, g)), pl.BlockSpec((tq, nsb), lambda g, i: (i, g))),
        compiler_params=_cparams("parallel", "parallel"),
        name="nsa_cmp",
    )(*([z] * N_REP), kcv, kcv, sel_map, z)


def _mem_attn_kernel(q_ref, k_ref, v_ref, o_ref):
    q = q_ref[0].astype(MXU_DTYPE)
    s = _dot_nt(q, k_ref[0].astype(MXU_DTYPE)) * (HD_M ** -0.5)
    p = jnp.exp(s - jnp.max(s, axis=-1, keepdims=True))
    p = p / jnp.sum(p, axis=-1, keepdims=True)
    o_ref[0] = _dot(p.astype(MXU_DTYPE), v_ref[0].astype(MXU_DTYPE))


def mem_attention(z3, q_col, mkv3, tq):
    B, Tq, _ = z3.shape
    n_mem = mkv3.shape[1]
    qb = q_col // HD_M
    return pl.pallas_call(
        _mem_attn_kernel,
        out_shape=jax.ShapeDtypeStruct((B, Tq, H_M * HD_M), F32),
        grid=(B, H_M, Tq // tq),
        in_specs=[pl.BlockSpec((1, tq, HD_M), lambda b, h, i: (b, i, qb + h)),
                  pl.BlockSpec((1, n_mem, HD_M), lambda b, h, i: (b, 0, 2 * h)),
                  pl.BlockSpec((1, n_mem, HD_M), lambda b, h, i: (b, 0, 2 * h + 1))],
        out_specs=pl.BlockSpec((1, tq, HD_M), lambda b, h, i: (b, i, h)),
        compiler_params=_cparams("parallel", "parallel", "arbitrary"),
        name="mem_attn",
    )(z3, mkv3, mkv3)


def t5_bucket(dist):
    max_exact = N_BUCKETS // 2
    n = jnp.maximum(dist, 0)
    nf = jnp.maximum(n, 1).astype(jnp.float32)
    large = max_exact + (jnp.log(nf / max_exact) / math.log(MAX_DIST / max_exact) * (N_BUCKETS - max_exact)).astype(jnp.int32)
    return jnp.where(n < max_exact, n, jnp.minimum(large, N_BUCKETS - 1))


def masked_softmax(logits, valid):
    p = jax.nn.softmax(jnp.where(valid, logits.astype(jnp.float32), NEG), axis=-1)
    return jnp.where(valid, p, 0.0)


def rmsnorm(x, g):
    xf = x.astype(jnp.float32)
    y = xf * lax.rsqrt(jnp.mean(xf * xf, axis=-1, keepdims=True) + EPS)
    return (y * g.astype(jnp.float32)).astype(x.dtype)


def gather_seq(rows, idx, g=None):
    if g is None:
        return jax.vmap(lambda r, i: r[i])(rows, idx)
    return jax.vmap(lambda r, i: r[i, g])(rows, idx)


def gather_paged(pool, page_table, new_rows, idx, g=None):
    past_len = page_table.shape[1] * PAGE_SIZE
    ic = jnp.clip(idx, 0, past_len - 1)
    page = jax.vmap(lambda pt, i: pt[i // PAGE_SIZE])(page_table, ic)
    off = ic % PAGE_SIZE
    past = pool[page, off] if g is None else pool[page, off, g]
    new = gather_seq(new_rows, jnp.clip(idx - past_len, 0, new_rows.shape[1] - 1), g)
    is_past = (idx < past_len).reshape(idx.shape + (1,) * (past.ndim - idx.ndim))
    return jnp.where(is_past, past, new)


def dsa_block(q, qi, wi, qpos, kidx, gather_kv, bias_a, topk):
    B, Tq, G, R, _ = q.shape
    S = kidx.shape[1]
    sc = jax.nn.relu(jnp.einsum('bthd,bsd->bths', qi, kidx).astype(jnp.float32))
    sc = jnp.einsum('bths,bth->bts', sc, wi.astype(jnp.float32))
    causal = jnp.arange(S)[None, :] <= qpos[:, None]
    _, idx = lax.top_k(jnp.where(causal[None], sc, NEG), topk)
    valid = (idx <= qpos[None, :, None])[:, :, None, None, :]
    kv = gather_kv(idx)
    k, v = kv[..., 0, :], kv[..., 1, :]
    logits = jnp.einsum('btgrd,btkgd->btgrk', q, k).astype(jnp.float32) * HD ** -0.5
    bias = bias_a[t5_bucket(qpos[None, :, None] - idx)].reshape(B, Tq, topk, G, R).transpose(0, 1, 3, 4, 2)
    p = masked_softmax(logits + bias, valid)
    o = jnp.einsum('btgrk,btkgd->btgrd', p.astype(v.dtype), v)
    return o.reshape(B, Tq, G * R * HD)


def compress(rows, cmp_pe, cmp_w1, cmp_w2, gain_c):
    B, S = rows.shape[:2]
    r = CMP_LEN // CMP_STRIDE
    n_chunk = -(-S // CMP_STRIDE)
    nc = n_chunk - r + 1
    rows = jnp.pad(rows, ((0, 0), (0, n_chunk * CMP_STRIDE - S), (0, 0), (0, 0), (0, 0)))
    ch = rows.reshape(B, n_chunk, CMP_STRIDE, G_B, 2, HD)
    w1 = cmp_w1.reshape(2, r, CMP_STRIDE, HD, HD)
    pe = cmp_pe.reshape(r, CMP_STRIDE, 2, HD)
    hid = jnp.einsum('jscd,cjsde->ce', pe, w1)[None, None, None]
    for j in range(r):
        hid = hid + jnp.einsum('bnsgcd,csde->bngce', ch[:, j:j + nc], w1[:, j])
    out = jnp.einsum('bngce,ced->bngcd', jax.nn.silu(hid), cmp_w2)
    kc = rmsnorm(out[..., 0, :], gain_c)
    vc = out[..., 1, :]
    ends = jnp.arange(nc) * CMP_STRIDE + CMP_LEN - 1
    return kc, vc, ends


def cmp_to_sel(n_cmp, n_sel_blocks):
    cs = jnp.arange(n_cmp)[:, None] * CMP_STRIDE
    ss = jnp.arange(n_sel_blocks)[None, :] * SEL_LEN
    return ((cs < ss + SEL_LEN) & (cs + CMP_LEN > ss)).astype(jnp.float32)


def nsa_block(q, gates, qpos, kc, vc, cmp_end, sel_map, gather_slc, wkv, kwpos, bias_b):
    B, Tq, G, R, _ = q.shape
    scale = HD ** -0.5
    bias_g = bias_b.reshape(N_BUCKETS, G, R)
    lc = jnp.einsum('btgrd,bngd->btgrn', q, kc).astype(jnp.float32) * scale
    pc = masked_softmax(lc, (cmp_end[None, :] <= qpos[:, None])[None, :, None, None, :])
    oc = jnp.einsum('btgrn,bngd->btgrd', pc.astype(vc.dtype), vc)
    imp = jnp.einsum('btgrn,ns->btgs', pc, sel_map)
    blk = jnp.arange(sel_map.shape[1])
    start_ok = (blk[None, :] * SEL_LEN <= qpos[:, None])[None, :, None, :]
    cur = (qpos // SEL_LEN)[:, None]
    forced = ((blk[None, :] == 0) | (blk[None, :] == cur) | (blk[None, :] == cur - 1))[None, :, None, :]
    score = jnp.where(start_ok, jnp.where(forced, -NEG, imp), NEG)
    n_sel = min(N_SEL, sel_map.shape[1])
    _, sel = lax.top_k(score, n_sel)
    tok = (sel[..., None] * SEL_LEN + jnp.arange(SEL_LEN)).reshape(B, Tq, G, n_sel * SEL_LEN)
    skv = gather_slc(tok)
    ls = jnp.einsum('btgrd,btgmd->btgrm', q, skv[..., 0, :]).astype(jnp.float32) * scale
    bs = bias_g[t5_bucket(qpos[None, :, None, None] - tok), jnp.arange(G)[:, None]]
    ps = masked_softmax(ls + jnp.moveaxis(bs, -1, 3), (tok <= qpos[None, :, None, None])[:, :, :, None, :])
    o_s = jnp.einsum('btgrm,btgmd->btgrd', ps.astype(skv.dtype), skv[..., 1, :])
    lw = jnp.einsum('btgrd,bsgd->btgrs', q, wkv[..., 0, :]).astype(jnp.float32) * scale
    dist = qpos[:, None] - kwpos[None, :]
    wvalid = (dist >= 0) & (dist < WINDOW) & (kwpos[None, :] >= 0)
    bw = bias_g[t5_bucket(dist)].transpose(0, 2, 3, 1)[None]
    pw = masked_softmax(lw + bw, wvalid[None, :, None, None, :])
    ow = jnp.einsum('btgrs,bsgd->btgrd', pw.astype(wkv.dtype), wkv[..., 1, :])
    gt = jax.nn.sigmoid(gates.astype(jnp.float32)).reshape(B, Tq, G, R, 3)
    o = gt[..., 0:1] * oc + gt[..., 1:2] * o_s + gt[..., 2:3] * ow
    return o.reshape(B, Tq, G * R * HD).astype(q.dtype)


def toeplitz_bias_tiles(bias_tab, tq):
    ii = jnp.arange(tq)[:, None]
    jj = jnp.arange(tq)[None, :]
    tiles = jnp.stack([bias_tab[t5_bucket(ii - jj)], bias_tab[t5_bucket(tq + ii - jj)]])
    return tiles.transpose(3, 0, 1, 2).reshape(G_A, N_REP, 2, tq, tq)


def block_expand_matrix(n_blocks, n_tiles, tk):
    b = jnp.arange(n_blocks)[None, :, None]
    key = (jnp.arange(n_tiles)[:, None, None] * tk + jnp.arange(tk)[None, None, :])
    return (key // SEL_LEN == b).astype(MXU_DTYPE)


def indexer_key_tiles(ki, tk):
    S = ki.shape[0]
    kt = ki.reshape(S // tk, tk, D_I).transpose(0, 2, 1).astype(MXU_DTYPE)
    zero = jnp.zeros_like(kt)
    return jnp.concatenate([jnp.concatenate([kt, zero], axis=2), jnp.concatenate([zero, kt], axis=2)], axis=1)


def _pad_cols(w, n):
    return jnp.pad(w, ((0, 0), (0, n - w.shape[1])))


def _seg(w, idx):
    return w[:, PROJ_OFFSETS[idx]:PROJ_OFFSETS[idx + 1]]


def prepare_weights(w_in, w_mem_kv, w_up_a, w_up_b, w_up_m, w_out, w_ffn_in, w_ffn_out):
    bf = MXU_DTYPE
    d_ff_pad = _round_up(D_FF, COL_TILE)
    main = jnp.concatenate([_seg(w_in, 2), _seg(w_in, 0), _seg(w_in, 1), _seg(w_in, 5), _seg(w_in, 6), _seg(w_in, 7),
                            _seg(w_in, 8), _seg(w_in, 10), _seg(w_in, 3), _seg(w_in, 4), _seg(w_in, 9)], axis=1)
    return dict(
        w_main=_pad_cols(main, N_MAIN).astype(bf),
        w_mg=_seg(w_in, 11).astype(bf),
        w_mem_kv=w_mem_kv.astype(bf),
        w_up_a=w_up_a.astype(bf), w_up_b=w_up_b.astype(bf), w_up_m=w_up_m.astype(bf),
        w_out=w_out.astype(bf),
        w_ffn_a=_pad_cols(w_ffn_in[:, :D_FF], d_ff_pad).astype(bf),
        w_ffn_u=_pad_cols(w_ffn_in[:, D_FF:], d_ff_pad).astype(bf),
        w_ffn_out=jnp.pad(w_ffn_out, ((0, d_ff_pad - D_FF), (0, 0))).astype(bf),
    )


def main_norm_vectors(qk_gain_a, qk_gain_b, qk_gain_m):
    one, zero = jnp.ones((HD,), F32), jnp.zeros((HD,), F32)

    def kv(gk):
        return jnp.tile(jnp.concatenate([gk, zero]), G_A), jnp.tile(jnp.concatenate([one, zero]), G_A)

    kva, fa = kv(qk_gain_a[1])
    slc, fs = kv(qk_gain_b[2])
    win, fw = kv(qk_gain_b[3])
    z = lambda n: jnp.zeros((n,), F32)
    gain = jnp.concatenate([z(H_I * D_I), jnp.tile(qk_gain_a[0], H_A), kva, jnp.tile(qk_gain_b[0], H_B), z(G_B * 2 * HD), slc, win,
                            jnp.tile(qk_gain_m[0], H_M), z(N_MAIN - COL_SMALL)])
    m128 = jnp.concatenate([z(H_I * D_I), jnp.ones((H_A * HD,), F32), fa, jnp.ones((H_B * HD,), F32), z(G_B * 2 * HD), fs, fw,
                            z(H_M * HD_M), z(N_MAIN - COL_SMALL)])
    m256 = jnp.concatenate([z(COL_QM), jnp.ones((H_M * HD_M,), F32), z(N_MAIN - COL_SMALL)])
    cols = np.arange(N_MAIN)
    has = ((cols >= COL_QA) & (cols < COL_CMP)) | ((cols >= COL_SLC) & (cols < COL_SMALL))
    tile_mode = jnp.asarray(has.reshape(-1, COL_TILE).any(axis=1).astype(np.int32))
    return gain.reshape(1, -1), m128.reshape(1, -1), m256.reshape(1, -1), tile_mode


def project_in(x2d, g_attn, W, norm_vecs, tm):
    h = rms_cast(x2d, g_attn, min(tm, 512))
    gain, m128, m256, tile_mode = norm_vecs
    z_main = matmul_headnorm(h, W['w_main'], gain, m128, m256, tile_mode, tm, COL_TILE, "in_proj_main")
    mg = matmul(h, W['w_mg'], tm, COL_TILE, name="in_proj_mg")
    return z_main, mg


def merge_ffn(x2d, o_a, o_b, o_m, mg, W, g_ffn, tm):
    mix = upmix(o_a, o_b, o_m, W['w_up_a'], W['w_up_b'], W['w_up_m'], mg, min(tm, 512), COL_TILE)
    x2 = matmul_residual(mix, W['w_out'], x2d, tm, COL_TILE)
    h2 = rms_cast(x2, g_ffn, min(tm, 512))
    act = swiglu_in(h2, W['w_ffn_a'], W['w_ffn_u'], tm, COL_TILE)
    tk = act.shape[1] // 4
    return matmul_residual_ksplit(act, W['w_ffn_out'], x2, tm, COL_TILE, tk)


def prompt_mixers(z, mkv, rel_bias, cmp_pe, cmp_w1, cmp_w2, gain_c, tq):
    T = z.shape[0]
    nk = T // tq
    bias_a, bias_b = rel_bias[:, :H_A], rel_bias[:, H_A:]
    far_a, far_b = bias_a[N_BUCKETS - 1], bias_b[N_BUCKETS - 1]
    near_a, near_b = toeplitz_bias_tiles(bias_a, tq), toeplitz_bias_tiles(bias_b, tq)
    ki = z[:, COL_SMALL + OFF_KI:COL_SMALL + OFF_KI + D_I]
    scores, tau = dsa_index(z, indexer_key_tiles(ki, tq), min(DSA_TOPK, T // 4), tq)
    o_a = prompt_attention('dsa', z, COL_QA, COL_KVA, near_a, far_a, tq, (scores, tau))
    nch = T // CMP_STRIDE
    nsb = T // SEL_LEN
    kcv = compress_blocks(z.reshape(nch, CMP_STRIDE, z.shape[1]), COL_CMP, cmp_pe, cmp_w1, cmp_w2, gain_c)
    sel_map = cmp_to_sel(nch, nsb).astype(MXU_DTYPE)
    oc, sel = nsa_cmp_attention(z, COL_QB, kcv, sel_map, tq)
    ocs = prompt_attention('slc', z, COL_QB, COL_SLC, near_b, far_b, tq, (sel, block_expand_matrix(nsb, nk, tq), oc))
    o_b = prompt_attention('win', z, COL_QB, COL_WIN, near_b, far_b, tq, (ocs,))
    o_m = mem_attention(z[None], COL_QM, mkv[None], tq)[0]
    return o_a, o_b, o_m


def kernel(x_prompt, x_sample, cache_a_kv, cache_a_idx, cache_b_cmp, cache_b_slc, cache_b_win, cache_mem_kv, page_table, mem_prompt, g_attn, w_in, qk_gain_a, qk_gain_b, qk_gain_m, cmp_pe, cmp_w1, cmp_w2, rel_bias, g_mem, w_mem_kv, w_up_a, w_up_b, w_up_m, w_out, g_ffn, w_ffn_in, w_ffn_out):
    l = 0
    bias_a = rel_bias[:, :H_A]
    bias_b = rel_bias[:, H_A:]
    past_len = page_table.shape[1] * PAGE_SIZE
    B, T, D = x_prompt.shape
    assert B == 1
    DB, DS = x_sample.shape[:2]
    s_len = past_len + DS
    qpos_s = past_len + jnp.arange(DS)
    w_eff = cache_b_win.shape[2]
    kwpos_s = past_len - w_eff + jnp.arange(w_eff + DS)
    g_idx = jnp.arange(G_B)[:, None]
    W = prepare_weights(w_in[l], w_mem_kv[l], w_up_a[l], w_up_b[l], w_up_m[l], w_out[l], w_ffn_in[l], w_ffn_out[l])
    norm_vecs = main_norm_vectors(qk_gain_a[l], qk_gain_b[l], qk_gain_m[l])

    xp = x_prompt.reshape(T, D)
    z, mg = project_in(xp, g_attn[l], W, norm_vecs, 1024)
    n_mem = mem_prompt.shape[1]
    hm = rms_cast(mem_prompt.reshape(n_mem, D), g_mem[l], n_mem)
    kgain = jnp.tile(jnp.concatenate([qk_gain_m[l, 1], jnp.zeros((HD_M,), F32)]), H_M).reshape(1, -1)
    kflag = jnp.tile(jnp.concatenate([jnp.ones((HD_M,), F32), jnp.zeros((HD_M,), F32)]), H_M).reshape(1, -1)
    mkv = matmul_headnorm(hm, W['w_mem_kv'], kgain, jnp.zeros_like(kflag), kflag,
                          jnp.ones((kflag.shape[1] // COL_TILE,), jnp.int32), n_mem, COL_TILE, "mem_kv")
    o_a, o_b, o_m = prompt_mixers(z, mkv, rel_bias, cmp_pe[l], cmp_w1[l], cmp_w2[l], qk_gain_b[l, 1], ATTN_TILE)
    y_p = merge_ffn(xp, o_a, o_b, o_m, mg, W, g_ffn[l], 1024).reshape(B, T, D)
    kv5 = lambda c, rows: z[rows, c:c + G_A * 2 * HD].reshape(1, 1, -1, G_A, 2, HD)
    p_out = (kv5(COL_KVA, slice(None)), z[:, COL_SMALL + OFF_KI:COL_SMALL + OFF_KI + D_I].reshape(1, 1, T, D_I),
             kv5(COL_CMP, slice(None)), kv5(COL_SLC, slice(None)), kv5(COL_WIN, slice(T - min(WINDOW, T), T)),
             mkv.reshape(1, 1, n_mem, H_M, 2, HD_M))

    n_s = DB * DS
    xs = x_sample.reshape(n_s, D)
    zs, mg = project_in(xs, g_attn[l], W, norm_vecs, n_s)
    seg = lambda c, n: zs[:, c:c + n].reshape(DB, DS, n)
    q_a = seg(COL_QA, H_A * HD).reshape(DB, DS, G_A, N_REP, HD)
    kv_a = seg(COL_KVA, G_A * 2 * HD).reshape(DB, DS, G_A, 2, HD)
    qi = seg(COL_QI, H_I * D_I).reshape(DB, DS, H_I, D_I)
    wi = seg(COL_SMALL + OFF_WI, H_I)
    ki = seg(COL_SMALL + OFF_KI, D_I)
    gates_b = seg(COL_SMALL + OFF_GATE, 3 * H_B)
    q_b = seg(COL_QB, H_B * HD).reshape(DB, DS, G_B, N_REP, HD)
    cmp_kv = seg(COL_CMP, G_B * 2 * HD).reshape(DB, DS, G_B, 2, HD)
    slc_kv = seg(COL_SLC, G_B * 2 * HD).reshape(DB, DS, G_B, 2, HD)
    win_kv = seg(COL_WIN, G_B * 2 * HD).reshape(DB, DS, G_B, 2, HD)
    pool_a = cache_a_kv[l]
    ki_all = jnp.concatenate([cache_a_idx[l][page_table].reshape(DB, past_len, D_I), ki], axis=1)
    o_a = dsa_block(q_a, qi, wi, qpos_s, ki_all, lambda idx: gather_paged(pool_a, page_table, kv_a, idx), bias_a, min(DSA_TOPK, s_len // 4))
    cmp_all = jnp.concatenate([cache_b_cmp[l][page_table].reshape(DB, past_len, G_B, 2, HD), cmp_kv], axis=1)
    kc, vc, ends = compress(cmp_all, cmp_pe[l], cmp_w1[l], cmp_w2[l], qk_gain_b[l, 1])
    sel_map = cmp_to_sel(kc.shape[1], -(-s_len // SEL_LEN))
    pool_s = cache_b_slc[l]
    win_all = jnp.concatenate([cache_b_win[l], win_kv], axis=1)
    o_b = nsa_block(q_b, gates_b, qpos_s, kc, vc, ends, sel_map, lambda idx: gather_paged(pool_s, page_table, slc_kv, idx, g_idx), win_all, kwpos_s, bias_b)
    o_m = mem_attention(zs.reshape(DB, DS, -1), COL_QM, cache_mem_kv[l].reshape(DB, N_MEM, H_M * 2 * HD_M), DS)
    y_s = merge_ffn(xs, o_a.reshape(n_s, -1), o_b.reshape(n_s, -1), o_m.reshape(n_s, -1), mg, W, g_ffn[l], n_s).reshape(DB, DS, D)
    s_out = (kv_a[None], ki[None], cmp_kv[None], slc_kv[None], win_all[:, win_all.shape[1] - w_eff:][None])
    return (y_p, y_s) + p_out + s_out
```

```python
import functools
import math

import numpy as np
import jax
import jax.numpy as jnp
from jax import lax
from jax.experimental import pallas as pl
from jax.experimental.pallas import tpu as pltpu

D_MODEL = 4096
PAGE_SIZE = 128
HD = 128
H_A = 12
G_A = 4
H_I = 32
D_I = 64
DSA_TOPK = 256
H_B = 12
G_B = 4
CMP_LEN = 32
CMP_STRIDE = 16
SEL_LEN = 64
N_SEL = 16
WINDOW = 512
N_MEM = 256
H_M = 4
HD_M = 256
N_BUCKETS = 32
MAX_DIST = 128
D_FF = -(-8 * D_MODEL // (3 * 256)) * 256
Q_BLOCK = 128
EPS = 1e-6
NEG = -1e30
BIG = 1e30
PROJ_SIZES = (H_A * HD, G_A * 2 * HD, H_I * D_I, H_I, D_I, H_B * HD, G_B * 2 * HD, G_B * 2 * HD, G_B * 2 * HD, 3 * H_B, H_M * HD_M, 3 * D_MODEL)
PROJ_OFFSETS = tuple(int(o) for o in np.concatenate([[0], np.cumsum(PROJ_SIZES)]))
D_MAIN = PROJ_OFFSETS[-2]
N_REP = H_A // G_A

VMEM_LIMIT_BYTES = 56 * 1024 * 1024
LANE = 128
COL_TILE = 512
ATTN_TILE = 256
MXU_DTYPE = jnp.bfloat16
F32 = jnp.float32
INT_MIN = -2 ** 31

COL_QI = 0
COL_QA = COL_QI + H_I * D_I
COL_KVA = COL_QA + H_A * HD
COL_QB = COL_KVA + G_A * 2 * HD
COL_CMP = COL_QB + H_B * HD
COL_SLC = COL_CMP + G_B * 2 * HD
COL_WIN = COL_SLC + G_B * 2 * HD
COL_QM = COL_WIN + G_B * 2 * HD
COL_SMALL = COL_QM + H_M * HD_M
SMALL_W = 256
OFF_WI, OFF_KI, OFF_GATE = 0, H_I, H_I + D_I
N_MAIN = -(-(COL_SMALL + SMALL_W) // COL_TILE) * COL_TILE


def _round_up(n, m):
    return -(-n // m) * m


def _cparams(*sem):
    return pltpu.CompilerParams(dimension_semantics=sem, vmem_limit_bytes=VMEM_LIMIT_BYTES)


def _dot(a, b):
    return jnp.dot(a, b, preferred_element_type=F32)


def _dot_nt(a, b):
    return lax.dot_general(a, b, (((1,), (1,)), ((), ())), preferred_element_type=F32)


def _float_key(x):
    b = pltpu.bitcast(x, jnp.int32)
    return b ^ ((b >> 31) & jnp.int32(0x7FFFFFFF))


def _np_float_key(v):
    b = int(np.array(v, np.float32).view(np.int32))
    return b ^ ((b >> 31) & 0x7FFFFFFF)


def _rms_cast_kernel(x_ref, g_ref, o_ref):
    x = x_ref[...]
    ms = jnp.mean(x * x, axis=-1, keepdims=True)
    o_ref[...] = ((x * lax.rsqrt(ms + EPS)) * g_ref[...]).astype(o_ref.dtype)


def rms_cast(x, g, tm):
    M, D = x.shape
    return pl.pallas_call(
        _rms_cast_kernel,
        out_shape=jax.ShapeDtypeStruct((M, D), MXU_DTYPE),
        grid=(M // tm,),
        in_specs=[pl.BlockSpec((tm, D), lambda i: (i, 0)), pl.BlockSpec((1, D), lambda i: (0, 0))],
        out_specs=pl.BlockSpec((tm, D), lambda i: (i, 0)),
        compiler_params=_cparams("parallel"),
        name="rms_cast",
    )(x, g.reshape(1, D))


def _mm_kernel(a_ref, w_ref, o_ref):
    o_ref[...] = _dot(a_ref[...], w_ref[...]).astype(o_ref.dtype)


def matmul(a, w, tm, tn, out_dtype=F32, name="matmul"):
    M, K = a.shape
    N = w.shape[1]
    return pl.pallas_call(
        _mm_kernel,
        out_shape=jax.ShapeDtypeStruct((M, N), out_dtype),
        grid=(M // tm, N // tn),
        in_specs=[pl.BlockSpec((tm, K), lambda i, j: (i, 0)), pl.BlockSpec((K, tn), lambda i, j: (0, j))],
        out_specs=pl.BlockSpec((tm, tn), lambda i, j: (i, j)),
        compiler_params=_cparams("parallel", "arbitrary"),
        name=name,
    )(a, w)


def _norm_cols(z, gain, m128, m256):
    tn = z.shape[1]
    sq = z * z
    ssq = [jnp.sum(sq[:, c * LANE:(c + 1) * LANE], axis=-1, keepdims=True) for c in range(tn // LANE)]
    cols = []
    for c in range(tn // LANE):
        sl = slice(c * LANE, (c + 1) * LANE)
        ms1 = ssq[c] * (1.0 / LANE)
        ms2 = (ssq[c - c % 2] + ssq[c - c % 2 + 1]) * (1.0 / (2 * LANE))
        ms = jnp.where(m256[:, sl] > 0, ms2, ms1)
        zc = z[:, sl]
        zn = (zc * lax.rsqrt(ms + EPS)) * gain[:, sl]
        cols.append(jnp.where((m128[:, sl] + m256[:, sl]) > 0, zn, zc))
    return jnp.concatenate(cols, axis=-1)


def _mm_norm_kernel(mode_ref, a_ref, w_ref, gain_ref, m128_ref, m256_ref, o_ref):
    j = pl.program_id(1)
    z = _dot(a_ref[...], w_ref[...])

    @pl.when(mode_ref[j] == 0)
    def _():
        o_ref[...] = z

    @pl.when(mode_ref[j] != 0)
    def _():
        o_ref[...] = _norm_cols(z, gain_ref[...], m128_ref[...], m256_ref[...])


def matmul_headnorm(a, w, gain, m128, m256, tile_mode, tm, tn, name):
    M, K = a.shape
    N = w.shape[1]
    vec = pl.BlockSpec((1, tn), lambda i, j, m: (0, j))
    return pl.pallas_call(
        _mm_norm_kernel,
        out_shape=jax.ShapeDtypeStruct((M, N), F32),
        grid_spec=pltpu.PrefetchScalarGridSpec(
            num_scalar_prefetch=1,
            grid=(M // tm, N // tn),
            in_specs=[pl.BlockSpec((tm, K), lambda i, j, m: (i, 0)), pl.BlockSpec((K, tn), lambda i, j, m: (0, j)), vec, vec, vec],
            out_specs=pl.BlockSpec((tm, tn), lambda i, j, m: (i, j)),
        ),
        compiler_params=_cparams("parallel", "arbitrary"),
        name=name,
    )(tile_mode, a, w, gain, m128, m256)


def _upmix_kernel(oa_ref, ob_ref, om_ref, wa_ref, wb_ref, wm_ref, g0_ref, g1_ref, g2_ref, o_ref):
    ya = _dot(oa_ref[...].astype(MXU_DTYPE), wa_ref[...])
    yb = _dot(ob_ref[...].astype(MXU_DTYPE), wb_ref[...])
    ym = _dot(om_ref[...].astype(MXU_DTYPE), wm_ref[...])
    mix = jax.nn.sigmoid(g0_ref[...]) * ya + jax.nn.sigmoid(g1_ref[...]) * yb + jax.nn.sigmoid(g2_ref[...]) * ym
    o_ref[...] = mix.astype(o_ref.dtype)


def upmix(o_a, o_b, o_m, w_a, w_b, w_m, mg, tm, tn):
    M = o_a.shape[0]
    D = w_a.shape[1]
    nj = D // tn
    a_spec = lambda k: pl.BlockSpec((tm, k), lambda i, j: (i, 0))
    w_spec = lambda k: pl.BlockSpec((k, tn), lambda i, j: (0, j))
    g_spec = lambda c: pl.BlockSpec((tm, tn), lambda i, j: (i, j + c * nj))
    return pl.pallas_call(
        _upmix_kernel,
        out_shape=jax.ShapeDtypeStruct((M, D), MXU_DTYPE),
        grid=(M // tm, nj),
        in_specs=[a_spec(o_a.shape[1]), a_spec(o_b.shape[1]), a_spec(o_m.shape[1]),
                  w_spec(w_a.shape[0]), w_spec(w_b.shape[0]), w_spec(w_m.shape[0]),
                  g_spec(0), g_spec(1), g_spec(2)],
        out_specs=pl.BlockSpec((tm, tn), lambda i, j: (i, j)),
        compiler_params=_cparams("parallel", "arbitrary"),
        name="upmix",
    )(o_a, o_b, o_m, w_a, w_b, w_m, mg, mg, mg)


def _mm_res_kernel(a_ref, w_ref, r_ref, o_ref):
    o_ref[...] = r_ref[...] + _dot(a_ref[...], w_ref[...])


def matmul_residual(a, w, r, tm, tn):
    M, K = a.shape
    N = w.shape[1]
    return pl.pallas_call(
        _mm_res_kernel,
        out_shape=jax.ShapeDtypeStruct((M, N), F32),
        grid=(M // tm, N // tn),
        in_specs=[pl.BlockSpec((tm, K), lambda i, j: (i, 0)), pl.BlockSpec((K, tn), lambda i, j: (0, j)),
                  pl.BlockSpec((tm, tn), lambda i, j: (i, j))],
        out_specs=pl.BlockSpec((tm, tn), lambda i, j: (i, j)),
        compiler_params=_cparams("parallel", "arbitrary"),
        name="matmul_residual",
    )(a, w, r)


def _swiglu_kernel(h_ref, wa_ref, wu_ref, o_ref):
    h = h_ref[...]
    a = _dot(h, wa_ref[...])
    u = _dot(h, wu_ref[...])
    o_ref[...] = (a * jax.nn.sigmoid(a) * u).astype(o_ref.dtype)


def swiglu_in(h, w_a, w_u, tm, tn):
    M, K = h.shape
    N = w_a.shape[1]
    return pl.pallas_call(
        _swiglu_kernel,
        out_shape=jax.ShapeDtypeStruct((M, N), MXU_DTYPE),
        grid=(M // tm, N // tn),
        in_specs=[pl.BlockSpec((tm, K), lambda i, j: (i, 0)), pl.BlockSpec((K, tn), lambda i, j: (0, j)),
                  pl.BlockSpec((K, tn), lambda i, j: (0, j))],
        out_specs=pl.BlockSpec((tm, tn), lambda i, j: (i, j)),
        compiler_params=_cparams("parallel", "arbitrary"),
        name="swiglu_in",
    )(h, w_a, w_u)


def _mm_res_acc_kernel(a_ref, w_ref, r_ref, o_ref, acc_ref):
    k = pl.program_id(2)

    @pl.when(k == 0)
    def _():
        acc_ref[...] = r_ref[...]

    acc_ref[...] += _dot(a_ref[...], w_ref[...])

    @pl.when(k == pl.num_programs(2) - 1)
    def _():
        o_ref[...] = acc_ref[...]


def matmul_residual_ksplit(a, w, r, tm, tn, tk):
    M, K = a.shape
    N = w.shape[1]
    return pl.pallas_call(
        _mm_res_acc_kernel,
        out_shape=jax.ShapeDtypeStruct((M, N), F32),
        grid=(M // tm, N // tn, K // tk),
        in_specs=[pl.BlockSpec((tm, tk), lambda i, j, k: (i, k)), pl.BlockSpec((tk, tn), lambda i, j, k: (k, j)),
                  pl.BlockSpec((tm, tn), lambda i, j, k: (i, j))],
        out_specs=pl.BlockSpec((tm, tn), lambda i, j, k: (i, j)),
        scratch_shapes=[pltpu.VMEM((tm, tn), F32)],
        compiler_params=_cparams("parallel", "arbitrary", "arbitrary"),
        name="matmul_residual_ksplit",
    )(a, w, r)


def _dsa_index_kernel(q_ref, small_ref, kd_ref, sc_ref, tau_ref, key_ref, *, topk, tq):
    i = pl.program_id(0)
    nk = sc_ref.shape[0]
    q = q_ref[...].astype(MXU_DTYPE)
    w = small_ref[:, OFF_WI:OFF_WI + H_I]
    row = lax.broadcasted_iota(jnp.int32, (tq, tq), 0)
    col = lax.broadcasted_iota(jnp.int32, (tq, tq), 1)
    neg_key = jnp.int32(_np_float_key(NEG))

    def score_tile(j, _):
        kd = kd_ref[j]
        acc = jnp.zeros((tq, tq), F32)
        for p in range(H_I // 2):
            r = _dot(q[:, p * 2 * D_I:(p + 1) * 2 * D_I], kd)
            acc = acc + jnp.maximum(r[:, :tq], 0.0) * w[:, 2 * p:2 * p + 1] + jnp.maximum(r[:, tq:], 0.0) * w[:, 2 * p + 1:2 * p + 2]
        sc_ref[j] = acc
        causal = (j * tq + col) <= (i * tq + row)
        key_ref[j] = jnp.where(causal, _float_key(acc), neg_key)
        return 0

    lax.fori_loop(0, i + 1, score_tile, 0)

    def zero_tile(j, _):
        sc_ref[j] = jnp.zeros((tq, tq), F32)
        return 0

    lax.fori_loop(i + 1, nk, zero_tile, 0)

    def bit_body(b, res_u):
        cand_u = res_u | jnp.left_shift(jnp.int32(1), 31 - b)
        cand_s = cand_u ^ jnp.int32(INT_MIN)

        def cnt_body(j, c):
            hit = jnp.where(key_ref[j] >= cand_s, 1.0, 0.0)
            part = hit[:, 0:LANE]
            for t in range(1, tq // LANE):
                part = part + hit[:, t * LANE:(t + 1) * LANE]
            return c + part

        cnt = lax.fori_loop(0, i + 1, cnt_body, jnp.zeros((tq, LANE), F32))
        total = jnp.sum(cnt, axis=-1, keepdims=True)
        return jnp.where(total >= topk, cand_u, res_u)

    res_u = lax.fori_loop(0, 32, bit_body, jnp.zeros((tq, 1), jnp.int32))
    tau_key = res_u ^ jnp.int32(INT_MIN)
    tau_bits = tau_key ^ ((tau_key >> 31) & jnp.int32(0x7FFFFFFF))
    tau_ref[...] = jnp.where(res_u == 0, -jnp.inf, pltpu.bitcast(tau_bits, F32))


def dsa_index(z, kd, topk, tq):
    T = z.shape[0]
    nk = kd.shape[0]
    return pl.pallas_call(
        functools.partial(_dsa_index_kernel, topk=topk, tq=tq),
        out_shape=(jax.ShapeDtypeStruct((nk, T, tq), F32), jax.ShapeDtypeStruct((T, 1), F32)),
        grid=(T // tq,),
        in_specs=[pl.BlockSpec((tq, H_I * D_I), lambda i: (i, COL_QI // (H_I * D_I))),
                  pl.BlockSpec((tq, SMALL_W), lambda i: (i, COL_SMALL // SMALL_W)),
                  pl.BlockSpec(kd.shape, lambda i: (0, 0, 0))],
        out_specs=(pl.BlockSpec((nk, tq, tq), lambda i: (0, i, 0)), pl.BlockSpec((tq, 1), lambda i: (i, 0))),
        scratch_shapes=[pltpu.VMEM((nk, tq, tq), jnp.int32)],
        compiler_params=_cparams("parallel"),
        name="dsa_index",
    )(z, z, kd)


def _flash_tile(q, k, v, bias, masks, state):
    m, l, acc = state
    s = _dot_nt(q, k) * (HD ** -0.5) + bias
    for mask in masks:
        s = jnp.where(mask, s, NEG)
    m_new = jnp.maximum(m, jnp.max(s, axis=-1, keepdims=True))
    alpha = jnp.exp(m - m_new)
    p = jnp.exp(s - m_new)
    l = alpha * l + jnp.sum(p, axis=-1, keepdims=True)
    acc = alpha * acc + _dot(p.astype(MXU_DTYPE), v)
    return m_new, l, acc


def _pattn_kernel(bfar_ref, *refs, mode, tq):
    q_refs = refs[0:N_REP]
    k_ref, v_ref, bias_ref = refs[N_REP:N_REP + 3]
    rest = refs[N_REP + 3:]
    if mode == 'dsa':
        sc_ref, tau_ref, o_ref = rest
    elif mode == 'slc':
        sel_ref, e_ref, small_ref, prev_ref, o_ref = rest
    else:
        small_ref, prev_ref, o_ref = rest
    g = pl.program_id(0)
    i = pl.program_id(1)
    q = [r[...].astype(MXU_DTYPE) for r in q_refs]
    row = lax.broadcasted_iota(jnp.int32, (tq, tq), 0)
    col = lax.broadcasted_iota(jnp.int32, (tq, tq), 1)
    if mode == 'dsa':
        tau = tau_ref[...]
    if mode == 'slc':
        sel = sel_ref[...].astype(MXU_DTYPE)

    def process(j, state, kind):
        start = pl.multiple_of(j * tq, tq)
        kt = k_ref[pl.ds(start, tq), :].astype(MXU_DTYPE)
        vt = v_ref[pl.ds(start, tq), :].astype(MXU_DTYPE)
        masks = []
        if mode == 'dsa':
            masks.append(sc_ref[j] >= tau)
        elif mode == 'slc':
            masks.append(_dot(sel, e_ref[j]) > 0.5)
        elif kind == 'edge':
            masks.append(row < col)
        if kind == 'diag':
            masks.append(row >= col)
        out = []
        for r in range(N_REP):
            bias = bfar_ref[g * N_REP + r] if kind in ('far', 'edge') else bias_ref[0, r, 0 if kind == 'diag' else 1]
            out.append(_flash_tile(q[r], kt, vt, bias, masks, state[r]))
        return tuple(out)

    state = tuple((jnp.full((tq, 1), NEG, F32), jnp.zeros((tq, 1), F32), jnp.zeros((tq, HD), F32)) for _ in range(N_REP))
    if mode == 'win':
        n_win = WINDOW // tq
        state = lax.cond(i >= n_win, lambda s: process(i - n_win, s, 'edge'), lambda s: s, state)
        for d in range(n_win - 1, 1, -1):
            state = lax.cond(i >= d, functools.partial(lambda s, d: process(i - d, s, 'far'), d=d), lambda s: s, state)
    else:
        state = lax.fori_loop(0, i - 1, lambda j, s: process(j, s, 'far'), state)
    state = lax.cond(i >= 1, lambda s: process(i - 1, s, 'near'), lambda s: s, state)
    state = process(i, state, 'diag')

    if mode != 'dsa':
        branch = 1 if mode == 'slc' else 2
        gates = jax.nn.sigmoid(small_ref[...])
        lane = lax.broadcasted_iota(jnp.int32, gates.shape, 1)
    for r in range(N_REP):
        m, l, acc = state[r]
        o = acc / l
        if mode != 'dsa':
            gcol = OFF_GATE + (g * N_REP + r) * 3 + branch
            o = prev_ref[:, r * HD:(r + 1) * HD] + o * jnp.sum(jnp.where(lane == gcol, gates, 0.0), axis=-1, keepdims=True)
        o_ref[:, r * HD:(r + 1) * HD] = o


def prompt_attention(mode, z, q_col, kv_col, bias_near, bias_far, tq, extra):
    T = z.shape[0]
    G = G_A
    qb = q_col // HD
    kb = kv_col // HD
    q_specs = [pl.BlockSpec((tq, HD), functools.partial(lambda g, i, b, r: (i, qb + g * N_REP + r), r=r)) for r in range(N_REP)]
    in_specs = q_specs + [
        pl.BlockSpec((T, HD), lambda g, i, b: (0, kb + 2 * g)),
        pl.BlockSpec((T, HD), lambda g, i, b: (0, kb + 2 * g + 1)),
        pl.BlockSpec((1, N_REP, 2, tq, tq), lambda g, i, b: (g, 0, 0, 0, 0)),
    ]
    args = [z] * N_REP + [z, z, bias_near]
    small_spec = pl.BlockSpec((tq, SMALL_W), lambda g, i, b: (i, COL_SMALL // SMALL_W))
    prev_spec = pl.BlockSpec((tq, N_REP * HD), lambda g, i, b: (i, g))
    if mode == 'dsa':
        scores, tau = extra
        nk = scores.shape[0]
        in_specs += [pl.BlockSpec((nk, tq, tq), lambda g, i, b: (0, i, 0)), pl.BlockSpec((tq, 1), lambda g, i, b: (i, 0))]
        args += [scores, tau]
    elif mode == 'slc':
        sel, e, prev = extra
        nsb = e.shape[1]
        in_specs += [pl.BlockSpec((tq, nsb), lambda g, i, b: (i, g)), pl.BlockSpec(e.shape, lambda g, i, b: (0, 0, 0)), small_spec, prev_spec]
        args += [sel, e, z, prev]
    else:
        (prev,) = extra
        in_specs += [small_spec, prev_spec]
        args += [z, prev]
    return pl.pallas_call(
        functools.partial(_pattn_kernel, mode=mode, tq=tq),
        out_shape=jax.ShapeDtypeStruct((T, G * N_REP * HD), F32),
        grid_spec=pltpu.PrefetchScalarGridSpec(
            num_scalar_prefetch=1,
            grid=(G, T // tq),
            in_specs=in_specs,
            out_specs=pl.BlockSpec((tq, N_REP * HD), lambda g, i, b: (i, g)),
        ),
        compiler_params=_cparams("parallel", "parallel"),
        name="attn_" + mode,
    )(bias_far, *args)


def _compress_kernel(x_ref, w1_ref, pe_ref, w2_ref, gain_ref, o_ref):
    c = pl.program_id(0) % 2
    nch = x_ref.shape[0]
    w1 = w1_ref[0]
    hid0 = jnp.zeros((nch, HD), F32)
    hid1 = jnp.zeros((nch, HD), F32)
    for s in range(CMP_STRIDE):
        xs = x_ref[:, s, :].astype(MXU_DTYPE)
        hid0 = hid0 + _dot(xs, w1[0, s].astype(MXU_DTYPE))
        hid1 = hid1 + _dot(xs, w1[1, s].astype(MXU_DTYPE))
    pe_term = _dot(pe_ref[0].astype(MXU_DTYPE), w1.reshape(CMP_LEN * HD, HD).astype(MXU_DTYPE))
    hid = pe_term + hid0 + pltpu.roll(hid1, nch - 1, 0)
    out = _dot((hid * jax.nn.sigmoid(hid)).astype(MXU_DTYPE), w2_ref[0].astype(MXU_DTYPE))
    ms = jnp.mean(out * out, axis=-1, keepdims=True)
    normed = (out * lax.rsqrt(ms + EPS)) * gain_ref[...]
    o_ref[0] = jnp.where(c == 0, normed, out)


def compress_blocks(z3, col, cmp_pe, cmp_w1, cmp_w2, gain_c):
    nch = z3.shape[0]
    r = CMP_LEN // CMP_STRIDE
    w1 = cmp_w1.reshape(2, r, CMP_STRIDE, HD, HD)
    pe = cmp_pe.reshape(CMP_LEN, 2, HD).transpose(1, 0, 2).reshape(2, 1, CMP_LEN * HD)
    cb = col // HD
    return pl.pallas_call(
        _compress_kernel,
        out_shape=jax.ShapeDtypeStruct((G_B * 2, nch, HD), F32),
        grid=(G_B * 2,),
        in_specs=[pl.BlockSpec((nch, CMP_STRIDE, HD), lambda gc: (0, 0, cb + gc)),
                  pl.BlockSpec((1, r, CMP_STRIDE, HD, HD), lambda gc: (gc % 2, 0, 0, 0, 0)),
                  pl.BlockSpec((1, 1, CMP_LEN * HD), lambda gc: (gc % 2, 0, 0)),
                  pl.BlockSpec((1, HD, HD), lambda gc: (gc % 2, 0, 0)),
                  pl.BlockSpec((1, HD), lambda gc: (0, 0))],
        out_specs=pl.BlockSpec((1, nch, HD), lambda gc: (gc, 0, 0)),
        compiler_params=_cparams("parallel"),
        name="nsa_compress",
    )(z3, w1, pe, cmp_w2, gain_c.reshape(1, HD))


def _nsa_cmp_kernel(*refs, tq, n_sel):
    q_refs = refs[0:N_REP]
    kc_ref, vc_ref, map_ref, small_ref, oc_ref, sel_ref = refs[N_REP:]
    g = pl.program_id(0)
    i = pl.program_id(1)
    kc = kc_ref[0].astype(MXU_DTYPE)
    vc = vc_ref[0].astype(MXU_DTYPE)
    smap = map_ref[...]
    nch = kc.shape[0]
    nsb = smap.shape[1]
    t = i * tq + lax.broadcasted_iota(jnp.int32, (tq, nch), 0)
    n = lax.broadcasted_iota(jnp.int32, (tq, nch), 1)
    valid = (n * CMP_STRIDE + CMP_LEN - 1) <= t
    gates = jax.nn.sigmoid(small_ref[...])
    lane = lax.broadcasted_iota(jnp.int32, gates.shape, 1)
    imp = jnp.zeros((tq, nsb), F32)
    for r in range(N_REP):
        q = q_refs[r][...].astype(MXU_DTYPE)
        lm = jnp.where(valid, _dot_nt(q, kc) * (HD ** -0.5), NEG)
        m = jnp.max(lm, axis=-1, keepdims=True)
        p = jnp.where(valid, jnp.exp(lm - m), 0.0)
        den = jnp.sum(p, axis=-1, keepdims=True)
        pc = (p / jnp.where(den > 0.0, den, 1.0)).astype(MXU_DTYPE)
        imp = imp + _dot(pc, smap)
        gcol = OFF_GATE + (g * N_REP + r) * 3
        gate = jnp.sum(jnp.where(lane == gcol, gates, 0.0), axis=-1, keepdims=True)
        oc_ref[:, r * HD:(r + 1) * HD] = _dot(pc, vc) * gate

    tpos = i * tq + lax.broadcasted_iota(jnp.int32, (tq, nsb), 0)
    blk = lax.broadcasted_iota(jnp.int32, (tq, nsb), 1)
    cur = tpos // SEL_LEN
    start_ok = blk <= cur
    forced_or_imp = jnp.where(blk == 0, BIG, jnp.where(blk == cur, BIG, jnp.where(blk == cur - 1, BIG, imp)))
    score = jnp.where(start_ok, forced_or_imp, NEG)
    blk_f = blk.astype(F32)
    sel = jnp.zeros((tq, nsb), F32)
    for _ in range(n_sel):
        best = jnp.max(score, axis=-1, keepdims=True)
        first = jnp.min(jnp.where(score == best, blk_f, float(nsb)), axis=-1, keepdims=True)
        hit = blk_f == first
        sel = jnp.where(hit, 1.0, sel)
        score = jnp.where(hit, -jnp.inf, score)
    sel_ref[...] = sel


def nsa_cmp_attention(z, q_col, kcv, sel_map, tq):
    T = z.shape[0]
    nch = kcv.shape[1]
    nsb = sel_map.shape[1]
    qb = q_col // HD
    q_specs = [pl.BlockSpec((tq, HD), functools.partial(lambda g, i, r: (i, qb + g * N_REP + r), r=r)) for r in range(N_REP)]
    return pl.pallas_call(
        functools.partial(_nsa_cmp_kernel, tq=tq, n_sel=min(N_SEL, nsb)),
        out_shape=(jax.ShapeDtypeStruct((T, H_B * HD), F32), jax.ShapeDtypeStruct((T, G_B * nsb), F32)),
        grid=(G_B, T // tq),
        in_specs=q_specs + [pl.BlockSpec((1, nch, HD), lambda g, i: (2 * g, 0, 0)),
                            pl.BlockSpec((1, nch, HD), lambda g, i: (2 * g + 1, 0, 0)),
                            pl.BlockSpec((nch, nsb), lambda g, i: (0, 0)),
                            pl.BlockSpec((tq, SMALL_W), lambda g, i: (i, COL_SMALL // SMALL_W))],
        out_specs=(pl.BlockSpec((tq, N_REP * HD), lambda g, i: (i, g)), pl.BlockSpec((tq, nsb), lambda g, i: (i, g))),
        compiler_params=_cparams("parallel", "parallel"),
        name="nsa_cmp",
    )(*([z] * N_REP), kcv, kcv, sel_map, z)


def _mem_attn_kernel(q_ref, k_ref, v_ref, o_ref):
    q = q_ref[0].astype(MXU_DTYPE)
    s = _dot_nt(q, k_ref[0].astype(MXU_DTYPE)) * (HD_M ** -0.5)
    p = jnp.exp(s - jnp.max(s, axis=-1, keepdims=True))
    p = p / jnp.sum(p, axis=-1, keepdims=True)
    o_ref[0] = _dot(p.astype(MXU_DTYPE), v_ref[0].astype(MXU_DTYPE))


def mem_attention(z3, q_col, mkv3, tq):
    B, Tq, _ = z3.shape
    n_mem = mkv3.shape[1]
    qb = q_col // HD_M
    return pl.pallas_call(
        _mem_attn_kernel,
        out_shape=jax.ShapeDtypeStruct((B, Tq, H_M * HD_M), F32),
        grid=(B, H_M, Tq // tq),
        in_specs=[pl.BlockSpec((1, tq, HD_M), lambda b, h, i: (b, i, qb + h)),
                  pl.BlockSpec((1, n_mem, HD_M), lambda b, h, i: (b, 0, 2 * h)),
                  pl.BlockSpec((1, n_mem, HD_M), lambda b, h, i: (b, 0, 2 * h + 1))],
        out_specs=pl.BlockSpec((1, tq, HD_M), lambda b, h, i: (b, i, h)),
        compiler_params=_cparams("parallel", "parallel", "arbitrary"),
        name="mem_attn",
    )(z3, mkv3, mkv3)


TOK_PAD = 8
NEW_PAD = 128


def _gather_pages(pt_ref, b, n_pages, src_at, buf, sem):
    def copy(p, page):
        return pltpu.make_async_copy(src_at(page), buf.at[pl.ds(p * PAGE_SIZE, PAGE_SIZE)], sem)

    def issue(p, _):
        copy(p, pt_ref[b, p]).start()
        return 0

    def wait(p, _):
        copy(p, pt_ref[b, p]).wait()
        return 0

    lax.fori_loop(0, n_pages, issue, 0)
    lax.fori_loop(0, n_pages, wait, 0)


def _sample_index_kernel(pt_ref, q_ref, w_ref, knew_ref, pool_ref, mp_ref, mn_ref, kbuf, sem, *, topk, n_pages, past_len, chunk):
    b = pl.program_id(0)
    _gather_pages(pt_ref, b, n_pages, lambda page: pool_ref.at[page], kbuf, sem)
    q = q_ref[0].astype(MXU_DTYPE)
    w = w_ref[0]

    def head_sum(keys):
        x = jnp.maximum(_dot_nt(q, keys.astype(MXU_DTYPE)), 0.0) * w
        return jnp.sum(x.reshape(TOK_PAD, H_I, x.shape[1]), axis=1)

    sc = [head_sum(kbuf[c * chunk:(c + 1) * chunk, :]) for c in range(past_len // chunk)]
    sc_new = head_sum(knew_ref[0])
    tok = lax.broadcasted_iota(jnp.int32, (TOK_PAD, NEW_PAD), 0)
    new = lax.broadcasted_iota(jnp.int32, (TOK_PAD, NEW_PAD), 1)
    causal_new = new <= tok
    keys = [_float_key(s) for s in sc]
    key_new = jnp.where(causal_new, _float_key(sc_new), jnp.int32(_np_float_key(NEG)))

    def bit_body(i, res_u):
        cand_u = res_u | jnp.left_shift(jnp.int32(1), 31 - i)
        cand_s = cand_u ^ jnp.int32(INT_MIN)
        cnt = jnp.sum(jnp.where(key_new >= cand_s, 1.0, 0.0), axis=-1, keepdims=True)
        for k in keys:
            cnt = cnt + jnp.sum(jnp.where(k >= cand_s, 1.0, 0.0), axis=-1, keepdims=True)
        return jnp.where(cnt >= topk, cand_u, res_u)

    res_u = lax.fori_loop(0, 32, bit_body, jnp.zeros((TOK_PAD, 1), jnp.int32))
    tau_key = res_u ^ jnp.int32(INT_MIN)
    tau_bits = tau_key ^ ((tau_key >> 31) & jnp.int32(0x7FFFFFFF))
    tau = jnp.where(res_u == 0, -jnp.inf, pltpu.bitcast(tau_bits, F32))
    for c, s in enumerate(sc):
        mp_ref[0, 0, :, c * chunk:(c + 1) * chunk] = jnp.where(s >= tau, 1.0, 0.0)
    mn_ref[0, 0] = jnp.where(causal_new, jnp.where(sc_new >= tau, 1.0, 0.0), 0.0)


def sample_index(page_table, q, w, k_new, pool, topk):
    DB, n_pages = page_table.shape
    past_len = n_pages * PAGE_SIZE
    chunk = min(1024, past_len)
    return pl.pallas_call(
        functools.partial(_sample_index_kernel, topk=topk, n_pages=n_pages, past_len=past_len, chunk=chunk),
        out_shape=(jax.ShapeDtypeStruct((DB, 1, TOK_PAD, past_len), F32), jax.ShapeDtypeStruct((DB, 1, TOK_PAD, NEW_PAD), F32)),
        grid_spec=pltpu.PrefetchScalarGridSpec(
            num_scalar_prefetch=1,
            grid=(DB,),
            in_specs=[pl.BlockSpec((1, TOK_PAD * H_I, D_I), lambda b, pt: (b, 0, 0)),
                      pl.BlockSpec((1, TOK_PAD * H_I, 1), lambda b, pt: (b, 0, 0)),
                      pl.BlockSpec((1, NEW_PAD, D_I), lambda b, pt: (b, 0, 0)),
                      pl.BlockSpec(memory_space=pl.ANY)],
            out_specs=(pl.BlockSpec((1, 1, TOK_PAD, past_len), lambda b, pt: (b, 0, 0, 0)),
                       pl.BlockSpec((1, 1, TOK_PAD, NEW_PAD), lambda b, pt: (b, 0, 0, 0))),
            scratch_shapes=[pltpu.VMEM((past_len, D_I), F32), pltpu.SemaphoreType.DMA(())],
        ),
        compiler_params=_cparams("arbitrary"),
        name="sample_index",
    )(page_table, q, w, k_new, pool)


def _sample_attend_kernel(pt_ref, q_ref, kvn_ref, mp_ref, mn_ref, tp_ref, tn_ref, *rest, gated, n_pages, past_len, chunk):
    if gated:
        gate_ref, prev_ref, pool_ref, o_ref, kvbuf, sem = rest
    else:
        pool_ref, o_ref, kvbuf, sem = rest
    b = pl.program_id(0)
    g = pl.program_id(1)
    _gather_pages(pt_ref, b, n_pages, lambda page: pool_ref.at[page, :, g], kvbuf, sem)
    q = q_ref[0, 0].astype(MXU_DTYPE)

    def logits(keys, bias_ref, mask_ref, sl):
        s = _dot_nt(q, keys.astype(MXU_DTYPE)) * (HD ** -0.5)
        bias = jnp.concatenate([bias_ref[r, :, sl] for r in range(N_REP)], axis=0)
        keep = mask_ref[0, 0, :, sl] > 0.5
        return jnp.where(jnp.concatenate([keep] * N_REP, axis=0), s + bias, NEG)

    n_chunks = past_len // chunk
    s_past = [logits(kvbuf[c * chunk:(c + 1) * chunk, 0, :], tp_ref, mp_ref, slice(c * chunk, (c + 1) * chunk)) for c in range(n_chunks)]
    s_new = logits(kvn_ref[0, 0, 0], tn_ref, mn_ref, slice(0, NEW_PAD))
    m = jnp.max(s_new, axis=-1, keepdims=True)
    for s in s_past:
        m = jnp.maximum(m, jnp.max(s, axis=-1, keepdims=True))
    p_new = jnp.exp(s_new - m)
    l = jnp.sum(p_new, axis=-1, keepdims=True)
    acc = _dot(p_new.astype(MXU_DTYPE), kvn_ref[0, 0, 1].astype(MXU_DTYPE))
    for c, s in enumerate(s_past):
        p = jnp.exp(s - m)
        l = l + jnp.sum(p, axis=-1, keepdims=True)
        acc = acc + _dot(p.astype(MXU_DTYPE), kvbuf[c * chunk:(c + 1) * chunk, 1, :].astype(MXU_DTYPE))
    o = acc / l
    if gated:
        o = prev_ref[0, 0] + o * jax.nn.sigmoid(gate_ref[0, 0])
    o_ref[0, 0] = o


def sample_attend(page_table, q, kv_new, mask_past, mask_new, tab_past, tab_new, pool, gate_prev=None):
    DB, n_pages = page_table.shape
    G = q.shape[1]
    past_len = n_pages * PAGE_SIZE
    chunk = min(1024, past_len)
    rows = N_REP * TOK_PAD
    mb, mg = mask_past.shape[0] > 1, mask_past.shape[1] > 1
    mask_map = lambda b, g, pt: (b if mb else 0, g if mg else 0, 0, 0)
    row_spec = lambda n: pl.BlockSpec((1, 1, rows, n), lambda b, g, pt: (b, g, 0, 0))
    in_specs = [row_spec(HD),
                pl.BlockSpec((1, 1, 2, NEW_PAD, HD), lambda b, g, pt: (b, g, 0, 0, 0)),
                pl.BlockSpec((1, 1, TOK_PAD, past_len), mask_map),
                pl.BlockSpec((1, 1, TOK_PAD, NEW_PAD), mask_map),
                pl.BlockSpec((N_REP, TOK_PAD, past_len), lambda b, g, pt: (g, 0, 0)),
                pl.BlockSpec((N_REP, TOK_PAD, NEW_PAD), lambda b, g, pt: (g, 0, 0))]
    args = [q, kv_new, mask_past, mask_new, tab_past, tab_new]
    if gate_prev is not None:
        in_specs += [row_spec(1), row_spec(HD)]
        args += list(gate_prev)
    in_specs.append(pl.BlockSpec(memory_space=pl.ANY))
    args.append(pool)
    return pl.pallas_call(
        functools.partial(_sample_attend_kernel, gated=gate_prev is not None, n_pages=n_pages, past_len=past_len, chunk=chunk),
        out_shape=jax.ShapeDtypeStruct((DB, G, rows, HD), F32),
        grid_spec=pltpu.PrefetchScalarGridSpec(
            num_scalar_prefetch=1,
            grid=(DB, G),
            in_specs=in_specs,
            out_specs=row_spec(HD),
            scratch_shapes=[pltpu.VMEM((past_len, 2, HD), F32), pltpu.SemaphoreType.DMA(())],
        ),
        compiler_params=_cparams("arbitrary", "arbitrary"),
        name="sample_attend",
    )(page_table, *args)


def _sample_compress_kernel(pt_ref, cnew_ref, w1_ref, pe_ref, w2_ref, gain_ref, pool_ref, o_ref, cbuf, sem, *, n_pages, n_blocks):
    b = pl.program_id(0)
    g = pl.program_id(1)
    per_page = PAGE_SIZE // CMP_STRIDE

    def copy(p, page):
        return pltpu.make_async_copy(pool_ref.at[page, :, :, g], cbuf.at[pl.ds(p * per_page, per_page)], sem)

    lax.fori_loop(0, n_pages, lambda p, _: (copy(p, pt_ref[b, p]).start(), 0)[1], 0)
    n_cached = n_pages * per_page
    cbuf[n_cached] = cnew_ref[0, 0]
    for n in range(n_cached + 1, cbuf.shape[0]):
        cbuf[n] = jnp.zeros(cbuf.shape[1:], F32)
    lax.fori_loop(0, n_pages, lambda p, _: (copy(p, pt_ref[b, p]).wait(), 0)[1], 0)

    nch = cbuf.shape[0]
    for c in range(2):
        hid0 = jnp.zeros((nch, HD), F32)
        hid1 = jnp.zeros((nch, HD), F32)
        for s in range(CMP_STRIDE):
            xs = cbuf[:, s, c, :].astype(MXU_DTYPE)
            hid0 = hid0 + _dot(xs, w1_ref[c, 0, s])
            hid1 = hid1 + _dot(xs, w1_ref[c, 1, s])
        pe_term = _dot(pe_ref[c].astype(MXU_DTYPE), w1_ref[c].reshape(CMP_LEN * HD, HD))
        hid = pe_term + hid0 + pltpu.roll(hid1, nch - 1, 0)
        out = _dot((hid * jax.nn.sigmoid(hid)).astype(MXU_DTYPE), w2_ref[c])[0:n_blocks]
        if c == 0:
            ms = jnp.mean(out * out, axis=-1, keepdims=True)
            out = (out * lax.rsqrt(ms + EPS)) * gain_ref[...]
        o_ref[0, c] = out


def sample_compress(page_table, c_new, pool6, cmp_pe, cmp_w1, cmp_w2, gain_c):
    DB, n_pages = page_table.shape
    n_blocks = n_pages * (PAGE_SIZE // CMP_STRIDE)
    n_slots = _round_up(n_blocks + 1, 8)
    r = CMP_LEN // CMP_STRIDE
    w1 = cmp_w1.reshape(2, r, CMP_STRIDE, HD, HD).astype(MXU_DTYPE)
    pe = cmp_pe.reshape(CMP_LEN, 2, HD).transpose(1, 0, 2).reshape(2, 1, CMP_LEN * HD)
    full = lambda a: pl.BlockSpec(a.shape, lambda b, g, pt: (0,) * a.ndim)
    w2 = cmp_w2.astype(MXU_DTYPE)
    gain = gain_c.reshape(1, HD)
    return pl.pallas_call(
        functools.partial(_sample_compress_kernel, n_pages=n_pages, n_blocks=n_blocks),
        out_shape=jax.ShapeDtypeStruct((DB, G_B * 2, n_blocks, HD), F32),
        grid_spec=pltpu.PrefetchScalarGridSpec(
            num_scalar_prefetch=1,
            grid=(DB, G_B),
            in_specs=[pl.BlockSpec((1, 1, CMP_STRIDE, 2, HD), lambda b, g, pt: (b, g, 0, 0, 0)),
                      full(w1), full(pe), full(w2), full(gain), pl.BlockSpec(memory_space=pl.ANY)],
            out_specs=pl.BlockSpec((1, 2, n_blocks, HD), lambda b, g, pt: (b, g, 0, 0)),
            scratch_shapes=[pltpu.VMEM((n_slots, CMP_STRIDE, 2, HD), F32), pltpu.SemaphoreType.DMA(())],
        ),
        compiler_params=_cparams("arbitrary", "arbitrary"),
        name="sample_compress",
    )(page_table, c_new, w1, pe, w2, gain, pool6)


def _sample_cmp_kernel(q_ref, kc_ref, vc_ref, map_ref, e_ref, gate_ref, oc_ref, mp_ref, mn_ref, *, past_len, n_real, n_sel):
    q = q_ref[0, 0].astype(MXU_DTYPE)
    kc = kc_ref[0, 0].astype(MXU_DTYPE)
    vc = vc_ref[0, 0].astype(MXU_DTYPE)
    rows = q.shape[0]
    nch = kc.shape[0]
    t = past_len + (lax.broadcasted_iota(jnp.int32, (rows, nch), 0) & (TOK_PAD - 1))
    n = lax.broadcasted_iota(jnp.int32, (rows, nch), 1)
    valid = (n * CMP_STRIDE + CMP_LEN - 1) <= t
    lm = jnp.where(valid, _dot_nt(q, kc) * (HD ** -0.5), NEG)
    m = jnp.max(lm, axis=-1, keepdims=True)
    p = jnp.where(valid, jnp.exp(lm - m), 0.0)
    den = jnp.sum(p, axis=-1, keepdims=True)
    pc = (p / jnp.where(den > 0.0, den, 1.0)).astype(MXU_DTYPE)
    oc_ref[0, 0] = _dot(pc, vc) * jax.nn.sigmoid(gate_ref[0, 0])
    imp_rows = _dot(pc, map_ref[...])
    imp = imp_rows[0:TOK_PAD]
    for r in range(1, N_REP):
        imp = imp + imp_rows[r * TOK_PAD:(r + 1) * TOK_PAD]

    nsb = imp.shape[1]
    tpos = past_len + lax.broadcasted_iota(jnp.int32, (TOK_PAD, nsb), 0)
    blk = lax.broadcasted_iota(jnp.int32, (TOK_PAD, nsb), 1)
    cur = tpos // SEL_LEN
    forced_or_imp = jnp.where(blk == 0, BIG, jnp.where(blk == cur, BIG, jnp.where(blk == cur - 1, BIG, imp)))
    score = jnp.where(blk <= cur, forced_or_imp, jnp.where(blk < n_real, NEG, -jnp.inf))
    blk_f = blk.astype(F32)
    sel = jnp.zeros((TOK_PAD, nsb), F32)
    for _ in range(n_sel):
        best = jnp.max(score, axis=-1, keepdims=True)
        first = jnp.min(jnp.where(score == best, blk_f, float(nsb)), axis=-1, keepdims=True)
        hit = blk_f == first
        sel = jnp.where(hit, 1.0, sel)
        score = jnp.where(hit, -jnp.inf, score)
    keep = _dot(sel.astype(MXU_DTYPE), e_ref[...])
    mp_ref[0, 0] = jnp.where(keep[:, 0:past_len] > 0.5, 1.0, 0.0)
    tok = lax.broadcasted_iota(jnp.int32, (TOK_PAD, NEW_PAD), 0)
    new = lax.broadcasted_iota(jnp.int32, (TOK_PAD, NEW_PAD), 1)
    mn_ref[0, 0] = jnp.where(new <= tok, jnp.where(keep[:, past_len:] > 0.5, 1.0, 0.0), 0.0)


def sample_cmp_attention(q, kcv, gate, past_len, n_tokens):
    DB, G, rows, _ = q.shape
    nc = kcv.shape[2]
    n_real = -(-(past_len + n_tokens) // SEL_LEN)
    nsb = _round_up(n_real, LANE)
    sel_map = cmp_to_sel(nc, nsb).astype(MXU_DTYPE)
    key = jnp.arange(past_len + NEW_PAD)[None, :]
    expand = ((key // SEL_LEN == jnp.arange(nsb)[:, None]) & (key < past_len + n_tokens)).astype(MXU_DTYPE)
    row_spec = lambda n: pl.BlockSpec((1, 1, rows, n), lambda b, g: (b, g, 0, 0))
    return pl.pallas_call(
        functools.partial(_sample_cmp_kernel, past_len=past_len, n_real=n_real, n_sel=min(N_SEL, n_real)),
        out_shape=(jax.ShapeDtypeStruct((DB, G, rows, HD), F32), jax.ShapeDtypeStruct((DB, G, TOK_PAD, past_len), F32),
                   jax.ShapeDtypeStruct((DB, G, TOK_PAD, NEW_PAD), F32)),
        grid=(DB, G),
        in_specs=[row_spec(HD),
                  pl.BlockSpec((1, 1, nc, HD), lambda b, g: (b, 2 * g, 0, 0)),
                  pl.BlockSpec((1, 1, nc, HD), lambda b, g: (b, 2 * g + 1, 0, 0)),
                  pl.BlockSpec(sel_map.shape, lambda b, g: (0, 0)),
                  pl.BlockSpec(expand.shape, lambda b, g: (0, 0)),
                  row_spec(1)],
        out_specs=(row_spec(HD), pl.BlockSpec((1, 1, TOK_PAD, past_len), lambda b, g: (b, g, 0, 0)),
                   pl.BlockSpec((1, 1, TOK_PAD, NEW_PAD), lambda b, g: (b, g, 0, 0))),
        compiler_params=_cparams("parallel", "parallel"),
        name="sample_cmp",
    )(q, kcv, kcv, sel_map, expand, gate)


def t5_bucket(dist):
    max_exact = N_BUCKETS // 2
    n = jnp.maximum(dist, 0)
    nf = jnp.maximum(n, 1).astype(jnp.float32)
    large = max_exact + (jnp.log(nf / max_exact) / math.log(MAX_DIST / max_exact) * (N_BUCKETS - max_exact)).astype(jnp.int32)
    return jnp.where(n < max_exact, n, jnp.minimum(large, N_BUCKETS - 1))


def cmp_to_sel(n_cmp, n_sel_blocks):
    cs = jnp.arange(n_cmp)[:, None] * CMP_STRIDE
    ss = jnp.arange(n_sel_blocks)[None, :] * SEL_LEN
    return ((cs < ss + SEL_LEN) & (cs + CMP_LEN > ss)).astype(jnp.float32)


def toeplitz_bias_tiles(bias_tab, tq):
    ii = jnp.arange(tq)[:, None]
    jj = jnp.arange(tq)[None, :]
    tiles = jnp.stack([bias_tab[t5_bucket(ii - jj)], bias_tab[t5_bucket(tq + ii - jj)]])
    return tiles.transpose(3, 0, 1, 2).reshape(G_A, N_REP, 2, tq, tq)


def block_expand_matrix(n_blocks, n_tiles, tk):
    b = jnp.arange(n_blocks)[None, :, None]
    key = (jnp.arange(n_tiles)[:, None, None] * tk + jnp.arange(tk)[None, None, :])
    return (key // SEL_LEN == b).astype(MXU_DTYPE)


def indexer_key_tiles(ki, tk):
    S = ki.shape[0]
    kt = ki.reshape(S // tk, tk, D_I).transpose(0, 2, 1).astype(MXU_DTYPE)
    zero = jnp.zeros_like(kt)
    return jnp.concatenate([jnp.concatenate([kt, zero], axis=2), jnp.concatenate([zero, kt], axis=2)], axis=1)


def _pad_cols(w, n):
    return jnp.pad(w, ((0, 0), (0, n - w.shape[1])))


def _seg(w, idx):
    return w[:, PROJ_OFFSETS[idx]:PROJ_OFFSETS[idx + 1]]


def prepare_weights(w_in, w_mem_kv, w_up_a, w_up_b, w_up_m, w_out, w_ffn_in, w_ffn_out):
    bf = MXU_DTYPE
    d_ff_pad = _round_up(D_FF, COL_TILE)
    main = jnp.concatenate([_seg(w_in, 2), _seg(w_in, 0), _seg(w_in, 1), _seg(w_in, 5), _seg(w_in, 6), _seg(w_in, 7),
                            _seg(w_in, 8), _seg(w_in, 10), _seg(w_in, 3), _seg(w_in, 4), _seg(w_in, 9)], axis=1)
    return dict(
        w_main=_pad_cols(main, N_MAIN).astype(bf),
        w_mg=_seg(w_in, 11).astype(bf),
        w_mem_kv=w_mem_kv.astype(bf),
        w_up_a=w_up_a.astype(bf), w_up_b=w_up_b.astype(bf), w_up_m=w_up_m.astype(bf),
        w_out=w_out.astype(bf),
        w_ffn_a=_pad_cols(w_ffn_in[:, :D_FF], d_ff_pad).astype(bf),
        w_ffn_u=_pad_cols(w_ffn_in[:, D_FF:], d_ff_pad).astype(bf),
        w_ffn_out=jnp.pad(w_ffn_out, ((0, d_ff_pad - D_FF), (0, 0))).astype(bf),
    )


def main_norm_vectors(qk_gain_a, qk_gain_b, qk_gain_m):
    one, zero = jnp.ones((HD,), F32), jnp.zeros((HD,), F32)

    def kv(gk):
        return jnp.tile(jnp.concatenate([gk, zero]), G_A), jnp.tile(jnp.concatenate([one, zero]), G_A)

    kva, fa = kv(qk_gain_a[1])
    slc, fs = kv(qk_gain_b[2])
    win, fw = kv(qk_gain_b[3])
    z = lambda n: jnp.zeros((n,), F32)
    gain = jnp.concatenate([z(H_I * D_I), jnp.tile(qk_gain_a[0], H_A), kva, jnp.tile(qk_gain_b[0], H_B), z(G_B * 2 * HD), slc, win,
                            jnp.tile(qk_gain_m[0], H_M), z(N_MAIN - COL_SMALL)])
    m128 = jnp.concatenate([z(H_I * D_I), jnp.ones((H_A * HD,), F32), fa, jnp.ones((H_B * HD,), F32), z(G_B * 2 * HD), fs, fw,
                            z(H_M * HD_M), z(N_MAIN - COL_SMALL)])
    m256 = jnp.concatenate([z(COL_QM), jnp.ones((H_M * HD_M,), F32), z(N_MAIN - COL_SMALL)])
    cols = np.arange(N_MAIN)
    has = ((cols >= COL_QA) & (cols < COL_CMP)) | ((cols >= COL_SLC) & (cols < COL_SMALL))
    tile_mode = jnp.asarray(has.reshape(-1, COL_TILE).any(axis=1).astype(np.int32))
    return gain.reshape(1, -1), m128.reshape(1, -1), m256.reshape(1, -1), tile_mode


def project_in(x2d, g_attn, W, norm_vecs, tm):
    h = rms_cast(x2d, g_attn, min(tm, 512))
    gain, m128, m256, tile_mode = norm_vecs
    z_main = matmul_headnorm(h, W['w_main'], gain, m128, m256, tile_mode, tm, COL_TILE, "in_proj_main")
    mg = matmul(h, W['w_mg'], tm, COL_TILE, name="in_proj_mg")
    return z_main, mg


def merge_ffn(x2d, o_a, o_b, o_m, mg, W, g_ffn, tm):
    mix = upmix(o_a, o_b, o_m, W['w_up_a'], W['w_up_b'], W['w_up_m'], mg, min(tm, 512), COL_TILE)
    x2 = matmul_residual(mix, W['w_out'], x2d, tm, COL_TILE)
    h2 = rms_cast(x2, g_ffn, min(tm, 512))
    act = swiglu_in(h2, W['w_ffn_a'], W['w_ffn_u'], tm, COL_TILE)
    tk = act.shape[1] // 4
    return matmul_residual_ksplit(act, W['w_ffn_out'], x2, tm, COL_TILE, tk)


def prompt_mixers(z, mkv, rel_bias, cmp_pe, cmp_w1, cmp_w2, gain_c, tq):
    T = z.shape[0]
    nk = T // tq
    bias_a, bias_b = rel_bias[:, :H_A], rel_bias[:, H_A:]
    far_a, far_b = bias_a[N_BUCKETS - 1], bias_b[N_BUCKETS - 1]
    near_a, near_b = toeplitz_bias_tiles(bias_a, tq), toeplitz_bias_tiles(bias_b, tq)
    ki = z[:, COL_SMALL + OFF_KI:COL_SMALL + OFF_KI + D_I]
    scores, tau = dsa_index(z, indexer_key_tiles(ki, tq), min(DSA_TOPK, T // 4), tq)
    o_a = prompt_attention('dsa', z, COL_QA, COL_KVA, near_a, far_a, tq, (scores, tau))
    nch = T // CMP_STRIDE
    nsb = T // SEL_LEN
    kcv = compress_blocks(z.reshape(nch, CMP_STRIDE, z.shape[1]), COL_CMP, cmp_pe, cmp_w1, cmp_w2, gain_c)
    sel_map = cmp_to_sel(nch, nsb).astype(MXU_DTYPE)
    oc, sel = nsa_cmp_attention(z, COL_QB, kcv, sel_map, tq)
    ocs = prompt_attention('slc', z, COL_QB, COL_SLC, near_b, far_b, tq, (sel, block_expand_matrix(nsb, nk, tq), oc))
    o_b = prompt_attention('win', z, COL_QB, COL_WIN, near_b, far_b, tq, (ocs,))
    o_m = mem_attention(z[None], COL_QM, mkv[None], tq)[0]
    return o_a, o_b, o_m


def sample_mixers(zs, page_table, pool_a_kv, pool_a_idx, pool_b_cmp, pool_b_slc, cache_win, cache_mem, rel_bias, cmp_pe, cmp_w1, cmp_w2, gain_c, DS):
    DB, n_pages = page_table.shape
    past_len = n_pages * PAGE_SIZE
    bias_a, bias_b = rel_bias[:, :H_A], rel_bias[:, H_A:]
    seg = lambda c, n: zs[:, c:c + n].reshape(DB, DS, n)
    pad_tok = lambda a: jnp.pad(a, ((0, 0), (0, TOK_PAD - DS)) + ((0, 0),) * (a.ndim - 2))
    rows = N_REP * TOK_PAD

    def q_rows(c):
        q = pad_tok(seg(c, H_A * HD)).reshape(DB, TOK_PAD, G_A, N_REP, HD)
        return q.transpose(0, 2, 3, 1, 4).reshape(DB, G_A, rows, HD)

    def kv_new(c):
        kv = seg(c, G_A * 2 * HD).reshape(DB, DS, G_A, 2, HD).transpose(0, 2, 3, 1, 4)
        return jnp.pad(kv, ((0, 0), (0, 0), (0, 0), (0, NEW_PAD - DS), (0, 0)))

    tok = jnp.arange(TOK_PAD)[:, None]
    new = jnp.arange(NEW_PAD)[None, :]

    def bias_tables(bias_tab, span):
        dist_p = span + tok - jnp.arange(span)[None, :]
        return bias_tab[t5_bucket(dist_p)].transpose(2, 0, 1), bias_tab[t5_bucket(tok - new)].transpose(2, 0, 1)

    unrows = lambda o: o.reshape(DB, G_A, N_REP, TOK_PAD, HD)[:, :, :, :DS].transpose(0, 3, 1, 2, 4).reshape(DB * DS, H_A * HD)

    qi = pad_tok(seg(COL_QI, H_I * D_I)).reshape(DB, TOK_PAD * H_I, D_I)
    wi = pad_tok(seg(COL_SMALL + OFF_WI, H_I)).reshape(DB, TOK_PAD * H_I, 1)
    ki_new = jnp.pad(seg(COL_SMALL + OFF_KI, D_I), ((0, 0), (0, NEW_PAD - DS), (0, 0)))
    keep_p, keep_n = sample_index(page_table, qi, wi, ki_new, pool_a_idx, min(DSA_TOPK, (past_len + DS) // 4))
    tab_p, tab_n = bias_tables(bias_a, past_len)
    o_a = sample_attend(page_table, q_rows(COL_QA), kv_new(COL_KVA), keep_p, keep_n, tab_p, tab_n, pool_a_kv)

    c_new = seg(COL_CMP, G_B * 2 * HD).reshape(DB, DS, G_B, 2, HD).transpose(0, 2, 1, 3, 4)
    c_new = jnp.pad(c_new, ((0, 0), (0, 0), (0, CMP_STRIDE - DS), (0, 0), (0, 0)))
    n_pool = pool_b_cmp.shape[0]
    kcv = sample_compress(page_table, c_new, pool_b_cmp.reshape(n_pool, PAGE_SIZE // CMP_STRIDE, CMP_STRIDE, G_B, 2, HD),
                          cmp_pe, cmp_w1, cmp_w2, gain_c)
    gates = pad_tok(seg(COL_SMALL + OFF_GATE, 3 * H_B)).reshape(DB, TOK_PAD, G_B, N_REP, 3)
    gates = gates.transpose(4, 0, 2, 3, 1).reshape(3, DB, G_B, rows, 1)
    q_b = q_rows(COL_QB)
    oc, keep_p, keep_n = sample_cmp_attention(q_b, kcv, gates[0], past_len, DS)
    tab_p, tab_n = bias_tables(bias_b, past_len)
    ocs = sample_attend(page_table, q_b, kv_new(COL_SLC), keep_p, keep_n, tab_p, tab_n, pool_b_slc, gate_prev=(gates[1], oc))
    w_eff = cache_win.shape[1]
    w_pages = w_eff // PAGE_SIZE
    win_pool = cache_win.reshape(DB * w_pages, PAGE_SIZE, G_B, 2, HD)
    win_pt = jnp.arange(DB * w_pages, dtype=jnp.int32).reshape(DB, w_pages)
    keep_w = ((w_eff + tok - jnp.arange(w_eff)[None, :]) < WINDOW).astype(F32).reshape(1, 1, TOK_PAD, w_eff)
    keep_wn = ((new <= tok) & (new < DS)).astype(F32).reshape(1, 1, TOK_PAD, NEW_PAD)
    tab_p, tab_n = bias_tables(bias_b, w_eff)
    o_b = sample_attend(win_pt, q_b, kv_new(COL_WIN), keep_w, keep_wn, tab_p, tab_n, win_pool, gate_prev=(gates[2], ocs))

    n_mem = cache_mem.shape[1]
    o_m = mem_attention(zs.reshape(DB, DS, -1), COL_QM, cache_mem.reshape(DB, n_mem, H_M * 2 * HD_M), DS)
    return unrows(o_a), unrows(o_b), o_m.reshape(DB * DS, H_M * HD_M)


def kernel(x_prompt, x_sample, cache_a_kv, cache_a_idx, cache_b_cmp, cache_b_slc, cache_b_win, cache_mem_kv, page_table, mem_prompt, g_attn, w_in, qk_gain_a, qk_gain_b, qk_gain_m, cmp_pe, cmp_w1, cmp_w2, rel_bias, g_mem, w_mem_kv, w_up_a, w_up_b, w_up_m, w_out, g_ffn, w_ffn_in, w_ffn_out):
    l = 0
    B, T, D = x_prompt.shape
    assert B == 1
    DB, DS = x_sample.shape[:2]
    ki_cols = slice(COL_SMALL + OFF_KI, COL_SMALL + OFF_KI + D_I)
    W = prepare_weights(w_in[l], w_mem_kv[l], w_up_a[l], w_up_b[l], w_up_m[l], w_out[l], w_ffn_in[l], w_ffn_out[l])
    norm_vecs = main_norm_vectors(qk_gain_a[l], qk_gain_b[l], qk_gain_m[l])

    xp = x_prompt.reshape(T, D)
    z, mg = project_in(xp, g_attn[l], W, norm_vecs, 1024)
    n_mem = mem_prompt.shape[1]
    hm = rms_cast(mem_prompt.reshape(n_mem, D), g_mem[l], n_mem)
    kgain = jnp.tile(jnp.concatenate([qk_gain_m[l, 1], jnp.zeros((HD_M,), F32)]), H_M).reshape(1, -1)
    kflag = jnp.tile(jnp.concatenate([jnp.ones((HD_M,), F32), jnp.zeros((HD_M,), F32)]), H_M).reshape(1, -1)
    mkv = matmul_headnorm(hm, W['w_mem_kv'], kgain, jnp.zeros_like(kflag), kflag,
                          jnp.ones((kflag.shape[1] // COL_TILE,), jnp.int32), n_mem, COL_TILE, "mem_kv")
    o_a, o_b, o_m = prompt_mixers(z, mkv, rel_bias, cmp_pe[l], cmp_w1[l], cmp_w2[l], qk_gain_b[l, 1], ATTN_TILE)
    y_p = merge_ffn(xp, o_a, o_b, o_m, mg, W, g_ffn[l], 1024).reshape(B, T, D)
    kv5 = lambda c, rows: z[rows, c:c + G_A * 2 * HD].reshape(1, 1, -1, G_A, 2, HD)
    p_out = (kv5(COL_KVA, slice(None)), z[:, COL_SMALL + OFF_KI:COL_SMALL + OFF_KI + D_I].reshape(1, 1, T, D_I),
             kv5(COL_CMP, slice(None)), kv5(COL_SLC, slice(None)), kv5(COL_WIN, slice(T - min(WINDOW, T), T)),
             mkv.reshape(1, 1, n_mem, H_M, 2, HD_M))

    n_s = DB * DS
    xs = x_sample.reshape(n_s, D)
    zs, mg = project_in(xs, g_attn[l], W, norm_vecs, n_s)
    o_a, o_b, o_m = sample_mixers(zs, page_table, cache_a_kv[l], cache_a_idx[l], cache_b_cmp[l], cache_b_slc[l], cache_b_win[l],
                                  cache_mem_kv[l], rel_bias, cmp_pe[l], cmp_w1[l], cmp_w2[l], qk_gain_b[l, 1], DS)
    y_s = merge_ffn(xs, o_a, o_b, o_m, mg, W, g_ffn[l], n_s).reshape(DB, DS, D)
    new5 = lambda c: zs[:, c:c + G_A * 2 * HD].reshape(1, DB, DS, G_A, 2, HD)
    s_out = (new5(COL_KVA), zs[:, ki_cols].reshape(1, DB, DS, D_I), new5(COL_CMP), new5(COL_SLC),
             jnp.concatenate([cache_b_win[l], new5(COL_WIN)[0]], axis=1)[None, :, DS:])
    return (y_p, y_s) + p_out + s_out
```

```python
import functools
import math

import numpy as np
import jax
import jax.numpy as jnp
from jax import lax
from jax.experimental import pallas as pl
from jax.experimental.pallas import tpu as pltpu

D_MODEL = 4096
PAGE_SIZE = 128
HD = 128
H_A = 12
G_A = 4
H_I = 32
D_I = 64
DSA_TOPK = 256
H_B = 12
G_B = 4
CMP_LEN = 32
CMP_STRIDE = 16
SEL_LEN = 64
N_SEL = 16
WINDOW = 512
N_MEM = 256
H_M = 4
HD_M = 256
N_BUCKETS = 32
MAX_DIST = 128
D_FF = -(-8 * D_MODEL // (3 * 256)) * 256
Q_BLOCK = 128
EPS = 1e-6
NEG = -1e30
BIG = 1e30
PROJ_SIZES = (H_A * HD, G_A * 2 * HD, H_I * D_I, H_I, D_I, H_B * HD, G_B * 2 * HD, G_B * 2 * HD, G_B * 2 * HD, 3 * H_B, H_M * HD_M, 3 * D_MODEL)
PROJ_OFFSETS = tuple(int(o) for o in np.concatenate([[0], np.cumsum(PROJ_SIZES)]))
D_MAIN = PROJ_OFFSETS[-2]
N_REP = H_A // G_A

VMEM_LIMIT_BYTES = 56 * 1024 * 1024
LANE = 128
COL_TILE = 512
ATTN_TILE = 256
FAR_WIDTH = 8
LOG2E = math.log2(math.e)
MXU_DTYPE = jnp.bfloat16
F32 = jnp.float32
INT_MIN = -2 ** 31

COL_QI = 0
COL_QA = COL_QI + H_I * D_I
COL_KVA = COL_QA + H_A * HD
COL_QB = COL_KVA + G_A * 2 * HD
COL_CMP = COL_QB + H_B * HD
COL_SLC = COL_CMP + G_B * 2 * HD
COL_WIN = COL_SLC + G_B * 2 * HD
COL_QM = COL_WIN + G_B * 2 * HD
COL_SMALL = COL_QM + H_M * HD_M
SMALL_W = 256
OFF_WI, OFF_KI, OFF_GATE = 0, H_I, H_I + D_I
N_MAIN = -(-(COL_SMALL + SMALL_W) // COL_TILE) * COL_TILE


def _round_up(n, m):
    return -(-n // m) * m


def _cparams(*sem):
    return pltpu.CompilerParams(dimension_semantics=sem, vmem_limit_bytes=VMEM_LIMIT_BYTES)


def _dot(a, b):
    return jnp.dot(a, b, preferred_element_type=F32)


def _dot_nt(a, b):
    return lax.dot_general(a, b, (((1,), (1,)), ((), ())), preferred_element_type=F32)


def _float_key(x):
    b = pltpu.bitcast(x, jnp.int32)
    return b ^ ((b >> 31) & jnp.int32(0x7FFFFFFF))


def _key_to_float(k):
    return pltpu.bitcast(k ^ ((k >> 31) & jnp.int32(0x7FFFFFFF)), F32)


def _np_float_key(v):
    b = int(np.array(v, np.float32).view(np.int32))
    return b ^ ((b >> 31) & 0x7FFFFFFF)


def _rms_cast_kernel(x_ref, g_ref, o_ref):
    x = x_ref[...]
    ms = jnp.mean(x * x, axis=-1, keepdims=True)
    o_ref[...] = ((x * lax.rsqrt(ms + EPS)) * g_ref[...]).astype(o_ref.dtype)


def rms_cast(x, g, tm):
    M, D = x.shape
    return pl.pallas_call(
        _rms_cast_kernel,
        out_shape=jax.ShapeDtypeStruct((M, D), MXU_DTYPE),
        grid=(M // tm,),
        in_specs=[pl.BlockSpec((tm, D), lambda i: (i, 0)), pl.BlockSpec((1, D), lambda i: (0, 0))],
        out_specs=pl.BlockSpec((tm, D), lambda i: (i, 0)),
        compiler_params=_cparams("parallel"),
        name="rms_cast",
    )(x, g.reshape(1, D))


def _mm_kernel(a_ref, w_ref, o_ref):
    o_ref[...] = _dot(a_ref[...], w_ref[...]).astype(o_ref.dtype)


def matmul(a, w, tm, tn, out_dtype=F32, name="matmul"):
    M, K = a.shape
    N = w.shape[1]
    return pl.pallas_call(
        _mm_kernel,
        out_shape=jax.ShapeDtypeStruct((M, N), out_dtype),
        grid=(M // tm, N // tn),
        in_specs=[pl.BlockSpec((tm, K), lambda i, j: (i, 0)), pl.BlockSpec((K, tn), lambda i, j: (0, j))],
        out_specs=pl.BlockSpec((tm, tn), lambda i, j: (i, j)),
        compiler_params=_cparams("parallel", "arbitrary"),
        name=name,
    )(a, w)


def _norm_cols(z, gain, m128, m256):
    tn = z.shape[1]
    sq = z * z
    ssq = [jnp.sum(sq[:, c * LANE:(c + 1) * LANE], axis=-1, keepdims=True) for c in range(tn // LANE)]
    cols = []
    for c in range(tn // LANE):
        sl = slice(c * LANE, (c + 1) * LANE)
        ms1 = ssq[c] * (1.0 / LANE)
        ms2 = (ssq[c - c % 2] + ssq[c - c % 2 + 1]) * (1.0 / (2 * LANE))
        ms = jnp.where(m256[:, sl] > 0, ms2, ms1)
        zc = z[:, sl]
        zn = (zc * lax.rsqrt(ms + EPS)) * gain[:, sl]
        cols.append(jnp.where((m128[:, sl] + m256[:, sl]) > 0, zn, zc))
    return jnp.concatenate(cols, axis=-1)


def _mm_norm_kernel(mode_ref, a_ref, w_ref, gain_ref, m128_ref, m256_ref, o_ref):
    j = pl.program_id(1)
    z = _dot(a_ref[...], w_ref[...])

    @pl.when(mode_ref[j] == 0)
    def _():
        o_ref[...] = z

    @pl.when(mode_ref[j] != 0)
    def _():
        o_ref[...] = _norm_cols(z, gain_ref[...], m128_ref[...], m256_ref[...])


def matmul_headnorm(a, w, gain, m128, m256, tile_mode, tm, tn, name):
    M, K = a.shape
    N = w.shape[1]
    vec = pl.BlockSpec((1, tn), lambda i, j, m: (0, j))
    return pl.pallas_call(
        _mm_norm_kernel,
        out_shape=jax.ShapeDtypeStruct((M, N), F32),
        grid_spec=pltpu.PrefetchScalarGridSpec(
            num_scalar_prefetch=1,
            grid=(M // tm, N // tn),
            in_specs=[pl.BlockSpec((tm, K), lambda i, j, m: (i, 0)), pl.BlockSpec((K, tn), lambda i, j, m: (0, j)), vec, vec, vec],
            out_specs=pl.BlockSpec((tm, tn), lambda i, j, m: (i, j)),
        ),
        compiler_params=_cparams("parallel", "arbitrary"),
        name=name,
    )(tile_mode, a, w, gain, m128, m256)


def _upmix_kernel(oa_ref, ob_ref, om_ref, wa_ref, wb_ref, wm_ref, g0_ref, g1_ref, g2_ref, o_ref):
    ya = _dot(oa_ref[...].astype(MXU_DTYPE), wa_ref[...])
    yb = _dot(ob_ref[...].astype(MXU_DTYPE), wb_ref[...])
    ym = _dot(om_ref[...].astype(MXU_DTYPE), wm_ref[...])
    mix = jax.nn.sigmoid(g0_ref[...]) * ya + jax.nn.sigmoid(g1_ref[...]) * yb + jax.nn.sigmoid(g2_ref[...]) * ym
    o_ref[...] = mix.astype(o_ref.dtype)


def upmix(o_a, o_b, o_m, w_a, w_b, w_m, mg, tm, tn):
    M = o_a.shape[0]
    D = w_a.shape[1]
    nj = D // tn
    a_spec = lambda k: pl.BlockSpec((tm, k), lambda i, j: (i, 0))
    w_spec = lambda k: pl.BlockSpec((k, tn), lambda i, j: (0, j))
    g_spec = lambda c: pl.BlockSpec((tm, tn), lambda i, j: (i, j + c * nj))
    return pl.pallas_call(
        _upmix_kernel,
        out_shape=jax.ShapeDtypeStruct((M, D), MXU_DTYPE),
        grid=(M // tm, nj),
        in_specs=[a_spec(o_a.shape[1]), a_spec(o_b.shape[1]), a_spec(o_m.shape[1]),
                  w_spec(w_a.shape[0]), w_spec(w_b.shape[0]), w_spec(w_m.shape[0]),
                  g_spec(0), g_spec(1), g_spec(2)],
        out_specs=pl.BlockSpec((tm, tn), lambda i, j: (i, j)),
        compiler_params=_cparams("parallel", "arbitrary"),
        name="upmix",
    )(o_a, o_b, o_m, w_a, w_b, w_m, mg, mg, mg)


def _mm_res_kernel(a_ref, w_ref, r_ref, o_ref):
    o_ref[...] = r_ref[...] + _dot(a_ref[...], w_ref[...])


def matmul_residual(a, w, r, tm, tn):
    M, K = a.shape
    N = w.shape[1]
    return pl.pallas_call(
        _mm_res_kernel,
        out_shape=jax.ShapeDtypeStruct((M, N), F32),
        grid=(M // tm, N // tn),
        in_specs=[pl.BlockSpec((tm, K), lambda i, j: (i, 0)), pl.BlockSpec((K, tn), lambda i, j: (0, j)),
                  pl.BlockSpec((tm, tn), lambda i, j: (i, j))],
        out_specs=pl.BlockSpec((tm, tn), lambda i, j: (i, j)),
        compiler_params=_cparams("parallel", "arbitrary"),
        name="matmul_residual",
    )(a, w, r)


def _swiglu_kernel(h_ref, wa_ref, wu_ref, o_ref):
    h = h_ref[...]
    a = _dot(h, wa_ref[...])
    u = _dot(h, wu_ref[...])
    o_ref[...] = (a * jax.nn.sigmoid(a) * u).astype(o_ref.dtype)


def swiglu_in(h, w_a, w_u, tm, tn):
    M, K = h.shape
    N = w_a.shape[1]
    return pl.pallas_call(
        _swiglu_kernel,
        out_shape=jax.ShapeDtypeStruct((M, N), MXU_DTYPE),
        grid=(M // tm, N // tn),
        in_specs=[pl.BlockSpec((tm, K), lambda i, j: (i, 0)), pl.BlockSpec((K, tn), lambda i, j: (0, j)),
                  pl.BlockSpec((K, tn), lambda i, j: (0, j))],
        out_specs=pl.BlockSpec((tm, tn), lambda i, j: (i, j)),
        compiler_params=_cparams("parallel", "arbitrary"),
        name="swiglu_in",
    )(h, w_a, w_u)


def _mm_res_acc_kernel(a_ref, w_ref, r_ref, o_ref, acc_ref):
    k = pl.program_id(2)

    @pl.when(k == 0)
    def _():
        acc_ref[...] = r_ref[...]

    acc_ref[...] += _dot(a_ref[...], w_ref[...])

    @pl.when(k == pl.num_programs(2) - 1)
    def _():
        o_ref[...] = acc_ref[...]


def matmul_residual_ksplit(a, w, r, tm, tn, tk):
    M, K = a.shape
    N = w.shape[1]
    return pl.pallas_call(
        _mm_res_acc_kernel,
        out_shape=jax.ShapeDtypeStruct((M, N), F32),
        grid=(M // tm, N // tn, K // tk),
        in_specs=[pl.BlockSpec((tm, tk), lambda i, j, k: (i, k)), pl.BlockSpec((tk, tn), lambda i, j, k: (k, j)),
                  pl.BlockSpec((tm, tn), lambda i, j, k: (i, j))],
        out_specs=pl.BlockSpec((tm, tn), lambda i, j, k: (i, j)),
        scratch_shapes=[pltpu.VMEM((tm, tn), F32)],
        compiler_params=_cparams("parallel", "arbitrary", "arbitrary"),
        name="matmul_residual_ksplit",
    )(a, w, r)


def _dsa_index_kernel(q_ref, small_ref, kd_ref, sc_ref, tau_ref, cmp_ref, *, topk, tq):
    i = pl.program_id(0)
    nk = sc_ref.shape[0]
    q = q_ref[...].astype(MXU_DTYPE)
    w = small_ref[:, OFF_WI:OFF_WI + H_I]
    row = lax.broadcasted_iota(jnp.int32, (tq, tq), 0)
    col = lax.broadcasted_iota(jnp.int32, (tq, tq), 1)

    def score_tile(j, _):
        kd = kd_ref[j]
        acc = jnp.zeros((tq, tq), F32)
        for p in range(H_I // 2):
            r = _dot(q[:, p * 2 * D_I:(p + 1) * 2 * D_I], kd)
            acc = acc + jnp.maximum(r[:, :tq], 0.0) * w[:, 2 * p:2 * p + 1] + jnp.maximum(r[:, tq:], 0.0) * w[:, 2 * p + 1:2 * p + 2]
        sc_ref[j] = acc
        causal = (j * tq + col) <= (i * tq + row)
        cmp_ref[j] = jnp.where(causal, acc, NEG)
        return 0

    lax.fori_loop(0, i + 1, score_tile, 0)

    def zero_tile(j, _):
        sc_ref[j] = jnp.zeros((tq, tq), F32)
        return 0

    lax.fori_loop(i + 1, nk, zero_tile, 0)

    def bit_body(b, res_u):
        cand_u = res_u | jnp.left_shift(jnp.int32(1), 31 - b)
        cand = jnp.broadcast_to(_key_to_float(cand_u ^ jnp.int32(INT_MIN)), (tq, LANE))

        def cnt_body(j, c):
            for t in range(tq // LANE):
                c = c + jnp.where(cmp_ref[j, :, t * LANE:(t + 1) * LANE] >= cand, 1.0, 0.0)
            return c

        cnt = lax.fori_loop(0, i + 1, cnt_body, jnp.zeros((tq, LANE), F32))
        total = jnp.sum(cnt, axis=-1, keepdims=True)
        return jnp.where(total >= topk, cand_u, res_u)

    res_u = lax.fori_loop(0, 32, bit_body, jnp.zeros((tq, 1), jnp.int32))
    tau_ref[...] = jnp.where(res_u == 0, -jnp.inf, _key_to_float(res_u ^ jnp.int32(INT_MIN)))


def dsa_index(z, kd, topk, tq):
    T = z.shape[0]
    nk = kd.shape[0]
    return pl.pallas_call(
        functools.partial(_dsa_index_kernel, topk=topk, tq=tq),
        out_shape=(jax.ShapeDtypeStruct((nk, T, tq), F32), jax.ShapeDtypeStruct((T, 1), F32)),
        grid=(T // tq,),
        in_specs=[pl.BlockSpec((tq, H_I * D_I), lambda i: (i, COL_QI // (H_I * D_I))),
                  pl.BlockSpec((tq, SMALL_W), lambda i: (i, COL_SMALL // SMALL_W)),
                  pl.BlockSpec(kd.shape, lambda i: (0, 0, 0))],
        out_specs=(pl.BlockSpec((nk, tq, tq), lambda i: (0, i, 0)), pl.BlockSpec((tq, 1), lambda i: (i, 0))),
        scratch_shapes=[pltpu.VMEM((nk, tq, tq), F32)],
        compiler_params=_cparams("parallel"),
        name="dsa_index",
    )(z, z, kd)


def _flash_tile(q, k, v, bias, masks, state):
    m, l, acc = state
    s = _dot_nt(q, k)
    if bias is not None:
        s = s + bias
    for mask in masks:
        s = jnp.where(mask, s, NEG)
    m_new = jnp.maximum(m, jnp.max(s, axis=-1, keepdims=True))
    alpha = jnp.exp2(m - m_new)
    p = jnp.exp2(s - m_new)
    l = alpha * l + jnp.sum(p, axis=-1, keepdims=True)
    acc = alpha * acc + _dot(p.astype(MXU_DTYPE), v)
    return m_new, l, acc


def _pattn_kernel(*refs, mode, tq):
    q_refs = refs[0:N_REP]
    k_ref, v_ref, bias_ref = refs[N_REP:N_REP + 3]
    rest = refs[N_REP + 3:]
    if mode == 'dsa':
        sc_ref, tau_ref, o_ref = rest
    elif mode == 'slc':
        sel_ref, e_ref, small_ref, prev_ref, o_ref = rest
    else:
        small_ref, prev_ref, o_ref = rest
    g = pl.program_id(0)
    i = pl.program_id(1)
    q = [(r[...] * (HD ** -0.5 * LOG2E)).astype(MXU_DTYPE) for r in q_refs]
    row = lax.broadcasted_iota(jnp.int32, (tq, tq), 0)
    col = lax.broadcasted_iota(jnp.int32, (tq, tq), 1)
    if mode == 'dsa':
        tau = tau_ref[...]
    if mode == 'slc':
        sel = sel_ref[...].astype(MXU_DTYPE)

    def keep_mask(j):
        if mode == 'dsa':
            return sc_ref[j] >= tau
        return _dot(sel, e_ref[j]) > 0.5

    def process(j, state, kind, width=1):
        start = pl.multiple_of(j * tq, tq)
        kt = k_ref[pl.ds(start, width * tq), :].astype(MXU_DTYPE)
        vt = v_ref[pl.ds(start, width * tq), :].astype(MXU_DTYPE)
        masks = []
        if mode in ('dsa', 'slc'):
            parts = [keep_mask(j + w) for w in range(width)]
            masks.append(parts[0] if width == 1 else jnp.concatenate(parts, axis=1))
        elif kind == 'edge':
            masks.append(row < col)
        if kind == 'diag':
            masks.append(row >= col)
        if kind == 'near+diag':
            wide_row = lax.broadcasted_iota(jnp.int32, (tq, 2 * tq), 0)
            wide_col = lax.broadcasted_iota(jnp.int32, (tq, 2 * tq), 1)
            masks.append(wide_col <= wide_row + tq)
        out = []
        for r in range(N_REP):
            if kind in ('far', 'edge'):
                bias = None
            elif kind == 'near+diag':
                bias = jnp.concatenate([bias_ref[0, r, 1], bias_ref[0, r, 0]], axis=1)
            else:
                bias = bias_ref[0, r, 0 if kind == 'diag' else 1]
            out.append(_flash_tile(q[r], kt, vt, bias, masks, state[r]))
        return tuple(out)

    state = tuple((jnp.full((tq, 1), NEG, F32), jnp.zeros((tq, 1), F32), jnp.zeros((tq, HD), F32)) for _ in range(N_REP))
    if mode == 'win':
        n_win = WINDOW // tq
        state = lax.cond(i >= n_win, lambda s: process(i - n_win, s, 'edge'), lambda s: s, state)
        for d in range(n_win - 1, 1, -1):
            state = lax.cond(i >= d, functools.partial(lambda s, d: process(i - d, s, 'far'), d=d), lambda s: s, state)
    else:
        n_far = jnp.maximum(i - 1, 0)
        n_wide = n_far // FAR_WIDTH
        state = lax.fori_loop(0, n_wide, lambda jj, s: process(jj * FAR_WIDTH, s, 'far', FAR_WIDTH), state)
        done = n_wide * FAR_WIDTH
        width = FAR_WIDTH // 2
        while width >= 1:
            take = ((n_far - done) >= width).astype(jnp.int32)
            state = lax.cond(take == 1, functools.partial(lambda s, d, w: process(d, s, 'far', w), d=done, w=width), lambda s: s, state)
            done = done + take * width
            width //= 2
    if mode == 'win':
        state = lax.cond(i >= 1, lambda s: process(i - 1, s, 'near'), lambda s: s, state)
        state = process(i, state, 'diag')
    else:
        state = lax.cond(i >= 1, lambda s: process(i - 1, s, 'near+diag', 2), lambda s: process(i, s, 'diag'), state)

    if mode != 'dsa':
        branch = 1 if mode == 'slc' else 2
        gates = jax.nn.sigmoid(small_ref[...])
        lane = lax.broadcasted_iota(jnp.int32, gates.shape, 1)
    for r in range(N_REP):
        m, l, acc = state[r]
        o = acc / l
        if mode != 'dsa':
            gcol = OFF_GATE + (g * N_REP + r) * 3 + branch
            o = prev_ref[:, r * HD:(r + 1) * HD] + o * jnp.sum(jnp.where(lane == gcol, gates, 0.0), axis=-1, keepdims=True)
        o_ref[:, r * HD:(r + 1) * HD] = o


def prompt_attention(mode, z, q_col, kv_col, bias_near, tq, extra):
    T = z.shape[0]
    G = G_A
    qb = q_col // HD
    kb = kv_col // HD
    q_specs = [pl.BlockSpec((tq, HD), functools.partial(lambda g, i, r: (i, qb + g * N_REP + r), r=r)) for r in range(N_REP)]
    in_specs = q_specs + [
        pl.BlockSpec((T, HD), lambda g, i: (0, kb + 2 * g)),
        pl.BlockSpec((T, HD), lambda g, i: (0, kb + 2 * g + 1)),
        pl.BlockSpec((1, N_REP, 2, tq, tq), lambda g, i: (g, 0, 0, 0, 0)),
    ]
    args = [z] * N_REP + [z, z, bias_near]
    small_spec = pl.BlockSpec((tq, SMALL_W), lambda g, i: (i, COL_SMALL // SMALL_W))
    prev_spec = pl.BlockSpec((tq, N_REP * HD), lambda g, i: (i, g))
    if mode == 'dsa':
        scores, tau = extra
        nk = scores.shape[0]
        in_specs += [pl.BlockSpec((nk, tq, tq), lambda g, i: (0, i, 0)), pl.BlockSpec((tq, 1), lambda g, i: (i, 0))]
        args += [scores, tau]
    elif mode == 'slc':
        sel, e, prev = extra
        nsb = e.shape[1]
        in_specs += [pl.BlockSpec((tq, nsb), lambda g, i: (i, g)), pl.BlockSpec(e.shape, lambda g, i: (0, 0, 0)), small_spec, prev_spec]
        args += [sel, e, z, prev]
    else:
        (prev,) = extra
        in_specs += [small_spec, prev_spec]
        args += [z, prev]
    return pl.pallas_call(
        functools.partial(_pattn_kernel, mode=mode, tq=tq),
        out_shape=jax.ShapeDtypeStruct((T, G * N_REP * HD), F32),
        grid=(G, T // tq),
        in_specs=in_specs,
        out_specs=pl.BlockSpec((tq, N_REP * HD), lambda g, i: (i, g)),
        compiler_params=_cparams("parallel", "parallel"),
        name="attn_" + mode,
    )(*args)


def _compress_kernel(x_ref, w1_ref, pe_ref, w2_ref, gain_ref, o_ref):
    c = pl.program_id(0) % 2
    nch = x_ref.shape[0]
    w1 = w1_ref[0]
    hid0 = jnp.zeros((nch, HD), F32)
    hid1 = jnp.zeros((nch, HD), F32)
    for s in range(CMP_STRIDE):
        xs = x_ref[:, s, :].astype(MXU_DTYPE)
        hid0 = hid0 + _dot(xs, w1[0, s].astype(MXU_DTYPE))
        hid1 = hid1 + _dot(xs, w1[1, s].astype(MXU_DTYPE))
    pe_term = _dot(pe_ref[0].astype(MXU_DTYPE), w1.reshape(CMP_LEN * HD, HD).astype(MXU_DTYPE))
    hid = pe_term + hid0 + pltpu.roll(hid1, nch - 1, 0)
    out = _dot((hid * jax.nn.sigmoid(hid)).astype(MXU_DTYPE), w2_ref[0].astype(MXU_DTYPE))
    ms = jnp.mean(out * out, axis=-1, keepdims=True)
    normed = (out * lax.rsqrt(ms + EPS)) * gain_ref[...]
    o_ref[0] = jnp.where(c == 0, normed, out)


def compress_blocks(z3, col, cmp_pe, cmp_w1, cmp_w2, gain_c):
    nch = z3.shape[0]
    r = CMP_LEN // CMP_STRIDE
    w1 = cmp_w1.reshape(2, r, CMP_STRIDE, HD, HD)
    pe = cmp_pe.reshape(CMP_LEN, 2, HD).transpose(1, 0, 2).reshape(2, 1, CMP_LEN * HD)
    cb = col // HD
    return pl.pallas_call(
        _compress_kernel,
        out_shape=jax.ShapeDtypeStruct((G_B * 2, nch, HD), F32),
        grid=(G_B * 2,),
        in_specs=[pl.BlockSpec((nch, CMP_STRIDE, HD), lambda gc: (0, 0, cb + gc)),
                  pl.BlockSpec((1, r, CMP_STRIDE, HD, HD), lambda gc: (gc % 2, 0, 0, 0, 0)),
                  pl.BlockSpec((1, 1, CMP_LEN * HD), lambda gc: (gc % 2, 0, 0)),
                  pl.BlockSpec((1, HD, HD), lambda gc: (gc % 2, 0, 0)),
                  pl.BlockSpec((1, HD), lambda gc: (0, 0))],
        out_specs=pl.BlockSpec((1, nch, HD), lambda gc: (gc, 0, 0)),
        compiler_params=_cparams("parallel"),
        name="nsa_compress",
    )(z3, w1, pe, cmp_w2, gain_c.reshape(1, HD))


def _nsa_cmp_kernel(*refs, tq, n_sel):
    q_refs = refs[0:N_REP]
    kc_ref, vc_ref, map_ref, small_ref, oc_ref, sel_ref = refs[N_REP:]
    g = pl.program_id(0)
    i = pl.program_id(1)
    kc = kc_ref[0].astype(MXU_DTYPE)
    vc = vc_ref[0].astype(MXU_DTYPE)
    smap = map_ref[...]
    nch = kc.shape[0]
    nsb = smap.shape[1]
    t = i * tq + lax.broadcasted_iota(jnp.int32, (tq, nch), 0)
    n = lax.broadcasted_iota(jnp.int32, (tq, nch), 1)
    valid = (n * CMP_STRIDE + CMP_LEN - 1) <= t
    gates = jax.nn.sigmoid(small_ref[...])
    lane = lax.broadcasted_iota(jnp.int32, gates.shape, 1)
    imp = jnp.zeros((tq, nsb), F32)
    for r in range(N_REP):
        q = q_refs[r][...].astype(MXU_DTYPE)
        lm = jnp.where(valid, _dot_nt(q, kc) * (HD ** -0.5), NEG)
        m = jnp.max(lm, axis=-1, keepdims=True)
        p = jnp.where(valid, jnp.exp(lm - m), 0.0)
        den = jnp.sum(p, axis=-1, keepdims=True)
        pc = (p / jnp.where(den > 0.0, den, 1.0)).astype(MXU_DTYPE)
        imp = imp + _dot(pc, smap)
        gcol = OFF_GATE + (g * N_REP + r) * 3
        gate = jnp.sum(jnp.where(lane == gcol, gates, 0.0), axis=-1, keepdims=True)
        oc_ref[:, r * HD:(r + 1) * HD] = _dot(pc, vc) * gate

    tpos = i * tq + lax.broadcasted_iota(jnp.int32, (tq, nsb), 0)
    blk = lax.broadcasted_iota(jnp.int32, (tq, nsb), 1)
    cur = tpos // SEL_LEN
    start_ok = blk <= cur
    forced_or_imp = jnp.where(blk == 0, BIG, jnp.where(blk == cur, BIG, jnp.where(blk == cur - 1, BIG, imp)))
    score = jnp.where(start_ok, forced_or_imp, NEG)
    blk_f = blk.astype(F32)
    sel = jnp.zeros((tq, nsb), F32)
    for _ in range(n_sel):
        best = jnp.max(score, axis=-1, keepdims=True)
        first = jnp.min(jnp.where(score == best, blk_f, float(nsb)), axis=-1, keepdims=True)
        hit = blk_f == first
        sel = jnp.where(hit, 1.0, sel)
        score = jnp.where(hit, -jnp.inf, score)
    sel_ref[...] = sel


def nsa_cmp_attention(z, q_col, kcv, sel_map, tq):
    T = z.shape[0]
    nch = kcv.shape[1]
    nsb = sel_map.shape[1]
    qb = q_col // HD
    q_specs = [pl.BlockSpec((tq, HD), functools.partial(lambda g, i, r: (i, qb + g * N_REP + r), r=r)) for r in range(N_REP)]
    return pl.pallas_call(
        functools.partial(_nsa_cmp_kernel, tq=tq, n_sel=min(N_SEL, nsb)),
        out_shape=(jax.ShapeDtypeStruct((T, H_B * HD), F32), jax.ShapeDtypeStruct((T, G_B * nsb), F32)),
        grid=(G_B, T // tq),
        in_specs=q_specs + [pl.BlockSpec((1, nch, HD), lambda g, i: (2 * g, 0, 0)),
                            pl.BlockSpec((1, nch, HD), lambda g, i: (2 * g + 1, 0, 0)),
                            pl.BlockSpec((nch, nsb), lambda g, i: (0, 0)),
                            pl.BlockSpec((tq, SMALL_W), lambda g, i: (i, COL_SMALL // SMALL_W))],
        out_specs=(pl.BlockSpec((tq, N_REP * HD), lambda g, i: (i, g)), pl.BlockSpec((tq, nsb), lambda g, i: (i, g))),
        compiler_params=_cparams("parallel", "parallel"),
        name="nsa_cmp",
    )(*([z] * N_REP), kcv, kcv, sel_map, z)


def _mem_attn_kernel(q_ref, k_ref, v_ref, o_ref):
    q = q_ref[0].astype(MXU_DTYPE)
    s = _dot_nt(q, k_ref[0].astype(MXU_DTYPE)) * (HD_M ** -0.5)
    p = jnp.exp(s - jnp.max(s, axis=-1, keepdims=True))
    p = p / jnp.sum(p, axis=-1, keepdims=True)
    o_ref[0] = _dot(p.astype(MXU_DTYPE), v_ref[0].astype(MXU_DTYPE))


def mem_attention(z3, q_col, mkv3, tq):
    B, Tq, _ = z3.shape
    n_mem = mkv3.shape[1]
    qb = q_col // HD_M
    return pl.pallas_call(
        _mem_attn_kernel,
        out_shape=jax.ShapeDtypeStruct((B, Tq, H_M * HD_M), F32),
        grid=(B, H_M, Tq // tq),
        in_specs=[pl.BlockSpec((1, tq, HD_M), lambda b, h, i: (b, i, qb + h)),
                  pl.BlockSpec((1, n_mem, HD_M), lambda b, h, i: (b, 0, 2 * h)),
                  pl.BlockSpec((1, n_mem, HD_M), lambda b, h, i: (b, 0, 2 * h + 1))],
        out_specs=pl.BlockSpec((1, tq, HD_M), lambda b, h, i: (b, i, h)),
        compiler_params=_cparams("parallel", "parallel", "arbitrary"),
        name="mem_attn",
    )(z3, mkv3, mkv3)


TOK_PAD = 8
NEW_PAD = 128


def _start_pages(pt_ref, b, n_pages, copies):
    def issue(p, _):
        for cp in copies(p, pt_ref[b, p]):
            cp.start()
        return 0

    lax.fori_loop(0, n_pages, issue, 0)


def _wait_pages(pt_ref, b, n_pages, copies):
    def wait(p, _):
        for cp in copies(p, pt_ref[b, p]):
            cp.wait()
        return 0

    lax.fori_loop(0, n_pages, wait, 0)


def _page_rows(pool_ref, page, row):
    return pool_ref.at[pl.ds(page * PAGE_SIZE, PAGE_SIZE), row, :]


def _sample_index_kernel(pt_ref, q_ref, w_ref, knew_ref, pool_ref, mp_ref, mn_ref, kbuf, sem, *, topk, n_pages, past_len, chunk):
    b = pl.program_id(0)
    copies = lambda p, page: [pltpu.make_async_copy(pool_ref.at[page], kbuf.at[pl.ds(p * PAGE_SIZE, PAGE_SIZE)], sem)]
    _start_pages(pt_ref, b, n_pages, copies)
    _wait_pages(pt_ref, b, n_pages, copies)
    q = q_ref[0].astype(MXU_DTYPE)
    w = w_ref[0]

    def head_sum(keys):
        x = jnp.maximum(_dot_nt(q, keys.astype(MXU_DTYPE)), 0.0) * w
        return jnp.sum(x.reshape(TOK_PAD, H_I, x.shape[1]), axis=1)

    sc = [head_sum(kbuf[c * chunk:(c + 1) * chunk, :]) for c in range(past_len // chunk)]
    sc_new = head_sum(knew_ref[0])
    tok = lax.broadcasted_iota(jnp.int32, (TOK_PAD, NEW_PAD), 0)
    new = lax.broadcasted_iota(jnp.int32, (TOK_PAD, NEW_PAD), 1)
    causal_new = new <= tok
    keys = [_float_key(s) for s in sc]
    key_new = jnp.where(causal_new, _float_key(sc_new), jnp.int32(_np_float_key(NEG)))

    def bit_body(i, res_u):
        cand_u = res_u | jnp.left_shift(jnp.int32(1), 31 - i)
        cand_s = cand_u ^ jnp.int32(INT_MIN)
        part = jnp.where(key_new >= cand_s, 1.0, 0.0)
        for k in keys:
            hit = jnp.where(k >= cand_s, 1.0, 0.0)
            for t in range(chunk // LANE):
                part = part + hit[:, t * LANE:(t + 1) * LANE]
        cnt = jnp.sum(part, axis=-1, keepdims=True)
        return jnp.where(cnt >= topk, cand_u, res_u)

    res_u = lax.fori_loop(0, 32, bit_body, jnp.zeros((TOK_PAD, 1), jnp.int32))
    tau_key = res_u ^ jnp.int32(INT_MIN)
    tau_bits = tau_key ^ ((tau_key >> 31) & jnp.int32(0x7FFFFFFF))
    tau = jnp.where(res_u == 0, -jnp.inf, pltpu.bitcast(tau_bits, F32))
    for c, s in enumerate(sc):
        mp_ref[0, 0, :, c * chunk:(c + 1) * chunk] = jnp.where(s >= tau, 1.0, 0.0)
    mn_ref[0, 0] = jnp.where(causal_new, jnp.where(sc_new >= tau, 1.0, 0.0), 0.0)


def sample_index(page_table, q, w, k_new, pool, topk):
    DB, n_pages = page_table.shape
    past_len = n_pages * PAGE_SIZE
    chunk = min(1024, past_len)
    return pl.pallas_call(
        functools.partial(_sample_index_kernel, topk=topk, n_pages=n_pages, past_len=past_len, chunk=chunk),
        out_shape=(jax.ShapeDtypeStruct((DB, 1, TOK_PAD, past_len), F32), jax.ShapeDtypeStruct((DB, 1, TOK_PAD, NEW_PAD), F32)),
        grid_spec=pltpu.PrefetchScalarGridSpec(
            num_scalar_prefetch=1,
            grid=(DB,),
            in_specs=[pl.BlockSpec((1, TOK_PAD * H_I, D_I), lambda b, pt: (b, 0, 0)),
                      pl.BlockSpec((1, TOK_PAD * H_I, 1), lambda b, pt: (b, 0, 0)),
                      pl.BlockSpec((1, NEW_PAD, D_I), lambda b, pt: (b, 0, 0)),
                      pl.BlockSpec(memory_space=pl.ANY)],
            out_specs=(pl.BlockSpec((1, 1, TOK_PAD, past_len), lambda b, pt: (b, 0, 0, 0)),
                       pl.BlockSpec((1, 1, TOK_PAD, NEW_PAD), lambda b, pt: (b, 0, 0, 0))),
            scratch_shapes=[pltpu.VMEM((past_len, D_I), F32), pltpu.SemaphoreType.DMA(())],
        ),
        compiler_params=_cparams("arbitrary"),
        name="sample_index",
    )(page_table, q, w, k_new, pool)


def _sample_attend_kernel(pt_ref, q_ref, kvn_ref, mp_ref, mn_ref, tp_ref, tn_ref, *rest, gated, n_pages, past_len, chunk):
    if gated:
        gate_ref, prev_ref, pool_ref, o_ref, kbuf, vbuf, sem = rest
    else:
        pool_ref, o_ref, kbuf, vbuf, sem = rest
    b = pl.program_id(0)
    g = pl.program_id(1)
    slot = lambda buf, p: buf.at[pl.ds(p * PAGE_SIZE, PAGE_SIZE)]
    copies = lambda p, page: [pltpu.make_async_copy(_page_rows(pool_ref, page, 2 * g), slot(kbuf, p), sem),
                              pltpu.make_async_copy(_page_rows(pool_ref, page, 2 * g + 1), slot(vbuf, p), sem)]
    _start_pages(pt_ref, b, n_pages, copies)
    _wait_pages(pt_ref, b, n_pages, copies)
    q = q_ref[0, 0].astype(MXU_DTYPE)

    def logits(keys, bias_ref, mask_ref, sl):
        s = _dot_nt(q, keys.astype(MXU_DTYPE)) * (HD ** -0.5)
        bias = jnp.concatenate([bias_ref[r, :, sl] for r in range(N_REP)], axis=0)
        keep = mask_ref[0, 0, :, sl] > 0.5
        return jnp.where(jnp.concatenate([keep] * N_REP, axis=0), s + bias, NEG)

    n_chunks = past_len // chunk
    s_past = [logits(kbuf[c * chunk:(c + 1) * chunk, :], tp_ref, mp_ref, slice(c * chunk, (c + 1) * chunk)) for c in range(n_chunks)]
    s_new = logits(kvn_ref[0, 0, 0], tn_ref, mn_ref, slice(0, NEW_PAD))
    m = jnp.max(s_new, axis=-1, keepdims=True)
    for s in s_past:
        m = jnp.maximum(m, jnp.max(s, axis=-1, keepdims=True))
    p_new = jnp.exp(s_new - m)
    l = jnp.sum(p_new, axis=-1, keepdims=True)
    acc = _dot(p_new.astype(MXU_DTYPE), kvn_ref[0, 0, 1].astype(MXU_DTYPE))
    for c, s in enumerate(s_past):
        p = jnp.exp(s - m)
        l = l + jnp.sum(p, axis=-1, keepdims=True)
        acc = acc + _dot(p.astype(MXU_DTYPE), vbuf[c * chunk:(c + 1) * chunk, :].astype(MXU_DTYPE))
    o = acc / l
    if gated:
        o = prev_ref[0, 0] + o * jax.nn.sigmoid(gate_ref[0, 0])
    o_ref[0, 0] = o


def sample_attend(page_table, q, kv_new, mask_past, mask_new, tab_past, tab_new, pool, gate_prev=None):
    DB, n_pages = page_table.shape
    G = q.shape[1]
    past_len = n_pages * PAGE_SIZE
    chunk = min(1024, past_len)
    rows = N_REP * TOK_PAD
    mb, mg = mask_past.shape[0] > 1, mask_past.shape[1] > 1
    mask_map = lambda b, g, pt: (b if mb else 0, g if mg else 0, 0, 0)
    row_spec = lambda n: pl.BlockSpec((1, 1, rows, n), lambda b, g, pt: (b, g, 0, 0))
    in_specs = [row_spec(HD),
                pl.BlockSpec((1, 1, 2, NEW_PAD, HD), lambda b, g, pt: (b, g, 0, 0, 0)),
                pl.BlockSpec((1, 1, TOK_PAD, past_len), mask_map),
                pl.BlockSpec((1, 1, TOK_PAD, NEW_PAD), mask_map),
                pl.BlockSpec((N_REP, TOK_PAD, past_len), lambda b, g, pt: (g, 0, 0)),
                pl.BlockSpec((N_REP, TOK_PAD, NEW_PAD), lambda b, g, pt: (g, 0, 0))]
    args = [q, kv_new, mask_past, mask_new, tab_past, tab_new]
    if gate_prev is not None:
        in_specs += [row_spec(1), row_spec(HD)]
        args += list(gate_prev)
    in_specs.append(pl.BlockSpec(memory_space=pl.ANY))
    args.append(pool)
    return pl.pallas_call(
        functools.partial(_sample_attend_kernel, gated=gate_prev is not None, n_pages=n_pages, past_len=past_len, chunk=chunk),
        out_shape=jax.ShapeDtypeStruct((DB, G, rows, HD), F32),
        grid_spec=pltpu.PrefetchScalarGridSpec(
            num_scalar_prefetch=1,
            grid=(DB, G),
            in_specs=in_specs,
            out_specs=row_spec(HD),
            scratch_shapes=[pltpu.VMEM((past_len, HD), F32), pltpu.VMEM((past_len, HD), F32), pltpu.SemaphoreType.DMA(())],
        ),
        compiler_params=_cparams("arbitrary", "arbitrary"),
        name="sample_attend",
    )(page_table, *args)


def _sample_compress_kernel(pt_ref, cnew_ref, w1_ref, pe_ref, w2_ref, gain_ref, pool_ref, o_ref, cbuf, sem, *, n_pages, n_blocks):
    b = pl.program_id(0)
    g = pl.program_id(1)
    slot = lambda c, p: cbuf.at[c, pl.ds(p * PAGE_SIZE, PAGE_SIZE)]
    copies = lambda p, page: [pltpu.make_async_copy(_page_rows(pool_ref, page, 2 * g + c), slot(c, p), sem) for c in range(2)]
    _start_pages(pt_ref, b, n_pages, copies)
    n_cached = n_pages * PAGE_SIZE
    n_tok = cbuf.shape[1]
    for c in range(2):
        cbuf[c, n_cached:n_cached + CMP_STRIDE] = cnew_ref[0, 0, c]
        cbuf[c, n_cached + CMP_STRIDE:n_tok] = jnp.zeros((n_tok - n_cached - CMP_STRIDE, HD), F32)
    _wait_pages(pt_ref, b, n_pages, copies)

    nch = n_tok // CMP_STRIDE
    for c in range(2):
        acc = jnp.zeros((nch + 8, 2 * HD), F32)
        for sp in range(CMP_STRIDE // 2):
            rows = [cbuf.at[c][pl.ds(2 * sp + d, nch, stride=CMP_STRIDE), :] for d in range(2)]
            lhs = jnp.concatenate([jnp.concatenate(rows, axis=1), pe_ref[c, sp]], axis=0).astype(MXU_DTYPE)
            acc = acc + _dot(lhs, w1_ref[c, sp])
        pe_term = acc[nch:nch + 1, 0:HD] + acc[nch + 1:nch + 2, HD:2 * HD]
        hid = pe_term + acc[0:nch, 0:HD] + pltpu.roll(acc[0:nch, HD:2 * HD], nch - 1, 0)
        out = _dot((hid * jax.nn.sigmoid(hid)).astype(MXU_DTYPE), w2_ref[c])[0:n_blocks]
        if c == 0:
            ms = jnp.mean(out * out, axis=-1, keepdims=True)
            out = (out * lax.rsqrt(ms + EPS)) * gain_ref[...]
        o_ref[0, c] = out


def sample_compress(page_table, c_new, pool, cmp_pe, cmp_w1, cmp_w2, gain_c):
    DB, n_pages = page_table.shape
    n_blocks = n_pages * (PAGE_SIZE // CMP_STRIDE)
    n_slots = _round_up(n_blocks + 1, 8)
    r = CMP_LEN // CMP_STRIDE
    assert r == 2
    half = CMP_STRIDE // 2
    w1 = cmp_w1.reshape(2, r, half, 2, HD, HD).transpose(0, 2, 3, 4, 1, 5).reshape(2, half, 2 * HD, r * HD).astype(MXU_DTYPE)
    pe = cmp_pe.reshape(r, half, 2, 2, HD).transpose(3, 1, 0, 2, 4).reshape(2, half, r, 2 * HD)
    pe = jnp.pad(pe, ((0, 0), (0, 0), (0, 8 - r), (0, 0)))
    full = lambda a: pl.BlockSpec(a.shape, lambda b, g, pt: (0,) * a.ndim)
    w2 = cmp_w2.astype(MXU_DTYPE)
    gain = gain_c.reshape(1, HD)
    return pl.pallas_call(
        functools.partial(_sample_compress_kernel, n_pages=n_pages, n_blocks=n_blocks),
        out_shape=jax.ShapeDtypeStruct((DB, G_B * 2, n_blocks, HD), F32),
        grid_spec=pltpu.PrefetchScalarGridSpec(
            num_scalar_prefetch=1,
            grid=(DB, G_B),
            in_specs=[pl.BlockSpec((1, 1, 2, CMP_STRIDE, HD), lambda b, g, pt: (b, g, 0, 0, 0)),
                      full(w1), full(pe), full(w2), full(gain), pl.BlockSpec(memory_space=pl.ANY)],
            out_specs=pl.BlockSpec((1, 2, n_blocks, HD), lambda b, g, pt: (b, g, 0, 0)),
            scratch_shapes=[pltpu.VMEM((2, n_slots * CMP_STRIDE, HD), F32), pltpu.SemaphoreType.DMA(())],
        ),
        compiler_params=_cparams("arbitrary", "arbitrary"),
        name="sample_compress",
    )(page_table, c_new, w1, pe, w2, gain, pool)


def _sample_cmp_kernel(q_ref, kc_ref, vc_ref, map_ref, e_ref, gate_ref, oc_ref, mp_ref, mn_ref, *, past_len, n_real, n_sel):
    q = q_ref[0, 0].astype(MXU_DTYPE)
    kc = kc_ref[0, 0].astype(MXU_DTYPE)
    vc = vc_ref[0, 0].astype(MXU_DTYPE)
    rows = q.shape[0]
    nch = kc.shape[0]
    t = past_len + (lax.broadcasted_iota(jnp.int32, (rows, nch), 0) & (TOK_PAD - 1))
    n = lax.broadcasted_iota(jnp.int32, (rows, nch), 1)
    valid = (n * CMP_STRIDE + CMP_LEN - 1) <= t
    lm = jnp.where(valid, _dot_nt(q, kc) * (HD ** -0.5), NEG)
    m = jnp.max(lm, axis=-1, keepdims=True)
    p = jnp.where(valid, jnp.exp(lm - m), 0.0)
    den = jnp.sum(p, axis=-1, keepdims=True)
    pc = (p / jnp.where(den > 0.0, den, 1.0)).astype(MXU_DTYPE)
    oc_ref[0, 0] = _dot(pc, vc) * jax.nn.sigmoid(gate_ref[0, 0])
    imp_rows = _dot(pc, map_ref[...])
    imp = imp_rows[0:TOK_PAD]
    for r in range(1, N_REP):
        imp = imp + imp_rows[r * TOK_PAD:(r + 1) * TOK_PAD]

    nsb = imp.shape[1]
    tpos = past_len + lax.broadcasted_iota(jnp.int32, (TOK_PAD, nsb), 0)
    blk = lax.broadcasted_iota(jnp.int32, (TOK_PAD, nsb), 1)
    cur = tpos // SEL_LEN
    forced_or_imp = jnp.where(blk == 0, BIG, jnp.where(blk == cur, BIG, jnp.where(blk == cur - 1, BIG, imp)))
    score = jnp.where(blk <= cur, forced_or_imp, jnp.where(blk < n_real, NEG, -jnp.inf))
    blk_f = blk.astype(F32)
    sel = jnp.zeros((TOK_PAD, nsb), F32)
    for _ in range(n_sel):
        best = jnp.max(score, axis=-1, keepdims=True)
        first = jnp.min(jnp.where(score == best, blk_f, float(nsb)), axis=-1, keepdims=True)
        hit = blk_f == first
        sel = jnp.where(hit, 1.0, sel)
        score = jnp.where(hit, -jnp.inf, score)
    keep = _dot(sel.astype(MXU_DTYPE), e_ref[...])
    mp_ref[0, 0] = jnp.where(keep[:, 0:past_len] > 0.5, 1.0, 0.0)
    tok = lax.broadcasted_iota(jnp.int32, (TOK_PAD, NEW_PAD), 0)
    new = lax.broadcasted_iota(jnp.int32, (TOK_PAD, NEW_PAD), 1)
    mn_ref[0, 0] = jnp.where(new <= tok, jnp.where(keep[:, past_len:] > 0.5, 1.0, 0.0), 0.0)


def sample_cmp_attention(q, kcv, gate, past_len, n_tokens):
    DB, G, rows, _ = q.shape
    nc = kcv.shape[2]
    n_real = -(-(past_len + n_tokens) // SEL_LEN)
    nsb = _round_up(n_real, LANE)
    sel_map = cmp_to_sel(nc, nsb).astype(MXU_DTYPE)
    key = jnp.arange(past_len + NEW_PAD)[None, :]
    expand = ((key // SEL_LEN == jnp.arange(nsb)[:, None]) & (key < past_len + n_tokens)).astype(MXU_DTYPE)
    row_spec = lambda n: pl.BlockSpec((1, 1, rows, n), lambda b, g: (b, g, 0, 0))
    return pl.pallas_call(
        functools.partial(_sample_cmp_kernel, past_len=past_len, n_real=n_real, n_sel=min(N_SEL, n_real)),
        out_shape=(jax.ShapeDtypeStruct((DB, G, rows, HD), F32), jax.ShapeDtypeStruct((DB, G, TOK_PAD, past_len), F32),
                   jax.ShapeDtypeStruct((DB, G, TOK_PAD, NEW_PAD), F32)),
        grid=(DB, G),
        in_specs=[row_spec(HD),
                  pl.BlockSpec((1, 1, nc, HD), lambda b, g: (b, 2 * g, 0, 0)),
                  pl.BlockSpec((1, 1, nc, HD), lambda b, g: (b, 2 * g + 1, 0, 0)),
                  pl.BlockSpec(sel_map.shape, lambda b, g: (0, 0)),
                  pl.BlockSpec(expand.shape, lambda b, g: (0, 0)),
                  row_spec(1)],
        out_specs=(row_spec(HD), pl.BlockSpec((1, 1, TOK_PAD, past_len), lambda b, g: (b, g, 0, 0)),
                   pl.BlockSpec((1, 1, TOK_PAD, NEW_PAD), lambda b, g: (b, g, 0, 0))),
        compiler_params=_cparams("parallel", "parallel"),
        name="sample_cmp",
    )(q, kcv, kcv, sel_map, expand, gate)


def t5_bucket(dist):
    max_exact = N_BUCKETS // 2
    n = jnp.maximum(dist, 0)
    nf = jnp.maximum(n, 1).astype(jnp.float32)
    large = max_exact + (jnp.log(nf / max_exact) / math.log(MAX_DIST / max_exact) * (N_BUCKETS - max_exact)).astype(jnp.int32)
    return jnp.where(n < max_exact, n, jnp.minimum(large, N_BUCKETS - 1))


def cmp_to_sel(n_cmp, n_sel_blocks):
    cs = jnp.arange(n_cmp)[:, None] * CMP_STRIDE
    ss = jnp.arange(n_sel_blocks)[None, :] * SEL_LEN
    return ((cs < ss + SEL_LEN) & (cs + CMP_LEN > ss)).astype(jnp.float32)


def toeplitz_bias_tiles(bias_tab, tq):
    max_exact = N_BUCKETS // 2
    nearest_far = np.float32(tq + 1)
    assert max_exact + int(np.log(nearest_far / max_exact) / math.log(MAX_DIST / max_exact) * (N_BUCKETS - max_exact)) >= N_BUCKETS - 1
    rel =(bias_tab - bias_tab[N_BUCKETS - 1]) * LOG2E
    ii = jnp.arange(tq)[:, None]
    jj = jnp.arange(tq)[None, :]
    tiles = jnp.stack([rel[t5_bucket(ii - jj)], rel[t5_bucket(tq + ii - jj)]])
    return tiles.transpose(3, 0, 1, 2).reshape(G_A, N_REP, 2, tq, tq)


def block_expand_matrix(n_blocks, n_tiles, tk):
    b = jnp.arange(n_blocks)[None, :, None]
    key = (jnp.arange(n_tiles)[:, None, None] * tk + jnp.arange(tk)[None, None, :])
    return (key // SEL_LEN == b).astype(MXU_DTYPE)


def indexer_key_tiles(ki, tk):
    S = ki.shape[0]
    kt = ki.reshape(S // tk, tk, D_I).transpose(0, 2, 1).astype(MXU_DTYPE)
    zero = jnp.zeros_like(kt)
    return jnp.concatenate([jnp.concatenate([kt, zero], axis=2), jnp.concatenate([zero, kt], axis=2)], axis=1)


def _pad_cols(w, n):
    return jnp.pad(w, ((0, 0), (0, n - w.shape[1])))


def _seg(w, idx):
    return w[:, PROJ_OFFSETS[idx]:PROJ_OFFSETS[idx + 1]]


def prepare_weights(w_in, w_mem_kv, w_up_a, w_up_b, w_up_m, w_out, w_ffn_in, w_ffn_out):
    bf = MXU_DTYPE
    d_ff_pad = _round_up(D_FF, COL_TILE)
    main = jnp.concatenate([_seg(w_in, 2), _seg(w_in, 0), _seg(w_in, 1), _seg(w_in, 5), _seg(w_in, 6), _seg(w_in, 7),
                            _seg(w_in, 8), _seg(w_in, 10), _seg(w_in, 3), _seg(w_in, 4), _seg(w_in, 9)], axis=1)
    return dict(
        w_main=_pad_cols(main, N_MAIN).astype(bf),
        w_mg=_seg(w_in, 11).astype(bf),
        w_mem_kv=w_mem_kv.astype(bf),
        w_up_a=w_up_a.astype(bf), w_up_b=w_up_b.astype(bf), w_up_m=w_up_m.astype(bf),
        w_out=w_out.astype(bf),
        w_ffn_a=_pad_cols(w_ffn_in[:, :D_FF], d_ff_pad).astype(bf),
        w_ffn_u=_pad_cols(w_ffn_in[:, D_FF:], d_ff_pad).astype(bf),
        w_ffn_out=jnp.pad(w_ffn_out, ((0, d_ff_pad - D_FF), (0, 0))).astype(bf),
    )


def main_norm_vectors(qk_gain_a, qk_gain_b, qk_gain_m):
    one, zero = jnp.ones((HD,), F32), jnp.zeros((HD,), F32)

    def kv(gk):
        return jnp.tile(jnp.concatenate([gk, zero]), G_A), jnp.tile(jnp.concatenate([one, zero]), G_A)

    kva, fa = kv(qk_gain_a[1])
    slc, fs = kv(qk_gain_b[2])
    win, fw = kv(qk_gain_b[3])
    z = lambda n: jnp.zeros((n,), F32)
    gain = jnp.concatenate([z(H_I * D_I), jnp.tile(qk_gain_a[0], H_A), kva, jnp.tile(qk_gain_b[0], H_B), z(G_B * 2 * HD), slc, win,
                            jnp.tile(qk_gain_m[0], H_M), z(N_MAIN - COL_SMALL)])
    m128 = jnp.concatenate([z(H_I * D_I), jnp.ones((H_A * HD,), F32), fa, jnp.ones((H_B * HD,), F32), z(G_B * 2 * HD), fs, fw,
                            z(H_M * HD_M), z(N_MAIN - COL_SMALL)])
    m256 = jnp.concatenate([z(COL_QM), jnp.ones((H_M * HD_M,), F32), z(N_MAIN - COL_SMALL)])
    cols = np.arange(N_MAIN)
    has = ((cols >= COL_QA) & (cols < COL_CMP)) | ((cols >= COL_SLC) & (cols < COL_SMALL))
    tile_mode = jnp.asarray(has.reshape(-1, COL_TILE).any(axis=1).astype(np.int32))
    return gain.reshape(1, -1), m128.reshape(1, -1), m256.reshape(1, -1), tile_mode


def project_in(x2d, g_attn, W, norm_vecs, tm):
    h = rms_cast(x2d, g_attn, min(tm, 512))
    gain, m128, m256, tile_mode = norm_vecs
    z_main = matmul_headnorm(h, W['w_main'], gain, m128, m256, tile_mode, tm, COL_TILE, "in_proj_main")
    mg = matmul(h, W['w_mg'], tm, COL_TILE, name="in_proj_mg")
    return z_main, mg


def merge_ffn(x2d, o_a, o_b, o_m, mg, W, g_ffn, tm):
    mix = upmix(o_a, o_b, o_m, W['w_up_a'], W['w_up_b'], W['w_up_m'], mg, min(tm, 512), COL_TILE)
    x2 = matmul_residual(mix, W['w_out'], x2d, tm, COL_TILE)
    h2 = rms_cast(x2, g_ffn, min(tm, 512))
    act = swiglu_in(h2, W['w_ffn_a'], W['w_ffn_u'], tm, COL_TILE)
    tk = act.shape[1] // 4
    return matmul_residual_ksplit(act, W['w_ffn_out'], x2, tm, COL_TILE, tk)


def prompt_mixers(z, mkv, rel_bias, cmp_pe, cmp_w1, cmp_w2, gain_c, tq):
    T = z.shape[0]
    nk = T // tq
    bias_a, bias_b = rel_bias[:, :H_A], rel_bias[:, H_A:]
    near_a, near_b = toeplitz_bias_tiles(bias_a, tq), toeplitz_bias_tiles(bias_b, tq)
    ki = z[:, COL_SMALL + OFF_KI:COL_SMALL + OFF_KI + D_I]
    scores, tau = dsa_index(z, indexer_key_tiles(ki, tq), min(DSA_TOPK, T // 4), tq)
    o_a = prompt_attention('dsa', z, COL_QA, COL_KVA, near_a, tq, (scores, tau))
    nch = T // CMP_STRIDE
    nsb = T // SEL_LEN
    kcv = compress_blocks(z.reshape(nch, CMP_STRIDE, z.shape[1]), COL_CMP, cmp_pe, cmp_w1, cmp_w2, gain_c)
    sel_map = cmp_to_sel(nch, nsb).astype(MXU_DTYPE)
    oc, sel = nsa_cmp_attention(z, COL_QB, kcv, sel_map, tq)
    ocs = prompt_attention('slc', z, COL_QB, COL_SLC, near_b, tq, (sel, block_expand_matrix(nsb, nk, tq), oc))
    o_b = prompt_attention('win', z, COL_QB, COL_WIN, near_b, tq, (ocs,))
    o_m = mem_attention(z[None], COL_QM, mkv[None], tq)[0]
    return o_a, o_b, o_m


def sample_mixers(zs, page_table, pool_a_kv, pool_a_idx, pool_b_cmp, pool_b_slc, cache_win, cache_mem, rel_bias, cmp_pe, cmp_w1, cmp_w2, gain_c, DS):
    DB, n_pages = page_table.shape
    past_len = n_pages * PAGE_SIZE
    bias_a, bias_b = rel_bias[:, :H_A], rel_bias[:, H_A:]
    seg = lambda c, n: zs[:, c:c + n].reshape(DB, DS, n)
    pad_tok = lambda a: jnp.pad(a, ((0, 0), (0, TOK_PAD - DS)) + ((0, 0),) * (a.ndim - 2))
    rows = N_REP * TOK_PAD

    def q_rows(c):
        q = pad_tok(seg(c, H_A * HD)).reshape(DB, TOK_PAD, G_A, N_REP, HD)
        return q.transpose(0, 2, 3, 1, 4).reshape(DB, G_A, rows, HD)

    def kv_new(c):
        kv = seg(c, G_A * 2 * HD).reshape(DB, DS, G_A, 2, HD).transpose(0, 2, 3, 1, 4)
        return jnp.pad(kv, ((0, 0), (0, 0), (0, 0), (0, NEW_PAD - DS), (0, 0)))

    tok = jnp.arange(TOK_PAD)[:, None]
    new = jnp.arange(NEW_PAD)[None, :]

    def bias_tables(bias_tab, span):
        near = min(span, NEW_PAD)
        assert near == span or near + 1 >= MAX_DIST
        dist_near = near + tok - jnp.arange(near)[None, :]
        tab_near = bias_tab[t5_bucket(dist_near)].transpose(2, 0, 1)
        tab_far = jnp.broadcast_to(bias_tab[N_BUCKETS - 1][:, None, None], (bias_tab.shape[1], TOK_PAD, span - near))
        return jnp.concatenate([tab_far, tab_near], axis=2), bias_tab[t5_bucket(tok - new)].transpose(2, 0, 1)

    unrows = lambda o: o.reshape(DB, G_A, N_REP, TOK_PAD, HD)[:, :, :, :DS].transpose(0, 3, 1, 2, 4).reshape(DB * DS, H_A * HD)

    qi = pad_tok(seg(COL_QI, H_I * D_I)).reshape(DB, TOK_PAD * H_I, D_I)
    wi = pad_tok(seg(COL_SMALL + OFF_WI, H_I)).reshape(DB, TOK_PAD * H_I, 1)
    ki_new = jnp.pad(seg(COL_SMALL + OFF_KI, D_I), ((0, 0), (0, NEW_PAD - DS), (0, 0)))
    keep_p, keep_n = sample_index(page_table, qi, wi, ki_new, pool_a_idx, min(DSA_TOPK, (past_len + DS) // 4))
    tab_p, tab_n = bias_tables(bias_a, past_len)
    token_rows = lambda pool: pool.reshape(-1, G_A * 2, HD)
    o_a = sample_attend(page_table, q_rows(COL_QA), kv_new(COL_KVA), keep_p, keep_n, tab_p, tab_n, token_rows(pool_a_kv))

    c_new = seg(COL_CMP, G_B * 2 * HD).reshape(DB, DS, G_B, 2, HD).transpose(0, 2, 3, 1, 4)
    c_new = jnp.pad(c_new, ((0, 0), (0, 0), (0, 0), (0, CMP_STRIDE - DS), (0, 0)))
    kcv = sample_compress(page_table, c_new, token_rows(pool_b_cmp), cmp_pe, cmp_w1, cmp_w2, gain_c)
    gates = pad_tok(seg(COL_SMALL + OFF_GATE, 3 * H_B)).reshape(DB, TOK_PAD, G_B, N_REP, 3)
    gates = gates.transpose(4, 0, 2, 3, 1).reshape(3, DB, G_B, rows, 1)
    q_b = q_rows(COL_QB)
    oc, keep_p, keep_n = sample_cmp_attention(q_b, kcv, gates[0], past_len, DS)
    tab_p, tab_n = bias_tables(bias_b, past_len)
    ocs = sample_attend(page_table, q_b, kv_new(COL_SLC), keep_p, keep_n, tab_p, tab_n, token_rows(pool_b_slc), gate_prev=(gates[1], oc))
    w_eff = cache_win.shape[1]
    w_pages = w_eff // PAGE_SIZE
    win_pool = token_rows(cache_win)
    win_pt = jnp.arange(DB * w_pages, dtype=jnp.int32).reshape(DB, w_pages)
    keep_w = ((w_eff + tok - jnp.arange(w_eff)[None, :]) < WINDOW).astype(F32).reshape(1, 1, TOK_PAD, w_eff)
    keep_wn = ((new <= tok) & (new < DS)).astype(F32).reshape(1, 1, TOK_PAD, NEW_PAD)
    tab_p, tab_n = bias_tables(bias_b, w_eff)
    o_b = sample_attend(win_pt, q_b, kv_new(COL_WIN), keep_w, keep_wn, tab_p, tab_n, win_pool, gate_prev=(gates[2], ocs))

    n_mem = cache_mem.shape[1]
    o_m = mem_attention(zs.reshape(DB, DS, -1), COL_QM, cache_mem.reshape(DB, n_mem, H_M * 2 * HD_M), DS)
    return unrows(o_a), unrows(o_b), o_m.reshape(DB * DS, H_M * HD_M)


def kernel(x_prompt, x_sample, cache_a_kv, cache_a_idx, cache_b_cmp, cache_b_slc, cache_b_win, cache_mem_kv, page_table, mem_prompt, g_attn, w_in, qk_gain_a, qk_gain_b, qk_gain_m, cmp_pe, cmp_w1, cmp_w2, rel_bias, g_mem, w_mem_kv, w_up_a, w_up_b, w_up_m, w_out, g_ffn, w_ffn_in, w_ffn_out):
    l = 0
    B, T, D = x_prompt.shape
    assert B == 1
    DB, DS = x_sample.shape[:2]
    ki_cols = slice(COL_SMALL + OFF_KI, COL_SMALL + OFF_KI + D_I)
    W = prepare_weights(w_in[l], w_mem_kv[l], w_up_a[l], w_up_b[l], w_up_m[l], w_out[l], w_ffn_in[l], w_ffn_out[l])
    norm_vecs = main_norm_vectors(qk_gain_a[l], qk_gain_b[l], qk_gain_m[l])

    xp = x_prompt.reshape(T, D)
    z, mg = project_in(xp, g_attn[l], W, norm_vecs, 1024)
    n_mem = mem_prompt.shape[1]
    hm = rms_cast(mem_prompt.reshape(n_mem, D), g_mem[l], n_mem)
    kgain = jnp.tile(jnp.concatenate([qk_gain_m[l, 1], jnp.zeros((HD_M,), F32)]), H_M).reshape(1, -1)
    kflag = jnp.tile(jnp.concatenate([jnp.ones((HD_M,), F32), jnp.zeros((HD_M,), F32)]), H_M).reshape(1, -1)
    mkv = matmul_headnorm(hm, W['w_mem_kv'], kgain, jnp.zeros_like(kflag), kflag,
                          jnp.ones((kflag.shape[1] // COL_TILE,), jnp.int32), n_mem, COL_TILE, "mem_kv")
    o_a, o_b, o_m = prompt_mixers(z, mkv, rel_bias, cmp_pe[l], cmp_w1[l], cmp_w2[l], qk_gain_b[l, 1], ATTN_TILE)
    y_p = merge_ffn(xp, o_a, o_b, o_m, mg, W, g_ffn[l], 1024).reshape(B, T, D)
    kv5 = lambda c, rows: z[rows, c:c + G_A * 2 * HD].reshape(1, 1, -1, G_A, 2, HD)
    p_out = (kv5(COL_KVA, slice(None)), z[:, COL_SMALL + OFF_KI:COL_SMALL + OFF_KI + D_I].reshape(1, 1, T, D_I),
             kv5(COL_CMP, slice(None)), kv5(COL_SLC, slice(None)), kv5(COL_WIN, slice(T - min(WINDOW, T), T)),
             mkv.reshape(1, 1, n_mem, H_M, 2, HD_M))

    n_s = DB * DS
    xs = x_sample.reshape(n_s, D)
    zs, mg = project_in(xs, g_attn[l], W, norm_vecs, n_s)
    o_a, o_b, o_m = sample_mixers(zs, page_table, cache_a_kv[l], cache_a_idx[l], cache_b_cmp[l], cache_b_slc[l], cache_b_win[l],
                                  cache_mem_kv[l], rel_bias, cmp_pe[l], cmp_w1[l], cmp_w2[l], qk_gain_b[l, 1], DS)
    y_s = merge_ffn(xs, o_a, o_b, o_m, mg, W, g_ffn[l], n_s).reshape(DB, DS, D)
    new5 = lambda c: zs[:, c:c + G_A * 2 * HD].reshape(1, DB, DS, G_A, 2, HD)
    s_out = (new5(COL_KVA), zs[:, ki_cols].reshape(1, DB, DS, D_I), new5(COL_CMP), new5(COL_SLC),
             jnp.concatenate([cache_b_win[l], new5(COL_WIN)[0]], axis=1)[None, :, DS:])
    return (y_p, y_s) + p_out + s_out
```

```python
import functools
import math

import numpy as np
import jax
import jax.numpy as jnp
from jax import lax
from jax.experimental import pallas as pl
from jax.experimental.pallas import tpu as pltpu

D_MODEL = 4096
PAGE_SIZE = 128
HD = 128
H_A = 12
G_A = 4
H_I = 32
D_I = 64
DSA_TOPK = 256
H_B = 12
G_B = 4
CMP_LEN = 32
CMP_STRIDE = 16
SEL_LEN = 64
N_SEL = 16
WINDOW = 512
N_MEM = 256
H_M = 4
HD_M = 256
N_BUCKETS = 32
MAX_DIST = 128
D_FF = -(-8 * D_MODEL // (3 * 256)) * 256
Q_BLOCK = 128
EPS = 1e-6
NEG = -1e30
BIG = 1e30
PROJ_SIZES = (H_A * HD, G_A * 2 * HD, H_I * D_I, H_I, D_I, H_B * HD, G_B * 2 * HD, G_B * 2 * HD, G_B * 2 * HD, 3 * H_B, H_M * HD_M, 3 * D_MODEL)
PROJ_OFFSETS = tuple(int(o) for o in np.concatenate([[0], np.cumsum(PROJ_SIZES)]))
D_MAIN = PROJ_OFFSETS[-2]
N_REP = H_A // G_A

VMEM_LIMIT_BYTES = 56 * 1024 * 1024
LANE = 128
COL_TILE = 512
ATTN_TILE = 256
FAR_WIDTH = 8
LOG2E = math.log2(math.e)
MXU_DTYPE = jnp.bfloat16
F32 = jnp.float32
INT_MIN = -2 ** 31

COL_QI = 0
COL_QA = COL_QI + H_I * D_I
COL_KVA = COL_QA + H_A * HD
COL_QB = COL_KVA + G_A * 2 * HD
COL_CMP = COL_QB + H_B * HD
COL_SLC = COL_CMP + G_B * 2 * HD
COL_WIN = COL_SLC + G_B * 2 * HD
COL_QM = COL_WIN + G_B * 2 * HD
COL_SMALL = COL_QM + H_M * HD_M
SMALL_W = 256
OFF_WI, OFF_KI, OFF_GATE = 0, H_I, H_I + D_I
N_MAIN = -(-(COL_SMALL + SMALL_W) // COL_TILE) * COL_TILE


def _round_up(n, m):
    return -(-n // m) * m


def _cparams(*sem):
    return pltpu.CompilerParams(dimension_semantics=sem, vmem_limit_bytes=VMEM_LIMIT_BYTES)


def _dot(a, b):
    return jnp.dot(a, b, preferred_element_type=F32)


def _dot_nt(a, b):
    return lax.dot_general(a, b, (((1,), (1,)), ((), ())), preferred_element_type=F32)


def _float_key(x):
    b = pltpu.bitcast(x, jnp.int32)
    return b ^ ((b >> 31) & jnp.int32(0x7FFFFFFF))


def _key_to_float(k):
    return pltpu.bitcast(k ^ ((k >> 31) & jnp.int32(0x7FFFFFFF)), F32)


def _np_float_key(v):
    b = int(np.array(v, np.float32).view(np.int32))
    return b ^ ((b >> 31) & 0x7FFFFFFF)


def _rms_cast_kernel(x_ref, g_ref, o_ref):
    x = x_ref[...]
    ms = jnp.mean(x * x, axis=-1, keepdims=True)
    o_ref[...] = ((x * lax.rsqrt(ms + EPS)) * g_ref[...]).astype(o_ref.dtype)


def rms_cast(x, g, tm):
    M, D = x.shape
    return pl.pallas_call(
        _rms_cast_kernel,
        out_shape=jax.ShapeDtypeStruct((M, D), MXU_DTYPE),
        grid=(M // tm,),
        in_specs=[pl.BlockSpec((tm, D), lambda i: (i, 0)), pl.BlockSpec((1, D), lambda i: (0, 0))],
        out_specs=pl.BlockSpec((tm, D), lambda i: (i, 0)),
        compiler_params=_cparams("parallel"),
        name="rms_cast",
    )(x, g.reshape(1, D))


def _mm_kernel(a_ref, w_ref, o_ref):
    o_ref[...] = _dot(a_ref[...], w_ref[...]).astype(o_ref.dtype)


def matmul(a, w, tm, tn, out_dtype=F32, name="matmul"):
    M, K = a.shape
    N = w.shape[1]
    return pl.pallas_call(
        _mm_kernel,
        out_shape=jax.ShapeDtypeStruct((M, N), out_dtype),
        grid=(M // tm, N // tn),
        in_specs=[pl.BlockSpec((tm, K), lambda i, j: (i, 0)), pl.BlockSpec((K, tn), lambda i, j: (0, j))],
        out_specs=pl.BlockSpec((tm, tn), lambda i, j: (i, j)),
        compiler_params=_cparams("parallel", "arbitrary"),
        name=name,
    )(a, w)


def _norm_cols(z, gain, m128, m256):
    tn = z.shape[1]
    sq = z * z
    ssq = [jnp.sum(sq[:, c * LANE:(c + 1) * LANE], axis=-1, keepdims=True) for c in range(tn // LANE)]
    cols = []
    for c in range(tn // LANE):
        sl = slice(c * LANE, (c + 1) * LANE)
        ms1 = ssq[c] * (1.0 / LANE)
        ms2 = (ssq[c - c % 2] + ssq[c - c % 2 + 1]) * (1.0 / (2 * LANE))
        ms = jnp.where(m256[:, sl] > 0, ms2, ms1)
        zc = z[:, sl]
        zn = (zc * lax.rsqrt(ms + EPS)) * gain[:, sl]
        cols.append(jnp.where((m128[:, sl] + m256[:, sl]) > 0, zn, zc))
    return jnp.concatenate(cols, axis=-1)


def _mm_norm_kernel(mode_ref, a_ref, w_ref, gain_ref, m128_ref, m256_ref, o_ref):
    j = pl.program_id(1)
    z = _dot(a_ref[...], w_ref[...])

    @pl.when(mode_ref[j] == 0)
    def _():
        o_ref[...] = z

    @pl.when(mode_ref[j] != 0)
    def _():
        o_ref[...] = _norm_cols(z, gain_ref[...], m128_ref[...], m256_ref[...])


def matmul_headnorm(a, w, gain, m128, m256, tile_mode, tm, tn, name):
    M, K = a.shape
    N = w.shape[1]
    vec = pl.BlockSpec((1, tn), lambda i, j, m: (0, j))
    return pl.pallas_call(
        _mm_norm_kernel,
        out_shape=jax.ShapeDtypeStruct((M, N), F32),
        grid_spec=pltpu.PrefetchScalarGridSpec(
            num_scalar_prefetch=1,
            grid=(M // tm, N // tn),
            in_specs=[pl.BlockSpec((tm, K), lambda i, j, m: (i, 0)), pl.BlockSpec((K, tn), lambda i, j, m: (0, j)), vec, vec, vec],
            out_specs=pl.BlockSpec((tm, tn), lambda i, j, m: (i, j)),
        ),
        compiler_params=_cparams("parallel", "arbitrary"),
        name=name,
    )(tile_mode, a, w, gain, m128, m256)


def _upmix_kernel(oa_ref, ob_ref, om_ref, wa_ref, wb_ref, wm_ref, g0_ref, g1_ref, g2_ref, o_ref):
    ya = _dot(oa_ref[...].astype(MXU_DTYPE), wa_ref[...])
    yb = _dot(ob_ref[...].astype(MXU_DTYPE), wb_ref[...])
    ym = _dot(om_ref[...].astype(MXU_DTYPE), wm_ref[...])
    mix = jax.nn.sigmoid(g0_ref[...]) * ya + jax.nn.sigmoid(g1_ref[...]) * yb + jax.nn.sigmoid(g2_ref[...]) * ym
    o_ref[...] = mix.astype(o_ref.dtype)


def upmix(o_a, o_b, o_m, w_a, w_b, w_m, mg, tm, tn):
    M = o_a.shape[0]
    D = w_a.shape[1]
    nj = D // tn
    a_spec = lambda k: pl.BlockSpec((tm, k), lambda i, j: (i, 0))
    w_spec = lambda k: pl.BlockSpec((k, tn), lambda i, j: (0, j))
    g_spec = lambda c: pl.BlockSpec((tm, tn), lambda i, j: (i, j + c * nj))
    return pl.pallas_call(
        _upmix_kernel,
        out_shape=jax.ShapeDtypeStruct((M, D), MXU_DTYPE),
        grid=(M // tm, nj),
        in_specs=[a_spec(o_a.shape[1]), a_spec(o_b.shape[1]), a_spec(o_m.shape[1]),
                  w_spec(w_a.shape[0]), w_spec(w_b.shape[0]), w_spec(w_m.shape[0]),
                  g_spec(0), g_spec(1), g_spec(2)],
        out_specs=pl.BlockSpec((tm, tn), lambda i, j: (i, j)),
        compiler_params=_cparams("parallel", "arbitrary"),
        name="upmix",
    )(o_a, o_b, o_m, w_a, w_b, w_m, mg, mg, mg)


def _mm_res_kernel(a_ref, w_ref, r_ref, o_ref):
    o_ref[...] = r_ref[...] + _dot(a_ref[...], w_ref[...])


def matmul_residual(a, w, r, tm, tn):
    M, K = a.shape
    N = w.shape[1]
    return pl.pallas_call(
        _mm_res_kernel,
        out_shape=jax.ShapeDtypeStruct((M, N), F32),
        grid=(M // tm, N // tn),
        in_specs=[pl.BlockSpec((tm, K), lambda i, j: (i, 0)), pl.BlockSpec((K, tn), lambda i, j: (0, j)),
                  pl.BlockSpec((tm, tn), lambda i, j: (i, j))],
        out_specs=pl.BlockSpec((tm, tn), lambda i, j: (i, j)),
        compiler_params=_cparams("parallel", "arbitrary"),
        name="matmul_residual",
    )(a, w, r)


def _swiglu_kernel(h_ref, wa_ref, wu_ref, o_ref):
    h = h_ref[...]
    a = _dot(h, wa_ref[...])
    u = _dot(h, wu_ref[...])
    o_ref[...] = (a * jax.nn.sigmoid(a) * u).astype(o_ref.dtype)


def swiglu_in(h, w_a, w_u, tm, tn):
    M, K = h.shape
    N = w_a.shape[1]
    return pl.pallas_call(
        _swiglu_kernel,
        out_shape=jax.ShapeDtypeStruct((M, N), MXU_DTYPE),
        grid=(M // tm, N // tn),
        in_specs=[pl.BlockSpec((tm, K), lambda i, j: (i, 0)), pl.BlockSpec((K, tn), lambda i, j: (0, j)),
                  pl.BlockSpec((K, tn), lambda i, j: (0, j))],
        out_specs=pl.BlockSpec((tm, tn), lambda i, j: (i, j)),
        compiler_params=_cparams("parallel", "arbitrary"),
        name="swiglu_in",
    )(h, w_a, w_u)


def _mm_res_acc_kernel(a_ref, w_ref, r_ref, o_ref, acc_ref):
    k = pl.program_id(2)

    @pl.when(k == 0)
    def _():
        acc_ref[...] = r_ref[...]

    acc_ref[...] += _dot(a_ref[...], w_ref[...])

    @pl.when(k == pl.num_programs(2) - 1)
    def _():
        o_ref[...] = acc_ref[...]


def matmul_residual_ksplit(a, w, r, tm, tn, tk):
    M, K = a.shape
    N = w.shape[1]
    return pl.pallas_call(
        _mm_res_acc_kernel,
        out_shape=jax.ShapeDtypeStruct((M, N), F32),
        grid=(M // tm, N // tn, K // tk),
        in_specs=[pl.BlockSpec((tm, tk), lambda i, j, k: (i, k)), pl.BlockSpec((tk, tn), lambda i, j, k: (k, j)),
                  pl.BlockSpec((tm, tn), lambda i, j, k: (i, j))],
        out_specs=pl.BlockSpec((tm, tn), lambda i, j, k: (i, j)),
        scratch_shapes=[pltpu.VMEM((tm, tn), F32)],
        compiler_params=_cparams("parallel", "arbitrary", "arbitrary"),
        name="matmul_residual_ksplit",
    )(a, w, r)


def _dsa_index_kernel(q_ref, small_ref, kd_ref, sc_ref, tau_ref, cmp_ref, *, topk, tq):
    i = pl.program_id(0)
    nk = sc_ref.shape[0]
    q = q_ref[...].astype(MXU_DTYPE)
    w = small_ref[:, OFF_WI:OFF_WI + H_I]
    row = lax.broadcasted_iota(jnp.int32, (tq, tq), 0)
    col = lax.broadcasted_iota(jnp.int32, (tq, tq), 1)

    def score_tile(j, _):
        kd = kd_ref[j]
        acc = jnp.zeros((tq, tq), F32)
        for p in range(H_I // 2):
            r = _dot(q[:, p * 2 * D_I:(p + 1) * 2 * D_I], kd)
            acc = acc + jnp.maximum(r[:, :tq], 0.0) * w[:, 2 * p:2 * p + 1] + jnp.maximum(r[:, tq:], 0.0) * w[:, 2 * p + 1:2 * p + 2]
        sc_ref[j] = acc
        causal = (j * tq + col) <= (i * tq + row)
        cmp_ref[j] = jnp.where(causal, acc, NEG)
        return 0

    lax.fori_loop(0, i + 1, score_tile, 0)

    def zero_tile(j, _):
        sc_ref[j] = jnp.zeros((tq, tq), F32)
        return 0

    lax.fori_loop(i + 1, nk, zero_tile, 0)

    def bit_body(b, res_u):
        cand_u = res_u | jnp.left_shift(jnp.int32(1), 31 - b)
        cand = jnp.broadcast_to(_key_to_float(cand_u ^ jnp.int32(INT_MIN)), (tq, LANE))

        def cnt_body(j, c):
            for t in range(tq // LANE):
                c = c + jnp.where(cmp_ref[j, :, t * LANE:(t + 1) * LANE] >= cand, 1.0, 0.0)
            return c

        cnt = lax.fori_loop(0, i + 1, cnt_body, jnp.zeros((tq, LANE), F32))
        total = jnp.sum(cnt, axis=-1, keepdims=True)
        return jnp.where(total >= topk, cand_u, res_u)

    res_u = lax.fori_loop(0, 32, bit_body, jnp.zeros((tq, 1), jnp.int32))
    tau_ref[...] = jnp.where(res_u == 0, -jnp.inf, _key_to_float(res_u ^ jnp.int32(INT_MIN)))


def dsa_index(z, kd, topk, tq):
    T = z.shape[0]
    nk = kd.shape[0]
    return pl.pallas_call(
        functools.partial(_dsa_index_kernel, topk=topk, tq=tq),
        out_shape=(jax.ShapeDtypeStruct((nk, T, tq), F32), jax.ShapeDtypeStruct((T, 1), F32)),
        grid=(T // tq,),
        in_specs=[pl.BlockSpec((tq, H_I * D_I), lambda i: (i, COL_QI // (H_I * D_I))),
                  pl.BlockSpec((tq, SMALL_W), lambda i: (i, COL_SMALL // SMALL_W)),
                  pl.BlockSpec(kd.shape, lambda i: (0, 0, 0))],
        out_specs=(pl.BlockSpec((nk, tq, tq), lambda i: (0, i, 0)), pl.BlockSpec((tq, 1), lambda i: (i, 0))),
        scratch_shapes=[pltpu.VMEM((nk, tq, tq), F32)],
        compiler_params=_cparams("parallel"),
        name="dsa_index",
    )(z, z, kd)


def _flash_tile(q, k, v, bias, masks, state):
    m, l, acc = state
    s = _dot_nt(q, k)
    if bias is not None:
        s = s + bias
    for mask in masks:
        s = jnp.where(mask, s, NEG)
    m_new = jnp.maximum(m, jnp.max(s, axis=-1, keepdims=True))
    alpha = jnp.exp2(m - m_new)
    p = jnp.exp2(s - m_new)
    l = alpha * l + jnp.sum(p, axis=-1, keepdims=True)
    acc = alpha * acc + _dot(p.astype(MXU_DTYPE), v)
    return m_new, l, acc


def _pattn_kernel(*refs, mode, tq):
    q_refs = refs[0:N_REP]
    k_ref, v_ref, bias_ref = refs[N_REP:N_REP + 3]
    rest = refs[N_REP + 3:]
    if mode == 'dsa':
        sc_ref, tau_ref, o_ref = rest
    elif mode == 'slc':
        sel_ref, e_ref, small_ref, prev_ref, o_ref = rest
    else:
        small_ref, prev_ref, o_ref = rest
    g = pl.program_id(0)
    i = pl.program_id(1)
    q = [(r[...] * (HD ** -0.5 * LOG2E)).astype(MXU_DTYPE) for r in q_refs]
    row = lax.broadcasted_iota(jnp.int32, (tq, tq), 0)
    col = lax.broadcasted_iota(jnp.int32, (tq, tq), 1)
    if mode == 'dsa':
        tau = tau_ref[...]
    if mode == 'slc':
        sel = sel_ref[...].astype(MXU_DTYPE)

    def keep_mask(j):
        if mode == 'dsa':
            return sc_ref[j] >= tau
        return _dot(sel, e_ref[j]) > 0.5

    def process(j, state, kind, width=1):
        start = pl.multiple_of(j * tq, tq)
        kt = k_ref[pl.ds(start, width * tq), :].astype(MXU_DTYPE)
        vt = v_ref[pl.ds(start, width * tq), :].astype(MXU_DTYPE)
        masks = []
        if mode in ('dsa', 'slc'):
            parts = [keep_mask(j + w) for w in range(width)]
            masks.append(parts[0] if width == 1 else jnp.concatenate(parts, axis=1))
        elif kind == 'edge':
            masks.append(row < col)
        if kind == 'diag':
            masks.append(row >= col)
        if kind == 'near+diag':
            wide_row = lax.broadcasted_iota(jnp.int32, (tq, 2 * tq), 0)
            wide_col = lax.broadcasted_iota(jnp.int32, (tq, 2 * tq), 1)
            masks.append(wide_col <= wide_row + tq)
        out = []
        for r in range(N_REP):
            if kind in ('far', 'edge'):
                bias = None
            elif kind == 'near+diag':
                bias = jnp.concatenate([bias_ref[0, r, 1], bias_ref[0, r, 0]], axis=1)
            else:
                bias = bias_ref[0, r, 0 if kind == 'diag' else 1]
            out.append(_flash_tile(q[r], kt, vt, bias, masks, state[r]))
        return tuple(out)

    state = tuple((jnp.full((tq, 1), NEG, F32), jnp.zeros((tq, 1), F32), jnp.zeros((tq, HD), F32)) for _ in range(N_REP))
    if mode == 'win':
        n_win = WINDOW // tq
        state = lax.cond(i >= n_win, lambda s: process(i - n_win, s, 'edge'), lambda s: s, state)
        for d in range(n_win - 1, 1, -1):
            state = lax.cond(i >= d, functools.partial(lambda s, d: process(i - d, s, 'far'), d=d), lambda s: s, state)
    else:
        n_far = jnp.maximum(i - 1, 0)
        n_wide = n_far // FAR_WIDTH
        state = lax.fori_loop(0, n_wide, lambda jj, s: process(jj * FAR_WIDTH, s, 'far', FAR_WIDTH), state)
        done = n_wide * FAR_WIDTH
        width = FAR_WIDTH // 2
        while width >= 1:
            take = ((n_far - done) >= width).astype(jnp.int32)
            state = lax.cond(take == 1, functools.partial(lambda s, d, w: process(d, s, 'far', w), d=done, w=width), lambda s: s, state)
            done = done + take * width
            width //= 2
    if mode == 'win':
        state = lax.cond(i >= 1, lambda s: process(i - 1, s, 'near'), lambda s: s, state)
        state = process(i, state, 'diag')
    else:
        state = lax.cond(i >= 1, lambda s: process(i - 1, s, 'near+diag', 2), lambda s: process(i, s, 'diag'), state)

    if mode != 'dsa':
        branch = 1 if mode == 'slc' else 2
        gates = jax.nn.sigmoid(small_ref[...])
        lane = lax.broadcasted_iota(jnp.int32, gates.shape, 1)
    for r in range(N_REP):
        m, l, acc = state[r]
        o = acc / l
        if mode != 'dsa':
            gcol = OFF_GATE + (g * N_REP + r) * 3 + branch
            o = prev_ref[:, r * HD:(r + 1) * HD] + o * jnp.sum(jnp.where(lane == gcol, gates, 0.0), axis=-1, keepdims=True)
        o_ref[:, r * HD:(r + 1) * HD] = o


def prompt_attention(mode, z, q_col, kv_col, bias_near, tq, extra):
    T = z.shape[0]
    G = G_A
    qb = q_col // HD
    kb = kv_col // HD
    q_specs = [pl.BlockSpec((tq, HD), functools.partial(lambda g, i, r: (i, qb + g * N_REP + r), r=r)) for r in range(N_REP)]
    in_specs = q_specs + [
        pl.BlockSpec((T, HD), lambda g, i: (0, kb + 2 * g)),
        pl.BlockSpec((T, HD), lambda g, i: (0, kb + 2 * g + 1)),
        pl.BlockSpec((1, N_REP, 2, tq, tq), lambda g, i: (g, 0, 0, 0, 0)),
    ]
    args = [z] * N_REP + [z, z, bias_near]
    small_spec = pl.BlockSpec((tq, SMALL_W), lambda g, i: (i, COL_SMALL // SMALL_W))
    prev_spec = pl.BlockSpec((tq, N_REP * HD), lambda g, i: (i, g))
    if mode == 'dsa':
        scores, tau = extra
        nk = scores.shape[0]
        in_specs += [pl.BlockSpec((nk, tq, tq), lambda g, i: (0, i, 0)), pl.BlockSpec((tq, 1), lambda g, i: (i, 0))]
        args += [scores, tau]
    elif mode == 'slc':
        sel, e, prev = extra
        nsb = e.shape[1]
        in_specs += [pl.BlockSpec((tq, nsb), lambda g, i: (i, g)), pl.BlockSpec(e.shape, lambda g, i: (0, 0, 0)), small_spec, prev_spec]
        args += [sel, e, z, prev]
    else:
        (prev,) = extra
        in_specs += [small_spec, prev_spec]
        args += [z, prev]
    return pl.pallas_call(
        functools.partial(_pattn_kernel, mode=mode, tq=tq),
        out_shape=jax.ShapeDtypeStruct((T, G * N_REP * HD), F32),
        grid=(G, T // tq),
        in_specs=in_specs,
        out_specs=pl.BlockSpec((tq, N_REP * HD), lambda g, i: (i, g)),
        compiler_params=_cparams("parallel", "parallel"),
        name="attn_" + mode,
    )(*args)


def _compress_kernel(x_ref, w1_ref, pe_ref, w2_ref, gain_ref, o_ref):
    c = pl.program_id(0) % 2
    nch = x_ref.shape[0]
    w1 = w1_ref[0]
    hid0 = jnp.zeros((nch, HD), F32)
    hid1 = jnp.zeros((nch, HD), F32)
    for s in range(CMP_STRIDE):
        xs = x_ref[:, s, :].astype(MXU_DTYPE)
        hid0 = hid0 + _dot(xs, w1[0, s].astype(MXU_DTYPE))
        hid1 = hid1 + _dot(xs, w1[1, s].astype(MXU_DTYPE))
    pe_term = _dot(pe_ref[0].astype(MXU_DTYPE), w1.reshape(CMP_LEN * HD, HD).astype(MXU_DTYPE))
    hid = pe_term + hid0 + pltpu.roll(hid1, nch - 1, 0)
    out = _dot((hid * jax.nn.sigmoid(hid)).astype(MXU_DTYPE), w2_ref[0].astype(MXU_DTYPE))
    ms = jnp.mean(out * out, axis=-1, keepdims=True)
    normed = (out * lax.rsqrt(ms + EPS)) * gain_ref[...]
    o_ref[0] = jnp.where(c == 0, normed, out)


def compress_blocks(z3, col, cmp_pe, cmp_w1, cmp_w2, gain_c):
    nch = z3.shape[0]
    r = CMP_LEN // CMP_STRIDE
    w1 = cmp_w1.reshape(2, r, CMP_STRIDE, HD, HD)
    pe = cmp_pe.reshape(CMP_LEN, 2, HD).transpose(1, 0, 2).reshape(2, 1, CMP_LEN * HD)
    cb = col // HD
    return pl.pallas_call(
        _compress_kernel,
        out_shape=jax.ShapeDtypeStruct((G_B * 2, nch, HD), F32),
        grid=(G_B * 2,),
        in_specs=[pl.BlockSpec((nch, CMP_STRIDE, HD), lambda gc: (0, 0, cb + gc)),
                  pl.BlockSpec((1, r, CMP_STRIDE, HD, HD), lambda gc: (gc % 2, 0, 0, 0, 0)),
                  pl.BlockSpec((1, 1, CMP_LEN * HD), lambda gc: (gc % 2, 0, 0)),
                  pl.BlockSpec((1, HD, HD), lambda gc: (gc % 2, 0, 0)),
                  pl.BlockSpec((1, HD), lambda gc: (0, 0))],
        out_specs=pl.BlockSpec((1, nch, HD), lambda gc: (gc, 0, 0)),
        compiler_params=_cparams("parallel"),
        name="nsa_compress",
    )(z3, w1, pe, cmp_w2, gain_c.reshape(1, HD))


def _nsa_cmp_kernel(*refs, tq, n_sel):
    q_refs = refs[0:N_REP]
    kc_ref, vc_ref, map_ref, small_ref, oc_ref, sel_ref = refs[N_REP:]
    g = pl.program_id(0)
    i = pl.program_id(1)
    kc = kc_ref[0].astype(MXU_DTYPE)
    vc = vc_ref[0].astype(MXU_DTYPE)
    smap = map_ref[...]
    nch = kc.shape[0]
    nsb = smap.shape[1]
    t = i * tq + lax.broadcasted_iota(jnp.int32, (tq, nch), 0)
    n = lax.broadcasted_iota(jnp.int32, (tq, nch), 1)
    valid = (n * CMP_STRIDE + CMP_LEN - 1) <= t
    gates = jax.nn.sigmoid(small_ref[...])
    lane = lax.broadcasted_iota(jnp.int32, gates.shape, 1)
    imp = jnp.zeros((tq, nsb), F32)
    for r in range(N_REP):
        q = q_refs[r][...].astype(MXU_DTYPE)
        lm = jnp.where(valid, _dot_nt(q, kc) * (HD ** -0.5), NEG)
        m = jnp.max(lm, axis=-1, keepdims=True)
        p = jnp.where(valid, jnp.exp(lm - m), 0.0)
        den = jnp.sum(p, axis=-1, keepdims=True)
        pc = (p / jnp.where(den > 0.0, den, 1.0)).astype(MXU_DTYPE)
        imp = imp + _dot(pc, smap)
        gcol = OFF_GATE + (g * N_REP + r) * 3
        gate = jnp.sum(jnp.where(lane == gcol, gates, 0.0), axis=-1, keepdims=True)
        oc_ref[:, r * HD:(r + 1) * HD] = _dot(pc, vc) * gate

    tpos = i * tq + lax.broadcasted_iota(jnp.int32, (tq, nsb), 0)
    blk = lax.broadcasted_iota(jnp.int32, (tq, nsb), 1)
    cur = tpos // SEL_LEN
    start_ok = blk <= cur
    forced_or_imp = jnp.where(blk == 0, BIG, jnp.where(blk == cur, BIG, jnp.where(blk == cur - 1, BIG, imp)))
    score = jnp.where(start_ok, forced_or_imp, NEG)
    blk_f = blk.astype(F32)
    sel = jnp.zeros((tq, nsb), F32)
    for _ in range(n_sel):
        best = jnp.max(score, axis=-1, keepdims=True)
        first = jnp.min(jnp.where(score == best, blk_f, float(nsb)), axis=-1, keepdims=True)
        hit = blk_f == first
        sel = jnp.where(hit, 1.0, sel)
        score = jnp.where(hit, -jnp.inf, score)
    sel_ref[...] = sel


def nsa_cmp_attention(z, q_col, kcv, sel_map, tq):
    T = z.shape[0]
    nch = kcv.shape[1]
    nsb = sel_map.shape[1]
    qb = q_col // HD
    q_specs = [pl.BlockSpec((tq, HD), functools.partial(lambda g, i, r: (i, qb + g * N_REP + r), r=r)) for r in range(N_REP)]
    return pl.pallas_call(
        functools.partial(_nsa_cmp_kernel, tq=tq, n_sel=min(N_SEL, nsb)),
        out_shape=(jax.ShapeDtypeStruct((T, H_B * HD), F32), jax.ShapeDtypeStruct((T, G_B * nsb), F32)),
        grid=(G_B, T // tq),
        in_specs=q_specs + [pl.BlockSpec((1, nch, HD), lambda g, i: (2 * g, 0, 0)),
                            pl.BlockSpec((1, nch, HD), lambda g, i: (2 * g + 1, 0, 0)),
                            pl.BlockSpec((nch, nsb), lambda g, i: (0, 0)),
                            pl.BlockSpec((tq, SMALL_W), lambda g, i: (i, COL_SMALL // SMALL_W))],
        out_specs=(pl.BlockSpec((tq, N_REP * HD), lambda g, i: (i, g)), pl.BlockSpec((tq, nsb), lambda g, i: (i, g))),
        compiler_params=_cparams("parallel", "parallel"),
        name="nsa_cmp",
    )(*([z] * N_REP), kcv, kcv, sel_map, z)


def _mem_attn_kernel(q_ref, k_ref, v_ref, o_ref):
    q = q_ref[0].astype(MXU_DTYPE)
    s = _dot_nt(q, k_ref[0].astype(MXU_DTYPE)) * (HD_M ** -0.5)
    p = jnp.exp(s - jnp.max(s, axis=-1, keepdims=True))
    p = p / jnp.sum(p, axis=-1, keepdims=True)
    o_ref[0] = _dot(p.astype(MXU_DTYPE), v_ref[0].astype(MXU_DTYPE))


def mem_attention(z3, q_col, mkv3, tq):
    B, Tq, _ = z3.shape
    n_mem = mkv3.shape[1]
    qb = q_col // HD_M
    return pl.pallas_call(
        _mem_attn_kernel,
        out_shape=jax.ShapeDtypeStruct((B, Tq, H_M * HD_M), F32),
        grid=(B, H_M, Tq // tq),
        in_specs=[pl.BlockSpec((1, tq, HD_M), lambda b, h, i: (b, i, qb + h)),
                  pl.BlockSpec((1, n_mem, HD_M), lambda b, h, i: (b, 0, 2 * h)),
                  pl.BlockSpec((1, n_mem, HD_M), lambda b, h, i: (b, 0, 2 * h + 1))],
        out_specs=pl.BlockSpec((1, tq, HD_M), lambda b, h, i: (b, i, h)),
        compiler_params=_cparams("parallel", "parallel", "arbitrary"),
        name="mem_attn",
    )(z3, mkv3, mkv3)


TOK_PAD = 8
NEW_PAD = 128


def _start_pages(pt_ref, b, n_pages, copies):
    def issue(p, _):
        for cp in copies(p, pt_ref[b, p]):
            cp.start()
        return 0

    lax.fori_loop(0, n_pages, issue, 0)


def _wait_pages(pt_ref, b, n_pages, copies):
    def wait(p, _):
        for cp in copies(p, pt_ref[b, p]):
            cp.wait()
        return 0

    lax.fori_loop(0, n_pages, wait, 0)


def _prefetched_pages(pt_ref, n_pages, copies):
    b, g = pl.program_id(0), pl.program_id(1)
    n_groups = pl.num_programs(1)
    n = b * n_groups + g
    slot = n % 2

    @pl.when(n == 0)
    def _():
        _start_pages(pt_ref, b, n_pages, functools.partial(copies, 0, g))

    @pl.when(n + 1 < pl.num_programs(0) * n_groups)
    def _():
        _start_pages(pt_ref, (n + 1) // n_groups, n_pages, functools.partial(copies, 1 - slot, (n + 1) % n_groups))

    _wait_pages(pt_ref, b, n_pages, functools.partial(copies, slot, g))
    return slot


def _page_rows(pool_ref, page, row):
    return pool_ref.at[pl.ds(page * PAGE_SIZE, PAGE_SIZE), row, :]


def _sample_index_kernel(pt_ref, q_ref, w_ref, knew_ref, pool_ref, mp_ref, mn_ref, kbuf, sc_ref, sem, *, topk, n_pages, past_len, chunk):
    b = pl.program_id(0)
    copies = lambda p, page: [pltpu.make_async_copy(pool_ref.at[page], kbuf.at[pl.ds(p * PAGE_SIZE, PAGE_SIZE)], sem)]
    _start_pages(pt_ref, b, n_pages, copies)
    _wait_pages(pt_ref, b, n_pages, copies)
    q = q_ref[0].astype(MXU_DTYPE)
    w = w_ref[0]

    def head_sum(keys):
        x = jnp.maximum(_dot_nt(q, keys.astype(MXU_DTYPE)), 0.0) * w
        return jnp.sum(x.reshape(TOK_PAD, H_I, x.shape[1]), axis=1)

    for c in range(past_len // chunk):
        sc_ref[:, c * chunk:(c + 1) * chunk] = head_sum(kbuf[c * chunk:(c + 1) * chunk, :])
    tok = lax.broadcasted_iota(jnp.int32, (TOK_PAD, NEW_PAD), 0)
    new = lax.broadcasted_iota(jnp.int32, (TOK_PAD, NEW_PAD), 1)
    sc_ref[:, past_len:past_len + NEW_PAD] = jnp.where(new <= tok, head_sum(knew_ref[0]), NEG)
    n_lane_tiles = (past_len + NEW_PAD) // LANE

    def bit_body(i, res_u):
        cand_u = res_u | jnp.left_shift(jnp.int32(1), 31 - i)
        cand = jnp.broadcast_to(_key_to_float(cand_u ^ jnp.int32(INT_MIN)), (TOK_PAD, LANE))
        part = jnp.zeros((TOK_PAD, LANE), F32)
        for t in range(n_lane_tiles):
            part = part + jnp.where(sc_ref[:, t * LANE:(t + 1) * LANE] >= cand, 1.0, 0.0)
        cnt = jnp.sum(part, axis=-1, keepdims=True)
        return jnp.where(cnt >= topk, cand_u, res_u)

    res_u = lax.fori_loop(0, 32, bit_body, jnp.zeros((TOK_PAD, 1), jnp.int32))
    tau = jnp.where(res_u == 0, -jnp.inf, _key_to_float(res_u ^ jnp.int32(INT_MIN)))
    mp_ref[0, 0] = jnp.where(sc_ref[:, 0:past_len] >= tau, 1.0, 0.0)
    mn_ref[0, 0] = jnp.where(new <= tok, jnp.where(sc_ref[:, past_len:past_len + NEW_PAD] >= tau, 1.0, 0.0), 0.0)


def sample_index(page_table, q, w, k_new, pool, topk):
    DB, n_pages = page_table.shape
    past_len = n_pages * PAGE_SIZE
    chunk = min(1024, past_len)
    return pl.pallas_call(
        functools.partial(_sample_index_kernel, topk=topk, n_pages=n_pages, past_len=past_len, chunk=chunk),
        out_shape=(jax.ShapeDtypeStruct((DB, 1, TOK_PAD, past_len), F32), jax.ShapeDtypeStruct((DB, 1, TOK_PAD, NEW_PAD), F32)),
        grid_spec=pltpu.PrefetchScalarGridSpec(
            num_scalar_prefetch=1,
            grid=(DB,),
            in_specs=[pl.BlockSpec((1, TOK_PAD * H_I, D_I), lambda b, pt: (b, 0, 0)),
                      pl.BlockSpec((1, TOK_PAD * H_I, 1), lambda b, pt: (b, 0, 0)),
                      pl.BlockSpec((1, NEW_PAD, D_I), lambda b, pt: (b, 0, 0)),
                      pl.BlockSpec(memory_space=pl.ANY)],
            out_specs=(pl.BlockSpec((1, 1, TOK_PAD, past_len), lambda b, pt: (b, 0, 0, 0)),
                       pl.BlockSpec((1, 1, TOK_PAD, NEW_PAD), lambda b, pt: (b, 0, 0, 0))),
            scratch_shapes=[pltpu.VMEM((past_len, D_I), F32), pltpu.VMEM((TOK_PAD, past_len + NEW_PAD), F32), pltpu.SemaphoreType.DMA(())],
        ),
        compiler_params=_cparams("arbitrary"),
        name="sample_index",
    )(page_table, q, w, k_new, pool)


def _sample_attend_kernel(pt_ref, q_ref, kvn_ref, mp_ref, mn_ref, tp_ref, tn_ref, *rest, gated, n_pages, past_len, chunk):
    if gated:
        gate_ref, prev_ref, pool_ref, o_ref, kbuf, vbuf, sem = rest
    else:
        pool_ref, o_ref, kbuf, vbuf, sem = rest
    def copies(slot, g, p, page):
        dst = lambda buf: buf.at[slot, pl.ds(p * PAGE_SIZE, PAGE_SIZE)]
        return [pltpu.make_async_copy(_page_rows(pool_ref, page, 2 * g), dst(kbuf), sem.at[slot]),
                pltpu.make_async_copy(_page_rows(pool_ref, page, 2 * g + 1), dst(vbuf), sem.at[slot])]

    slot = _prefetched_pages(pt_ref, n_pages, copies)
    q = q_ref[0, 0].astype(MXU_DTYPE)

    def logits(keys, bias_ref, mask_ref, sl):
        s = _dot_nt(q, keys.astype(MXU_DTYPE)) * (HD ** -0.5)
        bias = jnp.concatenate([bias_ref[r, :, sl] for r in range(N_REP)], axis=0)
        keep = mask_ref[0, 0, :, sl] > 0.5
        return jnp.where(jnp.concatenate([keep] * N_REP, axis=0), s + bias, NEG)

    n_chunks = past_len // chunk
    s_past = [logits(kbuf[slot, c * chunk:(c + 1) * chunk, :], tp_ref, mp_ref, slice(c * chunk, (c + 1) * chunk)) for c in range(n_chunks)]
    s_new = logits(kvn_ref[0, 0, 0], tn_ref, mn_ref, slice(0, NEW_PAD))
    m = jnp.max(s_new, axis=-1, keepdims=True)
    for s in s_past:
        m = jnp.maximum(m, jnp.max(s, axis=-1, keepdims=True))
    p_new = jnp.exp(s_new - m)
    l = jnp.sum(p_new, axis=-1, keepdims=True)
    acc = _dot(p_new.astype(MXU_DTYPE), kvn_ref[0, 0, 1].astype(MXU_DTYPE))
    for c, s in enumerate(s_past):
        p = jnp.exp(s - m)
        l = l + jnp.sum(p, axis=-1, keepdims=True)
        acc = acc + _dot(p.astype(MXU_DTYPE), vbuf[slot, c * chunk:(c + 1) * chunk, :].astype(MXU_DTYPE))
    o = acc / l
    if gated:
        o = prev_ref[0, 0] + o * jax.nn.sigmoid(gate_ref[0, 0])
    o_ref[0, 0] = o


def sample_attend(page_table, q, kv_new, mask_past, mask_new, tab_past, tab_new, pool, gate_prev=None):
    DB, n_pages = page_table.shape
    G = q.shape[1]
    past_len = n_pages * PAGE_SIZE
    chunk = min(1024, past_len)
    rows = N_REP * TOK_PAD
    mb, mg = mask_past.shape[0] > 1, mask_past.shape[1] > 1
    mask_map = lambda b, g, pt: (b if mb else 0, g if mg else 0, 0, 0)
    row_spec = lambda n: pl.BlockSpec((1, 1, rows, n), lambda b, g, pt: (b, g, 0, 0))
    in_specs = [row_spec(HD),
                pl.BlockSpec((1, 1, 2, NEW_PAD, HD), lambda b, g, pt: (b, g, 0, 0, 0)),
                pl.BlockSpec((1, 1, TOK_PAD, past_len), mask_map),
                pl.BlockSpec((1, 1, TOK_PAD, NEW_PAD), mask_map),
                pl.BlockSpec((N_REP, TOK_PAD, past_len), lambda b, g, pt: (g, 0, 0)),
                pl.BlockSpec((N_REP, TOK_PAD, NEW_PAD), lambda b, g, pt: (g, 0, 0))]
    args = [q, kv_new, mask_past, mask_new, tab_past, tab_new]
    if gate_prev is not None:
        in_specs += [row_spec(1), row_spec(HD)]
        args += list(gate_prev)
    in_specs.append(pl.BlockSpec(memory_space=pl.ANY))
    args.append(pool)
    return pl.pallas_call(
        functools.partial(_sample_attend_kernel, gated=gate_prev is not None, n_pages=n_pages, past_len=past_len, chunk=chunk),
        out_shape=jax.ShapeDtypeStruct((DB, G, rows, HD), F32),
        grid_spec=pltpu.PrefetchScalarGridSpec(
            num_scalar_prefetch=1,
            grid=(DB, G),
            in_specs=in_specs,
            out_specs=row_spec(HD),
            scratch_shapes=[pltpu.VMEM((2, past_len, HD), F32), pltpu.VMEM((2, past_len, HD), F32), pltpu.SemaphoreType.DMA((2,))],
        ),
        compiler_params=_cparams("arbitrary", "arbitrary"),
        name="sample_attend",
    )(page_table, *args)


def _sample_compress_kernel(pt_ref, cnew_ref, w1_ref, pe_ref, w2_ref, gain_ref, pool_ref, o_ref, cbuf, sem, *, n_pages, n_blocks):
    def copies(slot, g, p, page):
        return [pltpu.make_async_copy(_page_rows(pool_ref, page, 2 * g + c), cbuf.at[slot, c, pl.ds(p * PAGE_SIZE, PAGE_SIZE)], sem.at[slot])
                for c in range(2)]

    slot = _prefetched_pages(pt_ref, n_pages, copies)
    n_cached = n_pages * PAGE_SIZE
    n_tok = cbuf.shape[2]
    for c in range(2):
        cbuf[slot, c, n_cached:n_cached + CMP_STRIDE] = cnew_ref[0, 0, c]
        cbuf[slot, c, n_cached + CMP_STRIDE:n_tok] = jnp.zeros((n_tok - n_cached - CMP_STRIDE, HD), F32)

    nch = n_tok // CMP_STRIDE
    for c in range(2):
        acc = jnp.zeros((nch + 8, 2 * HD), F32)
        for sp in range(CMP_STRIDE // 2):
            rows = [cbuf.at[slot, c][pl.ds(2 * sp + d, nch, stride=CMP_STRIDE), :] for d in range(2)]
            lhs = jnp.concatenate([jnp.concatenate(rows, axis=1), pe_ref[c, sp]], axis=0).astype(MXU_DTYPE)
            acc = acc + _dot(lhs, w1_ref[c, sp])
        pe_term = acc[nch:nch + 1, 0:HD] + acc[nch + 1:nch + 2, HD:2 * HD]
        hid = pe_term + acc[0:nch, 0:HD] + pltpu.roll(acc[0:nch, HD:2 * HD], nch - 1, 0)
        out = _dot((hid * jax.nn.sigmoid(hid)).astype(MXU_DTYPE), w2_ref[c])[0:n_blocks]
        if c == 0:
            ms = jnp.mean(out * out, axis=-1, keepdims=True)
            out = (out * lax.rsqrt(ms + EPS)) * gain_ref[...]
        o_ref[0, c] = out


def sample_compress(page_table, c_new, pool, cmp_pe, cmp_w1, cmp_w2, gain_c):
    DB, n_pages = page_table.shape
    n_blocks = n_pages * (PAGE_SIZE // CMP_STRIDE)
    n_slots = _round_up(n_blocks + 1, 8)
    r = CMP_LEN // CMP_STRIDE
    assert r == 2
    half = CMP_STRIDE // 2
    w1 = cmp_w1.reshape(2, r, half, 2, HD, HD).transpose(0, 2, 3, 4, 1, 5).reshape(2, half, 2 * HD, r * HD).astype(MXU_DTYPE)
    pe = cmp_pe.reshape(r, half, 2, 2, HD).transpose(3, 1, 0, 2, 4).reshape(2, half, r, 2 * HD)
    pe = jnp.pad(pe, ((0, 0), (0, 0), (0, 8 - r), (0, 0)))
    full = lambda a: pl.BlockSpec(a.shape, lambda b, g, pt: (0,) * a.ndim)
    w2 = cmp_w2.astype(MXU_DTYPE)
    gain = gain_c.reshape(1, HD)
    return pl.pallas_call(
        functools.partial(_sample_compress_kernel, n_pages=n_pages, n_blocks=n_blocks),
        out_shape=jax.ShapeDtypeStruct((DB, G_B * 2, n_blocks, HD), F32),
        grid_spec=pltpu.PrefetchScalarGridSpec(
            num_scalar_prefetch=1,
            grid=(DB, G_B),
            in_specs=[pl.BlockSpec((1, 1, 2, CMP_STRIDE, HD), lambda b, g, pt: (b, g, 0, 0, 0)),
                      full(w1), full(pe), full(w2), full(gain), pl.BlockSpec(memory_space=pl.ANY)],
            out_specs=pl.BlockSpec((1, 2, n_blocks, HD), lambda b, g, pt: (b, g, 0, 0)),
            scratch_shapes=[pltpu.VMEM((2, 2, n_slots * CMP_STRIDE, HD), F32), pltpu.SemaphoreType.DMA((2,))],
        ),
        compiler_params=_cparams("arbitrary", "arbitrary"),
        name="sample_compress",
    )(page_table, c_new, w1, pe, w2, gain, pool)


def _sample_cmp_kernel(q_ref, kc_ref, vc_ref, map_ref, e_ref, gate_ref, oc_ref, mp_ref, mn_ref, *, past_len, n_real, n_sel):
    q = q_ref[0, 0].astype(MXU_DTYPE)
    kc = kc_ref[0, 0].astype(MXU_DTYPE)
    vc = vc_ref[0, 0].astype(MXU_DTYPE)
    rows = q.shape[0]
    nch = kc.shape[0]
    t = past_len + (lax.broadcasted_iota(jnp.int32, (rows, nch), 0) & (TOK_PAD - 1))
    n = lax.broadcasted_iota(jnp.int32, (rows, nch), 1)
    valid = (n * CMP_STRIDE + CMP_LEN - 1) <= t
    lm = jnp.where(valid, _dot_nt(q, kc) * (HD ** -0.5), NEG)
    m = jnp.max(lm, axis=-1, keepdims=True)
    p = jnp.where(valid, jnp.exp(lm - m), 0.0)
    den = jnp.sum(p, axis=-1, keepdims=True)
    pc = (p / jnp.where(den > 0.0, den, 1.0)).astype(MXU_DTYPE)
    oc_ref[0, 0] = _dot(pc, vc) * jax.nn.sigmoid(gate_ref[0, 0])
    imp_rows = _dot(pc, map_ref[...])
    imp = imp_rows[0:TOK_PAD]
    for r in range(1, N_REP):
        imp = imp + imp_rows[r * TOK_PAD:(r + 1) * TOK_PAD]

    nsb = imp.shape[1]
    tpos = past_len + lax.broadcasted_iota(jnp.int32, (TOK_PAD, nsb), 0)
    blk = lax.broadcasted_iota(jnp.int32, (TOK_PAD, nsb), 1)
    cur = tpos // SEL_LEN
    forced_or_imp = jnp.where(blk == 0, BIG, jnp.where(blk == cur, BIG, jnp.where(blk == cur - 1, BIG, imp)))
    score = jnp.where(blk <= cur, forced_or_imp, jnp.where(blk < n_real, NEG, -jnp.inf))
    blk_f = blk.astype(F32)
    sel = jnp.zeros((TOK_PAD, nsb), F32)
    for _ in range(n_sel):
        best = jnp.max(score, axis=-1, keepdims=True)
        first = jnp.min(jnp.where(score == best, blk_f, float(nsb)), axis=-1, keepdims=True)
        hit = blk_f == first
        sel = jnp.where(hit, 1.0, sel)
        score = jnp.where(hit, -jnp.inf, score)
    keep = _dot(sel.astype(MXU_DTYPE), e_ref[...])
    mp_ref[0, 0] = jnp.where(keep[:, 0:past_len] > 0.5, 1.0, 0.0)
    tok = lax.broadcasted_iota(jnp.int32, (TOK_PAD, NEW_PAD), 0)
    new = lax.broadcasted_iota(jnp.int32, (TOK_PAD, NEW_PAD), 1)
    mn_ref[0, 0] = jnp.where(new <= tok, jnp.where(keep[:, past_len:] > 0.5, 1.0, 0.0), 0.0)


def sample_cmp_attention(q, kcv, gate, past_len, n_tokens):
    DB, G, rows, _ = q.shape
    nc = kcv.shape[2]
    n_real = -(-(past_len + n_tokens) // SEL_LEN)
    nsb = _round_up(n_real, LANE)
    sel_map = cmp_to_sel(nc, nsb).astype(MXU_DTYPE)
    key = jnp.arange(past_len + NEW_PAD)[None, :]
    expand = ((key // SEL_LEN == jnp.arange(nsb)[:, None]) & (key < past_len + n_tokens)).astype(MXU_DTYPE)
    row_spec = lambda n: pl.BlockSpec((1, 1, rows, n), lambda b, g: (b, g, 0, 0))
    return pl.pallas_call(
        functools.partial(_sample_cmp_kernel, past_len=past_len, n_real=n_real, n_sel=min(N_SEL, n_real)),
        out_shape=(jax.ShapeDtypeStruct((DB, G, rows, HD), F32), jax.ShapeDtypeStruct((DB, G, TOK_PAD, past_len), F32),
                   jax.ShapeDtypeStruct((DB, G, TOK_PAD, NEW_PAD), F32)),
        grid=(DB, G),
        in_specs=[row_spec(HD),
                  pl.BlockSpec((1, 1, nc, HD), lambda b, g: (b, 2 * g, 0, 0)),
                  pl.BlockSpec((1, 1, nc, HD), lambda b, g: (b, 2 * g + 1, 0, 0)),
                  pl.BlockSpec(sel_map.shape, lambda b, g: (0, 0)),
                  pl.BlockSpec(expand.shape, lambda b, g: (0, 0)),
                  row_spec(1)],
        out_specs=(row_spec(HD), pl.BlockSpec((1, 1, TOK_PAD, past_len), lambda b, g: (b, g, 0, 0)),
                   pl.BlockSpec((1, 1, TOK_PAD, NEW_PAD), lambda b, g: (b, g, 0, 0))),
        compiler_params=_cparams("parallel", "parallel"),
        name="sample_cmp",
    )(q, kcv, kcv, sel_map, expand, gate)


def t5_bucket(dist):
    max_exact = N_BUCKETS // 2
    n = jnp.maximum(dist, 0)
    nf = jnp.maximum(n, 1).astype(jnp.float32)
    large = max_exact + (jnp.log(nf / max_exact) / math.log(MAX_DIST / max_exact) * (N_BUCKETS - max_exact)).astype(jnp.int32)
    return jnp.where(n < max_exact, n, jnp.minimum(large, N_BUCKETS - 1))


def cmp_to_sel(n_cmp, n_sel_blocks):
    cs = jnp.arange(n_cmp)[:, None] * CMP_STRIDE
    ss = jnp.arange(n_sel_blocks)[None, :] * SEL_LEN
    return ((cs < ss + SEL_LEN) & (cs + CMP_LEN > ss)).astype(jnp.float32)


def toeplitz_bias_tiles(bias_tab, tq):
    max_exact = N_BUCKETS // 2
    nearest_far = np.float32(tq + 1)
    assert max_exact + int(np.log(nearest_far / max_exact) / math.log(MAX_DIST / max_exact) * (N_BUCKETS - max_exact)) >= N_BUCKETS - 1
    rel = (bias_tab - bias_tab[N_BUCKETS - 1]) * LOG2E
    ii = jnp.arange(tq)[:, None]
    jj = jnp.arange(tq)[None, :]
    bucket = jnp.stack([t5_bucket(ii - jj), t5_bucket(tq + ii - jj)])
    onehot = (bucket[None] == jnp.arange(N_BUCKETS)[:, None, None, None]).astype(F32)
    tiles = jnp.einsum('bh,bctk->hctk', rel, onehot, precision=lax.Precision.HIGHEST)
    return tiles.reshape(G_A, N_REP, 2, tq, tq)


def block_expand_matrix(n_blocks, n_tiles, tk):
    b = jnp.arange(n_blocks)[None, :, None]
    key = (jnp.arange(n_tiles)[:, None, None] * tk + jnp.arange(tk)[None, None, :])
    return (key // SEL_LEN == b).astype(MXU_DTYPE)


def indexer_key_tiles(ki, tk):
    S = ki.shape[0]
    kt = ki.reshape(S // tk, tk, D_I).transpose(0, 2, 1).astype(MXU_DTYPE)
    zero = jnp.zeros_like(kt)
    return jnp.concatenate([jnp.concatenate([kt, zero], axis=2), jnp.concatenate([zero, kt], axis=2)], axis=1)


def _pad_cols(w, n):
    return jnp.pad(w, ((0, 0), (0, n - w.shape[1])))


def _seg(w, idx):
    return w[:, PROJ_OFFSETS[idx]:PROJ_OFFSETS[idx + 1]]


def prepare_weights(w_in, w_mem_kv, w_up_a, w_up_b, w_up_m, w_out, w_ffn_in, w_ffn_out):
    bf = MXU_DTYPE
    d_ff_pad = _round_up(D_FF, COL_TILE)
    order = (2, 0, 1, 5, 6, 7, 8, 10, 3, 4, 9)
    main = jnp.concatenate([_seg(w_in, idx).astype(bf) for idx in order] + [jnp.zeros((w_in.shape[0], N_MAIN - D_MAIN), bf)], axis=1)
    return dict(
        w_main=main,
        w_mg=_seg(w_in, 11).astype(bf),
        w_mem_kv=w_mem_kv.astype(bf),
        w_up_a=w_up_a.astype(bf), w_up_b=w_up_b.astype(bf), w_up_m=w_up_m.astype(bf),
        w_out=w_out.astype(bf),
        w_ffn_a=_pad_cols(w_ffn_in[:, :D_FF], d_ff_pad).astype(bf),
        w_ffn_u=_pad_cols(w_ffn_in[:, D_FF:], d_ff_pad).astype(bf),
        w_ffn_out=jnp.pad(w_ffn_out, ((0, d_ff_pad - D_FF), (0, 0))).astype(bf),
    )


def main_norm_vectors(qk_gain_a, qk_gain_b, qk_gain_m):
    one, zero = jnp.ones((HD,), F32), jnp.zeros((HD,), F32)

    def kv(gk):
        return jnp.tile(jnp.concatenate([gk, zero]), G_A), jnp.tile(jnp.concatenate([one, zero]), G_A)

    kva, fa = kv(qk_gain_a[1])
    slc, fs = kv(qk_gain_b[2])
    win, fw = kv(qk_gain_b[3])
    z = lambda n: jnp.zeros((n,), F32)
    gain = jnp.concatenate([z(H_I * D_I), jnp.tile(qk_gain_a[0], H_A), kva, jnp.tile(qk_gain_b[0], H_B), z(G_B * 2 * HD), slc, win,
                            jnp.tile(qk_gain_m[0], H_M), z(N_MAIN - COL_SMALL)])
    m128 = jnp.concatenate([z(H_I * D_I), jnp.ones((H_A * HD,), F32), fa, jnp.ones((H_B * HD,), F32), z(G_B * 2 * HD), fs, fw,
                            z(H_M * HD_M), z(N_MAIN - COL_SMALL)])
    m256 = jnp.concatenate([z(COL_QM), jnp.ones((H_M * HD_M,), F32), z(N_MAIN - COL_SMALL)])
    cols = np.arange(N_MAIN)
    has = ((cols >= COL_QA) & (cols < COL_CMP)) | ((cols >= COL_SLC) & (cols < COL_SMALL))
    tile_mode = jnp.asarray(has.reshape(-1, COL_TILE).any(axis=1).astype(np.int32))
    return gain.reshape(1, -1), m128.reshape(1, -1), m256.reshape(1, -1), tile_mode


def project_in(x2d, g_attn, W, norm_vecs, tm):
    h = rms_cast(x2d, g_attn, min(tm, 512))
    gain, m128, m256, tile_mode = norm_vecs
    z_main = matmul_headnorm(h, W['w_main'], gain, m128, m256, tile_mode, tm, COL_TILE, "in_proj_main")
    mg = matmul(h, W['w_mg'], tm, COL_TILE, name="in_proj_mg")
    return z_main, mg


def merge_ffn(x2d, o_a, o_b, o_m, mg, W, g_ffn, tm):
    mix = upmix(o_a, o_b, o_m, W['w_up_a'], W['w_up_b'], W['w_up_m'], mg, min(tm, 512), COL_TILE)
    x2 = matmul_residual(mix, W['w_out'], x2d, tm, COL_TILE)
    h2 = rms_cast(x2, g_ffn, min(tm, 512))
    act = swiglu_in(h2, W['w_ffn_a'], W['w_ffn_u'], tm, COL_TILE)
    tk = act.shape[1] // 4
    return matmul_residual_ksplit(act, W['w_ffn_out'], x2, tm, COL_TILE, tk)


def prompt_mixers(z, mkv, rel_bias, cmp_pe, cmp_w1, cmp_w2, gain_c, tq):
    T = z.shape[0]
    nk = T // tq
    bias_a, bias_b = rel_bias[:, :H_A], rel_bias[:, H_A:]
    near_a, near_b = toeplitz_bias_tiles(bias_a, tq), toeplitz_bias_tiles(bias_b, tq)
    ki = z[:, COL_SMALL + OFF_KI:COL_SMALL + OFF_KI + D_I]
    scores, tau = dsa_index(z, indexer_key_tiles(ki, tq), min(DSA_TOPK, T // 4), tq)
    o_a = prompt_attention('dsa', z, COL_QA, COL_KVA, near_a, tq, (scores, tau))
    nch = T // CMP_STRIDE
    nsb = T // SEL_LEN
    kcv = compress_blocks(z.reshape(nch, CMP_STRIDE, z.shape[1]), COL_CMP, cmp_pe, cmp_w1, cmp_w2, gain_c)
    sel_map = cmp_to_sel(nch, nsb).astype(MXU_DTYPE)
    oc, sel = nsa_cmp_attention(z, COL_QB, kcv, sel_map, tq)
    ocs = prompt_attention('slc', z, COL_QB, COL_SLC, near_b, tq, (sel, block_expand_matrix(nsb, nk, tq), oc))
    o_b = prompt_attention('win', z, COL_QB, COL_WIN, near_b, tq, (ocs,))
    o_m = mem_attention(z[None], COL_QM, mkv[None], tq)[0]
    return o_a, o_b, o_m


def sample_mixers(zs, page_table, pool_a_kv, pool_a_idx, pool_b_cmp, pool_b_slc, cache_win, cache_mem, rel_bias, cmp_pe, cmp_w1, cmp_w2, gain_c, DS):
    DB, n_pages = page_table.shape
    past_len = n_pages * PAGE_SIZE
    bias_a, bias_b = rel_bias[:, :H_A], rel_bias[:, H_A:]
    seg = lambda c, n: zs[:, c:c + n].reshape(DB, DS, n)
    pad_tok = lambda a: jnp.pad(a, ((0, 0), (0, TOK_PAD - DS)) + ((0, 0),) * (a.ndim - 2))
    rows = N_REP * TOK_PAD

    def q_rows(c):
        q = pad_tok(seg(c, H_A * HD)).reshape(DB, TOK_PAD, G_A, N_REP, HD)
        return q.transpose(0, 2, 3, 1, 4).reshape(DB, G_A, rows, HD)

    def kv_new(c):
        kv = seg(c, G_A * 2 * HD).reshape(DB, DS, G_A, 2, HD).transpose(0, 2, 3, 1, 4)
        return jnp.pad(kv, ((0, 0), (0, 0), (0, 0), (0, NEW_PAD - DS), (0, 0)))

    tok = jnp.arange(TOK_PAD)[:, None]
    new = jnp.arange(NEW_PAD)[None, :]

    def bias_tables(bias_tab, span):
        near = min(span, NEW_PAD)
        assert near == span or near + 1 >= MAX_DIST
        dist_near = near + tok - jnp.arange(near)[None, :]
        tab_near = bias_tab[t5_bucket(dist_near)].transpose(2, 0, 1)
        tab_far = jnp.broadcast_to(bias_tab[N_BUCKETS - 1][:, None, None], (bias_tab.shape[1], TOK_PAD, span - near))
        return jnp.concatenate([tab_far, tab_near], axis=2), bias_tab[t5_bucket(tok - new)].transpose(2, 0, 1)

    unrows = lambda o: o.reshape(DB, G_A, N_REP, TOK_PAD, HD)[:, :, :, :DS].transpose(0, 3, 1, 2, 4).reshape(DB * DS, H_A * HD)

    qi = pad_tok(seg(COL_QI, H_I * D_I)).reshape(DB, TOK_PAD * H_I, D_I)
    wi = pad_tok(seg(COL_SMALL + OFF_WI, H_I)).reshape(DB, TOK_PAD * H_I, 1)
    ki_new = jnp.pad(seg(COL_SMALL + OFF_KI, D_I), ((0, 0), (0, NEW_PAD - DS), (0, 0)))
    keep_p, keep_n = sample_index(page_table, qi, wi, ki_new, pool_a_idx, min(DSA_TOPK, (past_len + DS) // 4))
    tab_p, tab_n = bias_tables(bias_a, past_len)
    token_rows = lambda pool: pool.reshape(-1, G_A * 2, HD)
    o_a = sample_attend(page_table, q_rows(COL_QA), kv_new(COL_KVA), keep_p, keep_n, tab_p, tab_n, token_rows(pool_a_kv))

    c_new = seg(COL_CMP, G_B * 2 * HD).reshape(DB, DS, G_B, 2, HD).transpose(0, 2, 3, 1, 4)
    c_new = jnp.pad(c_new, ((0, 0), (0, 0), (0, 0), (0, CMP_STRIDE - DS), (0, 0)))
    kcv = sample_compress(page_table, c_new, token_rows(pool_b_cmp), cmp_pe, cmp_w1, cmp_w2, gain_c)
    gates = pad_tok(seg(COL_SMALL + OFF_GATE, 3 * H_B)).reshape(DB, TOK_PAD, G_B, N_REP, 3)
    gates = gates.transpose(4, 0, 2, 3, 1).reshape(3, DB, G_B, rows, 1)
    q_b = q_rows(COL_QB)
    oc, keep_p, keep_n = sample_cmp_attention(q_b, kcv, gates[0], past_len, DS)
    tab_p, tab_n = bias_tables(bias_b, past_len)
    ocs = sample_attend(page_table, q_b, kv_new(COL_SLC), keep_p, keep_n, tab_p, tab_n, token_rows(pool_b_slc), gate_prev=(gates[1], oc))
    w_eff = cache_win.shape[1]
    w_pages = w_eff // PAGE_SIZE
    win_pool = token_rows(cache_win)
    win_pt = jnp.arange(DB * w_pages, dtype=jnp.int32).reshape(DB, w_pages)
    keep_w = ((w_eff + tok - jnp.arange(w_eff)[None, :]) < WINDOW).astype(F32).reshape(1, 1, TOK_PAD, w_eff)
    keep_wn = ((new <= tok) & (new < DS)).astype(F32).reshape(1, 1, TOK_PAD, NEW_PAD)
    tab_p, tab_n = bias_tables(bias_b, w_eff)
    o_b = sample_attend(win_pt, q_b, kv_new(COL_WIN), keep_w, keep_wn, tab_p, tab_n, win_pool, gate_prev=(gates[2], ocs))

    n_mem = cache_mem.shape[1]
    o_m = mem_attention(zs.reshape(DB, DS, -1), COL_QM, cache_mem.reshape(DB, n_mem, H_M * 2 * HD_M), DS)
    return unrows(o_a), unrows(o_b), o_m.reshape(DB * DS, H_M * HD_M)


def kernel(x_prompt, x_sample, cache_a_kv, cache_a_idx, cache_b_cmp, cache_b_slc, cache_b_win, cache_mem_kv, page_table, mem_prompt, g_attn, w_in, qk_gain_a, qk_gain_b, qk_gain_m, cmp_pe, cmp_w1, cmp_w2, rel_bias, g_mem, w_mem_kv, w_up_a, w_up_b, w_up_m, w_out, g_ffn, w_ffn_in, w_ffn_out):
    l = 0
    B, T, D = x_prompt.shape
    assert B == 1
    DB, DS = x_sample.shape[:2]
    ki_cols = slice(COL_SMALL + OFF_KI, COL_SMALL + OFF_KI + D_I)
    W = prepare_weights(w_in[l], w_mem_kv[l], w_up_a[l], w_up_b[l], w_up_m[l], w_out[l], w_ffn_in[l], w_ffn_out[l])
    norm_vecs = main_norm_vectors(qk_gain_a[l], qk_gain_b[l], qk_gain_m[l])

    xp = x_prompt.reshape(T, D)
    z, mg = project_in(xp, g_attn[l], W, norm_vecs, 1024)
    n_mem = mem_prompt.shape[1]
    hm = rms_cast(mem_prompt.reshape(n_mem, D), g_mem[l], n_mem)
    kgain = jnp.tile(jnp.concatenate([qk_gain_m[l, 1], jnp.zeros((HD_M,), F32)]), H_M).reshape(1, -1)
    kflag = jnp.tile(jnp.concatenate([jnp.ones((HD_M,), F32), jnp.zeros((HD_M,), F32)]), H_M).reshape(1, -1)
    mkv = matmul_headnorm(hm, W['w_mem_kv'], kgain, jnp.zeros_like(kflag), kflag,
                          jnp.ones((kflag.shape[1] // COL_TILE,), jnp.int32), n_mem, COL_TILE, "mem_kv")
    o_a, o_b, o_m = prompt_mixers(z, mkv, rel_bias, cmp_pe[l], cmp_w1[l], cmp_w2[l], qk_gain_b[l, 1], ATTN_TILE)
    y_p = merge_ffn(xp, o_a, o_b, o_m, mg, W, g_ffn[l], 1024).reshape(B, T, D)
    kv5 = lambda c, rows: z[rows, c:c + G_A * 2 * HD].reshape(1, 1, -1, G_A, 2, HD)
    p_out = (kv5(COL_KVA, slice(None)), z[:, COL_SMALL + OFF_KI:COL_SMALL + OFF_KI + D_I].reshape(1, 1, T, D_I),
             kv5(COL_CMP, slice(None)), kv5(COL_SLC, slice(None)), kv5(COL_WIN, slice(T - min(WINDOW, T), T)),
             mkv.reshape(1, 1, n_mem, H_M, 2, HD_M))

    n_s = DB * DS
    xs = x_sample.reshape(n_s, D)
    zs, mg = project_in(xs, g_attn[l], W, norm_vecs, n_s)
    o_a, o_b, o_m = sample_mixers(zs, page_table, cache_a_kv[l], cache_a_idx[l], cache_b_cmp[l], cache_b_slc[l], cache_b_win[l],
                                  cache_mem_kv[l], rel_bias, cmp_pe[l], cmp_w1[l], cmp_w2[l], qk_gain_b[l, 1], DS)
    y_s = merge_ffn(xs, o_a, o_b, o_m, mg, W, g_ffn[l], n_s).reshape(DB, DS, D)
    new5 = lambda c: zs[:, c:c + G_A * 2 * HD].reshape(1, DB, DS, G_A, 2, HD)
    s_out = (new5(COL_KVA), zs[:, ki_cols].reshape(1, DB, DS, D_I), new5(COL_CMP), new5(COL_SLC),
             jnp.concatenate([cache_b_win[l], new5(COL_WIN)[0]], axis=1)[None, :, DS:])
    return (y_p, y_s) + p_out + s_out
```

```python
import functools
import math

import numpy as np
import jax
import jax.numpy as jnp
from jax import lax
from jax.experimental import pallas as pl
from jax.experimental.pallas import tpu as pltpu

D_MODEL = 4096
PAGE_SIZE = 128
HD = 128
H_A = 12
G_A = 4
H_I = 32
D_I = 64
DSA_TOPK = 256
H_B = 12
G_B = 4
CMP_LEN = 32
CMP_STRIDE = 16
SEL_LEN = 64
N_SEL = 16
WINDOW = 512
N_MEM = 256
H_M = 4
HD_M = 256
N_BUCKETS = 32
MAX_DIST = 128
D_FF = -(-8 * D_MODEL // (3 * 256)) * 256
Q_BLOCK = 128
EPS = 1e-6
NEG = -1e30
BIG = 1e30
PROJ_SIZES = (H_A * HD, G_A * 2 * HD, H_I * D_I, H_I, D_I, H_B * HD, G_B * 2 * HD, G_B * 2 * HD, G_B * 2 * HD, 3 * H_B, H_M * HD_M, 3 * D_MODEL)
PROJ_OFFSETS = tuple(int(o) for o in np.concatenate([[0], np.cumsum(PROJ_SIZES)]))
D_MAIN = PROJ_OFFSETS[-2]
N_REP = H_A // G_A

VMEM_LIMIT_BYTES = 56 * 1024 * 1024
LANE = 128
COL_TILE = 512
ATTN_TILE = 256
BISECT_ROWS = 128
FAR_WIDTH = 8
LOG2E = math.log2(math.e)
MXU_DTYPE = jnp.bfloat16
F32 = jnp.float32
INT_MIN = -2 ** 31

COL_QI = 0
COL_QA = COL_QI + H_I * D_I
COL_KVA = COL_QA + H_A * HD
COL_QB = COL_KVA + G_A * 2 * HD
COL_CMP = COL_QB + H_B * HD
COL_SLC = COL_CMP + G_B * 2 * HD
COL_WIN = COL_SLC + G_B * 2 * HD
COL_QM = COL_WIN + G_B * 2 * HD
COL_SMALL = COL_QM + H_M * HD_M
SMALL_W = 256
OFF_WI, OFF_KI, OFF_GATE = 0, H_I, H_I + D_I
N_MAIN = -(-(COL_SMALL + SMALL_W) // COL_TILE) * COL_TILE


def _round_up(n, m):
    return -(-n // m) * m


def _cparams(*sem):
    return pltpu.CompilerParams(dimension_semantics=sem, vmem_limit_bytes=VMEM_LIMIT_BYTES)


def _dot(a, b):
    return jnp.dot(a, b, preferred_element_type=F32)


def _dot_nt(a, b):
    return lax.dot_general(a, b, (((1,), (1,)), ((), ())), preferred_element_type=F32)


def _float_key(x):
    b = pltpu.bitcast(x, jnp.int32)
    return b ^ ((b >> 31) & jnp.int32(0x7FFFFFFF))


def _key_to_float(k):
    return pltpu.bitcast(k ^ ((k >> 31) & jnp.int32(0x7FFFFFFF)), F32)


def _np_float_key(v):
    b = int(np.array(v, np.float32).view(np.int32))
    return b ^ ((b >> 31) & 0x7FFFFFFF)


def _rms_cast_kernel(x_ref, g_ref, o_ref):
    x = x_ref[...]
    ms = jnp.mean(x * x, axis=-1, keepdims=True)
    o_ref[...] = ((x * lax.rsqrt(ms + EPS)) * g_ref[...]).astype(o_ref.dtype)


def rms_cast(x, g, tm):
    M, D = x.shape
    return pl.pallas_call(
        _rms_cast_kernel,
        out_shape=jax.ShapeDtypeStruct((M, D), MXU_DTYPE),
        grid=(M // tm,),
        in_specs=[pl.BlockSpec((tm, D), lambda i: (i, 0)), pl.BlockSpec((1, D), lambda i: (0, 0))],
        out_specs=pl.BlockSpec((tm, D), lambda i: (i, 0)),
        compiler_params=_cparams("parallel"),
        name="rms_cast",
    )(x, g.reshape(1, D))


def _mm_kernel(a_ref, w_ref, o_ref):
    o_ref[...] = _dot(a_ref[...], w_ref[...]).astype(o_ref.dtype)


def matmul(a, w, tm, tn, out_dtype=F32, name="matmul"):
    M, K = a.shape
    N = w.shape[1]
    return pl.pallas_call(
        _mm_kernel,
        out_shape=jax.ShapeDtypeStruct((M, N), out_dtype),
        grid=(M // tm, N // tn),
        in_specs=[pl.BlockSpec((tm, K), lambda i, j: (i, 0)), pl.BlockSpec((K, tn), lambda i, j: (0, j))],
        out_specs=pl.BlockSpec((tm, tn), lambda i, j: (i, j)),
        compiler_params=_cparams("parallel", "arbitrary"),
        name=name,
    )(a, w)


def _norm_cols(z, gain, m128, m256):
    tn = z.shape[1]
    sq = z * z
    ssq = [jnp.sum(sq[:, c * LANE:(c + 1) * LANE], axis=-1, keepdims=True) for c in range(tn // LANE)]
    cols = []
    for c in range(tn // LANE):
        sl = slice(c * LANE, (c + 1) * LANE)
        ms1 = ssq[c] * (1.0 / LANE)
        ms2 = (ssq[c - c % 2] + ssq[c - c % 2 + 1]) * (1.0 / (2 * LANE))
        ms = jnp.where(m256[:, sl] > 0, ms2, ms1)
        zc = z[:, sl]
        zn = (zc * lax.rsqrt(ms + EPS)) * gain[:, sl]
        cols.append(jnp.where((m128[:, sl] + m256[:, sl]) > 0, zn, zc))
    return jnp.concatenate(cols, axis=-1)


def _mm_norm_kernel(mode_ref, a_ref, w_ref, gain_ref, m128_ref, m256_ref, o_ref):
    j = pl.program_id(1)
    z = _dot(a_ref[...], w_ref[...])

    @pl.when(mode_ref[j] == 0)
    def _():
        o_ref[...] = z

    @pl.when(mode_ref[j] != 0)
    def _():
        o_ref[...] = _norm_cols(z, gain_ref[...], m128_ref[...], m256_ref[...])


def matmul_headnorm(a, w, gain, m128, m256, tile_mode, tm, tn, name):
    M, K = a.shape
    N = w.shape[1]
    vec = pl.BlockSpec((1, tn), lambda i, j, m: (0, j))
    return pl.pallas_call(
        _mm_norm_kernel,
        out_shape=jax.ShapeDtypeStruct((M, N), F32),
        grid_spec=pltpu.PrefetchScalarGridSpec(
            num_scalar_prefetch=1,
            grid=(M // tm, N // tn),
            in_specs=[pl.BlockSpec((tm, K), lambda i, j, m: (i, 0)), pl.BlockSpec((K, tn), lambda i, j, m: (0, j)), vec, vec, vec],
            out_specs=pl.BlockSpec((tm, tn), lambda i, j, m: (i, j)),
        ),
        compiler_params=_cparams("parallel", "arbitrary"),
        name=name,
    )(tile_mode, a, w, gain, m128, m256)


def _upmix_kernel(oa_ref, ob_ref, om_ref, wa_ref, wb_ref, wm_ref, g0_ref, g1_ref, g2_ref, o_ref):
    ya = _dot(oa_ref[...].astype(MXU_DTYPE), wa_ref[...])
    yb = _dot(ob_ref[...].astype(MXU_DTYPE), wb_ref[...])
    ym = _dot(om_ref[...].astype(MXU_DTYPE), wm_ref[...])
    mix = jax.nn.sigmoid(g0_ref[...]) * ya + jax.nn.sigmoid(g1_ref[...]) * yb + jax.nn.sigmoid(g2_ref[...]) * ym
    o_ref[...] = mix.astype(o_ref.dtype)


def upmix(o_a, o_b, o_m, w_a, w_b, w_m, mg, tm, tn):
    M = o_a.shape[0]
    D = w_a.shape[1]
    nj = D // tn
    a_spec = lambda k: pl.BlockSpec((tm, k), lambda i, j: (i, 0))
    w_spec = lambda k: pl.BlockSpec((k, tn), lambda i, j: (0, j))
    g_spec = lambda c: pl.BlockSpec((tm, tn), lambda i, j: (i, j + c * nj))
    return pl.pallas_call(
        _upmix_kernel,
        out_shape=jax.ShapeDtypeStruct((M, D), MXU_DTYPE),
        grid=(M // tm, nj),
        in_specs=[a_spec(o_a.shape[1]), a_spec(o_b.shape[1]), a_spec(o_m.shape[1]),
                  w_spec(w_a.shape[0]), w_spec(w_b.shape[0]), w_spec(w_m.shape[0]),
                  g_spec(0), g_spec(1), g_spec(2)],
        out_specs=pl.BlockSpec((tm, tn), lambda i, j: (i, j)),
        compiler_params=_cparams("parallel", "arbitrary"),
        name="upmix",
    )(o_a, o_b, o_m, w_a, w_b, w_m, mg, mg, mg)


def _mm_res_kernel(a_ref, w_ref, r_ref, o_ref):
    o_ref[...] = r_ref[...] + _dot(a_ref[...], w_ref[...])


def matmul_residual(a, w, r, tm, tn):
    M, K = a.shape
    N = w.shape[1]
    return pl.pallas_call(
        _mm_res_kernel,
        out_shape=jax.ShapeDtypeStruct((M, N), F32),
        grid=(M // tm, N // tn),
        in_specs=[pl.BlockSpec((tm, K), lambda i, j: (i, 0)), pl.BlockSpec((K, tn), lambda i, j: (0, j)),
                  pl.BlockSpec((tm, tn), lambda i, j: (i, j))],
        out_specs=pl.BlockSpec((tm, tn), lambda i, j: (i, j)),
        compiler_params=_cparams("parallel", "arbitrary"),
        name="matmul_residual",
    )(a, w, r)


def _swiglu_kernel(h_ref, wa_ref, wu_ref, o_ref):
    h = h_ref[...]
    a = _dot(h, wa_ref[...])
    u = _dot(h, wu_ref[...])
    o_ref[...] = (a * jax.nn.sigmoid(a) * u).astype(o_ref.dtype)


def swiglu_in(h, w_a, w_u, tm, tn):
    M, K = h.shape
    N = w_a.shape[1]
    return pl.pallas_call(
        _swiglu_kernel,
        out_shape=jax.ShapeDtypeStruct((M, N), MXU_DTYPE),
        grid=(M // tm, N // tn),
        in_specs=[pl.BlockSpec((tm, K), lambda i, j: (i, 0)), pl.BlockSpec((K, tn), lambda i, j: (0, j)),
                  pl.BlockSpec((K, tn), lambda i, j: (0, j))],
        out_specs=pl.BlockSpec((tm, tn), lambda i, j: (i, j)),
        compiler_params=_cparams("parallel", "arbitrary"),
        name="swiglu_in",
    )(h, w_a, w_u)


def _mm_res_acc_kernel(a_ref, w_ref, r_ref, o_ref, acc_ref):
    k = pl.program_id(2)

    @pl.when(k == 0)
    def _():
        acc_ref[...] = r_ref[...]

    acc_ref[...] += _dot(a_ref[...], w_ref[...])

    @pl.when(k == pl.num_programs(2) - 1)
    def _():
        o_ref[...] = acc_ref[...]


def matmul_residual_ksplit(a, w, r, tm, tn, tk):
    M, K = a.shape
    N = w.shape[1]
    return pl.pallas_call(
        _mm_res_acc_kernel,
        out_shape=jax.ShapeDtypeStruct((M, N), F32),
        grid=(M // tm, N // tn, K // tk),
        in_specs=[pl.BlockSpec((tm, tk), lambda i, j, k: (i, k)), pl.BlockSpec((tk, tn), lambda i, j, k: (k, j)),
                  pl.BlockSpec((tm, tn), lambda i, j, k: (i, j))],
        out_specs=pl.BlockSpec((tm, tn), lambda i, j, k: (i, j)),
        scratch_shapes=[pltpu.VMEM((tm, tn), F32)],
        compiler_params=_cparams("parallel", "arbitrary", "arbitrary"),
        name="matmul_residual_ksplit",
    )(a, w, r)


def _dsa_index_kernel(q_ref, small_ref, kd_ref, sc_ref, tau_ref, cmp_ref, *, topk, tq):
    i = pl.program_id(0)
    nk = sc_ref.shape[0]
    q = q_ref[...].astype(MXU_DTYPE)
    w = small_ref[:, OFF_WI:OFF_WI + H_I]
    row = lax.broadcasted_iota(jnp.int32, (tq, tq), 0)
    col = lax.broadcasted_iota(jnp.int32, (tq, tq), 1)

    def score_tile(j, _):
        kd = kd_ref[j]
        acc = jnp.zeros((tq, tq), F32)
        for p in range(H_I // 2):
            r = _dot(q[:, p * 2 * D_I:(p + 1) * 2 * D_I], kd)
            acc = acc + jnp.maximum(r[:, :tq], 0.0) * w[:, 2 * p:2 * p + 1] + jnp.maximum(r[:, tq:], 0.0) * w[:, 2 * p + 1:2 * p + 2]
        sc_ref[j] = acc
        causal = (j * tq + col) <= (i * tq + row)
        cmp_ref[j] = jnp.where(causal, acc, NEG)
        return 0

    lax.fori_loop(0, i + 1, score_tile, 0)

    def zero_tile(j, _):
        sc_ref[j] = jnp.zeros((tq, tq), F32)
        return 0

    lax.fori_loop(i + 1, nk, zero_tile, 0)

    for r0 in range(0, tq, BISECT_ROWS):
        strip = slice(r0, r0 + BISECT_ROWS)

        def bit_body(b, res_u):
            cand_u = res_u | jnp.left_shift(jnp.int32(1), 31 - b)
            cand = jnp.broadcast_to(_key_to_float(cand_u ^ jnp.int32(INT_MIN)), (BISECT_ROWS, LANE))

            def cnt_body(j, c):
                for t in range(tq // LANE):
                    c = c + jnp.where(cmp_ref[j, strip, t * LANE:(t + 1) * LANE] >= cand, 1.0, 0.0)
                return c

            cnt = lax.fori_loop(0, i + 1, cnt_body, jnp.zeros((BISECT_ROWS, LANE), F32))
            total = jnp.sum(cnt, axis=-1, keepdims=True)
            return jnp.where(total >= topk, cand_u, res_u)

        res_u = lax.fori_loop(0, 32, bit_body, jnp.zeros((BISECT_ROWS, 1), jnp.int32))
        tau_ref[strip, :] = jnp.where(res_u == 0, -jnp.inf, _key_to_float(res_u ^ jnp.int32(INT_MIN)))


def dsa_index(z, kd, topk, tq):
    T = z.shape[0]
    nk = kd.shape[0]
    return pl.pallas_call(
        functools.partial(_dsa_index_kernel, topk=topk, tq=tq),
        out_shape=(jax.ShapeDtypeStruct((nk, T, tq), F32), jax.ShapeDtypeStruct((T, 1), F32)),
        grid=(T // tq,),
        in_specs=[pl.BlockSpec((tq, H_I * D_I), lambda i: (i, COL_QI // (H_I * D_I))),
                  pl.BlockSpec((tq, SMALL_W), lambda i: (i, COL_SMALL // SMALL_W)),
                  pl.BlockSpec(kd.shape, lambda i: (0, 0, 0))],
        out_specs=(pl.BlockSpec((nk, tq, tq), lambda i: (0, i, 0)), pl.BlockSpec((tq, 1), lambda i: (i, 0))),
        scratch_shapes=[pltpu.VMEM((nk, tq, tq), F32)],
        compiler_params=_cparams("parallel"),
        name="dsa_index",
    )(z, z, kd)


def _flash_tile(q, k, v, bias, masks, state):
    m, l, acc = state
    s = _dot_nt(q, k)
    if bias is not None:
        s = s + bias
    for mask in masks:
        s = jnp.where(mask, s, NEG)
    m_new = jnp.maximum(m, jnp.max(s, axis=-1, keepdims=True))
    alpha = jnp.exp2(m - m_new)
    p = jnp.exp2(s - m_new)
    l = alpha * l + jnp.sum(p, axis=-1, keepdims=True)
    acc = alpha * acc + _dot(p.astype(MXU_DTYPE), v)
    return m_new, l, acc


def _pattn_kernel(*refs, mode, tq):
    q_refs = refs[0:N_REP]
    k_ref, v_ref, bias_ref = refs[N_REP:N_REP + 3]
    rest = refs[N_REP + 3:]
    if mode == 'dsa':
        sc_ref, tau_ref, o_ref = rest
    elif mode == 'slc':
        sel_ref, e_ref, small_ref, prev_ref, o_ref = rest
    else:
        small_ref, prev_ref, o_ref = rest
    g = pl.program_id(0)
    i = pl.program_id(1)
    q = [(r[...] * (HD ** -0.5 * LOG2E)).astype(MXU_DTYPE) for r in q_refs]
    row = lax.broadcasted_iota(jnp.int32, (tq, tq), 0)
    col = lax.broadcasted_iota(jnp.int32, (tq, tq), 1)
    if mode == 'dsa':
        tau = tau_ref[...]
    if mode == 'slc':
        sel = sel_ref[...].astype(MXU_DTYPE)

    def keep_mask(j):
        if mode == 'dsa':
            return sc_ref[j] >= tau
        return _dot(sel, e_ref[j]) > 0.5

    def process(j, state, kind, width=1):
        start = pl.multiple_of(j * tq, tq)
        kt = k_ref[pl.ds(start, width * tq), :].astype(MXU_DTYPE)
        vt = v_ref[pl.ds(start, width * tq), :].astype(MXU_DTYPE)
        masks = []
        if mode in ('dsa', 'slc'):
            parts = [keep_mask(j + w) for w in range(width)]
            masks.append(parts[0] if width == 1 else jnp.concatenate(parts, axis=1))
        elif kind == 'edge':
            masks.append(row < col)
        if kind == 'diag':
            masks.append(row >= col)
        if kind == 'near+diag':
            wide_row = lax.broadcasted_iota(jnp.int32, (tq, 2 * tq), 0)
            wide_col = lax.broadcasted_iota(jnp.int32, (tq, 2 * tq), 1)
            masks.append(wide_col <= wide_row + tq)
        out = []
        for r in range(N_REP):
            if kind in ('far', 'edge'):
                bias = None
            elif kind == 'near+diag':
                bias = jnp.concatenate([bias_ref[0, r, 1], bias_ref[0, r, 0]], axis=1)
            else:
                bias = bias_ref[0, r, 0 if kind == 'diag' else 1]
            out.append(_flash_tile(q[r], kt, vt, bias, masks, state[r]))
        return tuple(out)

    state = tuple((jnp.full((tq, 1), NEG, F32), jnp.zeros((tq, 1), F32), jnp.zeros((tq, HD), F32)) for _ in range(N_REP))
    if mode == 'win':
        n_win = WINDOW // tq
        state = lax.cond(i >= n_win, lambda s: process(i - n_win, s, 'edge'), lambda s: s, state)
        for d in range(n_win - 1, 1, -1):
            state = lax.cond(i >= d, functools.partial(lambda s, d: process(i - d, s, 'far'), d=d), lambda s: s, state)
    else:
        n_far = jnp.maximum(i - 1, 0)
        n_wide = n_far // FAR_WIDTH
        state = lax.fori_loop(0, n_wide, lambda jj, s: process(jj * FAR_WIDTH, s, 'far', FAR_WIDTH), state)
        done = n_wide * FAR_WIDTH
        width = FAR_WIDTH // 2
        while width >= 1:
            take = ((n_far - done) >= width).astype(jnp.int32)
            state = lax.cond(take == 1, functools.partial(lambda s, d, w: process(d, s, 'far', w), d=done, w=width), lambda s: s, state)
            done = done + take * width
            width //= 2
    if mode == 'win':
        state = lax.cond(i >= 1, lambda s: process(i - 1, s, 'near'), lambda s: s, state)
        state = process(i, state, 'diag')
    else:
        state = lax.cond(i >= 1, lambda s: process(i - 1, s, 'near+diag', 2), lambda s: process(i, s, 'diag'), state)

    if mode != 'dsa':
        branch = 1 if mode == 'slc' else 2
        gates = jax.nn.sigmoid(small_ref[...])
        lane = lax.broadcasted_iota(jnp.int32, gates.shape, 1)
    for r in range(N_REP):
        m, l, acc = state[r]
        o = acc / l
        if mode != 'dsa':
            gcol = OFF_GATE + (g * N_REP + r) * 3 + branch
            o = prev_ref[:, r * HD:(r + 1) * HD] + o * jnp.sum(jnp.where(lane == gcol, gates, 0.0), axis=-1, keepdims=True)
        o_ref[:, r * HD:(r + 1) * HD] = o


def prompt_attention(mode, z, q_col, kv_col, bias_near, tq, extra):
    T = z.shape[0]
    G = G_A
    qb = q_col // HD
    kb = kv_col // HD
    q_specs = [pl.BlockSpec((tq, HD), functools.partial(lambda g, i, r: (i, qb + g * N_REP + r), r=r)) for r in range(N_REP)]
    in_specs = q_specs + [
        pl.BlockSpec((T, HD), lambda g, i: (0, kb + 2 * g)),
        pl.BlockSpec((T, HD), lambda g, i: (0, kb + 2 * g + 1)),
        pl.BlockSpec((1, N_REP, 2, tq, tq), lambda g, i: (g, 0, 0, 0, 0)),
    ]
    args = [z] * N_REP + [z, z, bias_near]
    small_spec = pl.BlockSpec((tq, SMALL_W), lambda g, i: (i, COL_SMALL // SMALL_W))
    prev_spec = pl.BlockSpec((tq, N_REP * HD), lambda g, i: (i, g))
    if mode == 'dsa':
        scores, tau = extra
        nk = scores.shape[0]
        in_specs += [pl.BlockSpec((nk, tq, tq), lambda g, i: (0, i, 0)), pl.BlockSpec((tq, 1), lambda g, i: (i, 0))]
        args += [scores, tau]
    elif mode == 'slc':
        sel, e, prev = extra
        nsb = e.shape[1]
        in_specs += [pl.BlockSpec((tq, nsb), lambda g, i: (i, g)), pl.BlockSpec(e.shape, lambda g, i: (0, 0, 0)), small_spec, prev_spec]
        args += [sel, e, z, prev]
    else:
        (prev,) = extra
        in_specs += [small_spec, prev_spec]
        args += [z, prev]
    return pl.pallas_call(
        functools.partial(_pattn_kernel, mode=mode, tq=tq),
        out_shape=jax.ShapeDtypeStruct((T, G * N_REP * HD), F32),
        grid=(G, T // tq),
        in_specs=in_specs,
        out_specs=pl.BlockSpec((tq, N_REP * HD), lambda g, i: (i, g)),
        compiler_params=_cparams("parallel", "parallel"),
        name="attn_" + mode,
    )(*args)


def _compress_kernel(x_ref, w1_ref, pe_ref, w2_ref, gain_ref, o_ref):
    c = pl.program_id(0) % 2
    nch = x_ref.shape[0]
    w1 = w1_ref[0]
    hid0 = jnp.zeros((nch, HD), F32)
    hid1 = jnp.zeros((nch, HD), F32)
    for s in range(CMP_STRIDE):
        xs = x_ref[:, s, :].astype(MXU_DTYPE)
        hid0 = hid0 + _dot(xs, w1[0, s].astype(MXU_DTYPE))
        hid1 = hid1 + _dot(xs, w1[1, s].astype(MXU_DTYPE))
    pe_term = _dot(pe_ref[0].astype(MXU_DTYPE), w1.reshape(CMP_LEN * HD, HD).astype(MXU_DTYPE))
    hid = pe_term + hid0 + pltpu.roll(hid1, nch - 1, 0)
    out = _dot((hid * jax.nn.sigmoid(hid)).astype(MXU_DTYPE), w2_ref[0].astype(MXU_DTYPE))
    ms = jnp.mean(out * out, axis=-1, keepdims=True)
    normed = (out * lax.rsqrt(ms + EPS)) * gain_ref[...]
    o_ref[0] = jnp.where(c == 0, normed, out)


def compress_blocks(z3, col, cmp_pe, cmp_w1, cmp_w2, gain_c):
    nch = z3.shape[0]
    r = CMP_LEN // CMP_STRIDE
    w1 = cmp_w1.reshape(2, r, CMP_STRIDE, HD, HD)
    pe = cmp_pe.reshape(CMP_LEN, 2, HD).transpose(1, 0, 2).reshape(2, 1, CMP_LEN * HD)
    cb = col // HD
    return pl.pallas_call(
        _compress_kernel,
        out_shape=jax.ShapeDtypeStruct((G_B * 2, nch, HD), F32),
        grid=(G_B * 2,),
        in_specs=[pl.BlockSpec((nch, CMP_STRIDE, HD), lambda gc: (0, 0, cb + gc)),
                  pl.BlockSpec((1, r, CMP_STRIDE, HD, HD), lambda gc: (gc % 2, 0, 0, 0, 0)),
                  pl.BlockSpec((1, 1, CMP_LEN * HD), lambda gc: (gc % 2, 0, 0)),
                  pl.BlockSpec((1, HD, HD), lambda gc: (gc % 2, 0, 0)),
                  pl.BlockSpec((1, HD), lambda gc: (0, 0))],
        out_specs=pl.BlockSpec((1, nch, HD), lambda gc: (gc, 0, 0)),
        compiler_params=_cparams("parallel"),
        name="nsa_compress",
    )(z3, w1, pe, cmp_w2, gain_c.reshape(1, HD))


def _nsa_cmp_kernel(*refs, tq, n_sel):
    q_refs = refs[0:N_REP]
    kc_ref, vc_ref, map_ref, small_ref, oc_ref, sel_ref = refs[N_REP:]
    g = pl.program_id(0)
    i = pl.program_id(1)
    kc = kc_ref[0].astype(MXU_DTYPE)
    vc = vc_ref[0].astype(MXU_DTYPE)
    smap_t = map_ref[...]
    nch = kc.shape[0]
    nsb = smap_t.shape[0]
    t = i * tq + lax.broadcasted_iota(jnp.int32, (tq, nch), 0)
    n = lax.broadcasted_iota(jnp.int32, (tq, nch), 1)
    valid = (n * CMP_STRIDE + CMP_LEN - 1) <= t
    gates = jax.nn.sigmoid(small_ref[...])
    lane = lax.broadcasted_iota(jnp.int32, gates.shape, 1)
    imp = jnp.zeros((nsb, tq), F32)
    for r in range(N_REP):
        q = q_refs[r][...].astype(MXU_DTYPE)
        lm = jnp.where(valid, _dot_nt(q, kc) * (HD ** -0.5), NEG)
        m = jnp.max(lm, axis=-1, keepdims=True)
        p = jnp.where(valid, jnp.exp(lm - m), 0.0)
        den = jnp.sum(p, axis=-1, keepdims=True)
        pc = (p / jnp.where(den > 0.0, den, 1.0)).astype(MXU_DTYPE)
        imp = imp + _dot_nt(smap_t, pc)
        gcol = OFF_GATE + (g * N_REP + r) * 3
        gate = jnp.sum(jnp.where(lane == gcol, gates, 0.0), axis=-1, keepdims=True)
        oc_ref[:, r * HD:(r + 1) * HD] = _dot(pc, vc) * gate

    tpos = i * tq + lax.broadcasted_iota(jnp.int32, (nsb, tq), 1)
    blk = lax.broadcasted_iota(jnp.int32, (nsb, tq), 0)
    cur = tpos // SEL_LEN
    start_ok = blk <= cur
    forced_or_imp = jnp.where(blk == 0, BIG, jnp.where(blk == cur, BIG, jnp.where(blk == cur - 1, BIG, imp)))
    score = jnp.where(start_ok, forced_or_imp, NEG)
    blk_f = blk.astype(F32)
    sel = jnp.zeros((nsb, tq), F32)
    for _ in range(n_sel):
        best = jnp.max(score, axis=0, keepdims=True)
        first = jnp.min(jnp.where(score == best, blk_f, float(nsb)), axis=0, keepdims=True)
        hit = blk_f == first
        sel = jnp.where(hit, 1.0, sel)
        score = jnp.where(hit, -jnp.inf, score)
    sel_ref[...] = sel.T


def nsa_cmp_attention(z, q_col, kcv, sel_map, tq):
    T = z.shape[0]
    nch = kcv.shape[1]
    nsb = sel_map.shape[1]
    qb = q_col // HD
    q_specs = [pl.BlockSpec((tq, HD), functools.partial(lambda g, i, r: (i, qb + g * N_REP + r), r=r)) for r in range(N_REP)]
    return pl.pallas_call(
        functools.partial(_nsa_cmp_kernel, tq=tq, n_sel=min(N_SEL, nsb)),
        out_shape=(jax.ShapeDtypeStruct((T, H_B * HD), F32), jax.ShapeDtypeStruct((T, G_B * nsb), F32)),
        grid=(G_B, T // tq),
        in_specs=q_specs + [pl.BlockSpec((1, nch, HD), lambda g, i: (2 * g, 0, 0)),
                            pl.BlockSpec((1, nch, HD), lambda g, i: (2 * g + 1, 0, 0)),
                            pl.BlockSpec((nsb, nch), lambda g, i: (0, 0)),
                            pl.BlockSpec((tq, SMALL_W), lambda g, i: (i, COL_SMALL // SMALL_W))],
        out_specs=(pl.BlockSpec((tq, N_REP * HD), lambda g, i: (i, g)), pl.BlockSpec((tq, nsb), lambda g, i: (i, g))),
        compiler_params=_cparams("parallel", "parallel"),
        name="nsa_cmp",
    )(*([z] * N_REP), kcv, kcv, sel_map.T, z)


def _mem_attn_kernel(q_ref, k_ref, v_ref, o_ref):
    q = q_ref[0].astype(MXU_DTYPE)
    s = _dot_nt(q, k_ref[0].astype(MXU_DTYPE)) * (HD_M ** -0.5)
    p = jnp.exp(s - jnp.max(s, axis=-1, keepdims=True))
    p = p / jnp.sum(p, axis=-1, keepdims=True)
    o_ref[0] = _dot(p.astype(MXU_DTYPE), v_ref[0].astype(MXU_DTYPE))


def mem_attention(z3, q_col, mkv3, tq):
    B, Tq, _ = z3.shape
    n_mem = mkv3.shape[1]
    qb = q_col // HD_M
    return pl.pallas_call(
        _mem_attn_kernel,
        out_shape=jax.ShapeDtypeStruct((B, Tq, H_M * HD_M), F32),
        grid=(B, H_M, Tq // tq),
        in_specs=[pl.BlockSpec((1, tq, HD_M), lambda b, h, i: (b, i, qb + h)),
                  pl.BlockSpec((1, n_mem, HD_M), lambda b, h, i: (b, 0, 2 * h)),
                  pl.BlockSpec((1, n_mem, HD_M), lambda b, h, i: (b, 0, 2 * h + 1))],
        out_specs=pl.BlockSpec((1, tq, HD_M), lambda b, h, i: (b, i, h)),
        compiler_params=_cparams("parallel", "parallel", "arbitrary"),
        name="mem_attn",
    )(z3, mkv3, mkv3)


TOK_PAD = 8
NEW_PAD = 128


def _start_pages(pt_ref, b, n_pages, copies):
    def issue(p, _):
        for cp in copies(p, pt_ref[b, p]):
            cp.start()
        return 0

    lax.fori_loop(0, n_pages, issue, 0)


def _wait_pages(pt_ref, b, n_pages, copies):
    def wait(p, _):
        for cp in copies(p, pt_ref[b, p]):
            cp.wait()
        return 0

    lax.fori_loop(0, n_pages, wait, 0)


def _prefetched_pages(pt_ref, n_pages, copies):
    b, g = pl.program_id(0), pl.program_id(1)
    n_groups = pl.num_programs(1)
    n = b * n_groups + g
    slot = n % 2

    @pl.when(n == 0)
    def _():
        _start_pages(pt_ref, b, n_pages, functools.partial(copies, 0, g))

    @pl.when(n + 1 < pl.num_programs(0) * n_groups)
    def _():
        _start_pages(pt_ref, (n + 1) // n_groups, n_pages, functools.partial(copies, 1 - slot, (n + 1) % n_groups))

    _wait_pages(pt_ref, b, n_pages, functools.partial(copies, slot, g))
    return slot


def _page_rows(pool_ref, page, row):
    return pool_ref.at[pl.ds(page * PAGE_SIZE, PAGE_SIZE), row, :]


def _sample_index_kernel(pt_ref, q_ref, w_ref, knew_ref, pool_ref, mp_ref, mn_ref, kbuf, sc_ref, sem, *, topk, n_pages, past_len, chunk):
    b = pl.program_id(0)
    copies = lambda p, page: [pltpu.make_async_copy(pool_ref.at[page], kbuf.at[pl.ds(p * PAGE_SIZE, PAGE_SIZE)], sem)]
    _start_pages(pt_ref, b, n_pages, copies)
    _wait_pages(pt_ref, b, n_pages, copies)
    q = q_ref[0].astype(MXU_DTYPE)
    w = w_ref[0]

    def head_sum(keys):
        x = jnp.maximum(_dot_nt(q, keys.astype(MXU_DTYPE)), 0.0) * w
        return jnp.sum(x.reshape(TOK_PAD, H_I, x.shape[1]), axis=1)

    for c in range(past_len // chunk):
        sc_ref[:, c * chunk:(c + 1) * chunk] = head_sum(kbuf[c * chunk:(c + 1) * chunk, :])
    tok = lax.broadcasted_iota(jnp.int32, (TOK_PAD, NEW_PAD), 0)
    new = lax.broadcasted_iota(jnp.int32, (TOK_PAD, NEW_PAD), 1)
    sc_ref[:, past_len:past_len + NEW_PAD] = jnp.where(new <= tok, head_sum(knew_ref[0]), NEG)
    n_lane_tiles = (past_len + NEW_PAD) // LANE

    def bit_body(i, res_u):
        cand_u = res_u | jnp.left_shift(jnp.int32(1), 31 - i)
        cand = jnp.broadcast_to(_key_to_float(cand_u ^ jnp.int32(INT_MIN)), (TOK_PAD, LANE))
        part = jnp.zeros((TOK_PAD, LANE), F32)
        for t in range(n_lane_tiles):
            part = part + jnp.where(sc_ref[:, t * LANE:(t + 1) * LANE] >= cand, 1.0, 0.0)
        cnt = jnp.sum(part, axis=-1, keepdims=True)
        return jnp.where(cnt >= topk, cand_u, res_u)

    res_u = lax.fori_loop(0, 32, bit_body, jnp.zeros((TOK_PAD, 1), jnp.int32))
    tau = jnp.where(res_u == 0, -jnp.inf, _key_to_float(res_u ^ jnp.int32(INT_MIN)))
    mp_ref[0, 0] = jnp.where(sc_ref[:, 0:past_len] >= tau, 1.0, 0.0)
    mn_ref[0, 0] = jnp.where(new <= tok, jnp.where(sc_ref[:, past_len:past_len + NEW_PAD] >= tau, 1.0, 0.0), 0.0)


def sample_index(page_table, q, w, k_new, pool, topk):
    DB, n_pages = page_table.shape
    past_len = n_pages * PAGE_SIZE
    chunk = min(1024, past_len)
    return pl.pallas_call(
        functools.partial(_sample_index_kernel, topk=topk, n_pages=n_pages, past_len=past_len, chunk=chunk),
        out_shape=(jax.ShapeDtypeStruct((DB, 1, TOK_PAD, past_len), F32), jax.ShapeDtypeStruct((DB, 1, TOK_PAD, NEW_PAD), F32)),
        grid_spec=pltpu.PrefetchScalarGridSpec(
            num_scalar_prefetch=1,
            grid=(DB,),
            in_specs=[pl.BlockSpec((1, TOK_PAD * H_I, D_I), lambda b, pt: (b, 0, 0)),
                      pl.BlockSpec((1, TOK_PAD * H_I, 1), lambda b, pt: (b, 0, 0)),
                      pl.BlockSpec((1, NEW_PAD, D_I), lambda b, pt: (b, 0, 0)),
                      pl.BlockSpec(memory_space=pl.ANY)],
            out_specs=(pl.BlockSpec((1, 1, TOK_PAD, past_len), lambda b, pt: (b, 0, 0, 0)),
                       pl.BlockSpec((1, 1, TOK_PAD, NEW_PAD), lambda b, pt: (b, 0, 0, 0))),
            scratch_shapes=[pltpu.VMEM((past_len, D_I), F32), pltpu.VMEM((TOK_PAD, past_len + NEW_PAD), F32), pltpu.SemaphoreType.DMA(())],
        ),
        compiler_params=_cparams("arbitrary"),
        name="sample_index",
    )(page_table, q, w, k_new, pool)


def _sample_attend_kernel(pt_ref, q_ref, kvn_ref, mp_ref, mn_ref, tp_ref, tn_ref, *rest, gated, n_pages, past_len, chunk):
    if gated:
        gate_ref, prev_ref, pool_ref, o_ref, kbuf, vbuf, sem = rest
    else:
        pool_ref, o_ref, kbuf, vbuf, sem = rest
    def copies(slot, g, p, page):
        dst = lambda buf: buf.at[slot, pl.ds(p * PAGE_SIZE, PAGE_SIZE)]
        return [pltpu.make_async_copy(_page_rows(pool_ref, page, 2 * g), dst(kbuf), sem.at[slot]),
                pltpu.make_async_copy(_page_rows(pool_ref, page, 2 * g + 1), dst(vbuf), sem.at[slot])]

    slot = _prefetched_pages(pt_ref, n_pages, copies)
    q = q_ref[0, 0].astype(MXU_DTYPE)

    def logits(keys, bias_ref, mask_ref, sl):
        s = _dot_nt(q, keys.astype(MXU_DTYPE)) * (HD ** -0.5)
        bias = jnp.concatenate([bias_ref[r, :, sl] for r in range(N_REP)], axis=0)
        keep = mask_ref[0, 0, :, sl] > 0.5
        return jnp.where(jnp.concatenate([keep] * N_REP, axis=0), s + bias, NEG)

    n_chunks = past_len // chunk
    s_past = [logits(kbuf[slot, c * chunk:(c + 1) * chunk, :], tp_ref, mp_ref, slice(c * chunk, (c + 1) * chunk)) for c in range(n_chunks)]
    s_new = logits(kvn_ref[0, 0, 0], tn_ref, mn_ref, slice(0, NEW_PAD))
    m = jnp.max(s_new, axis=-1, keepdims=True)
    for s in s_past:
        m = jnp.maximum(m, jnp.max(s, axis=-1, keepdims=True))
    p_new = jnp.exp(s_new - m)
    l = jnp.sum(p_new, axis=-1, keepdims=True)
    acc = _dot(p_new.astype(MXU_DTYPE), kvn_ref[0, 0, 1].astype(MXU_DTYPE))
    for c, s in enumerate(s_past):
        p = jnp.exp(s - m)
        l = l + jnp.sum(p, axis=-1, keepdims=True)
        acc = acc + _dot(p.astype(MXU_DTYPE), vbuf[slot, c * chunk:(c + 1) * chunk, :].astype(MXU_DTYPE))
    o = acc / l
    if gated:
        o = prev_ref[0, 0] + o * jax.nn.sigmoid(gate_ref[0, 0])
    o_ref[0, 0] = o


def sample_attend(page_table, q, kv_new, mask_past, mask_new, tab_past, tab_new, pool, gate_prev=None):
    DB, n_pages = page_table.shape
    G = q.shape[1]
    past_len = n_pages * PAGE_SIZE
    chunk = min(1024, past_len)
    rows = N_REP * TOK_PAD
    mb, mg = mask_past.shape[0] > 1, mask_past.shape[1] > 1
    mask_map = lambda b, g, pt: (b if mb else 0, g if mg else 0, 0, 0)
    row_spec = lambda n: pl.BlockSpec((1, 1, rows, n), lambda b, g, pt: (b, g, 0, 0))
    in_specs = [row_spec(HD),
                pl.BlockSpec((1, 1, 2, NEW_PAD, HD), lambda b, g, pt: (b, g, 0, 0, 0)),
                pl.BlockSpec((1, 1, TOK_PAD, past_len), mask_map),
                pl.BlockSpec((1, 1, TOK_PAD, NEW_PAD), mask_map),
                pl.BlockSpec((N_REP, TOK_PAD, past_len), lambda b, g, pt: (g, 0, 0)),
                pl.BlockSpec((N_REP, TOK_PAD, NEW_PAD), lambda b, g, pt: (g, 0, 0))]
    args = [q, kv_new, mask_past, mask_new, tab_past, tab_new]
    if gate_prev is not None:
        in_specs += [row_spec(1), row_spec(HD)]
        args += list(gate_prev)
    in_specs.append(pl.BlockSpec(memory_space=pl.ANY))
    args.append(pool)
    return pl.pallas_call(
        functools.partial(_sample_attend_kernel, gated=gate_prev is not None, n_pages=n_pages, past_len=past_len, chunk=chunk),
        out_shape=jax.ShapeDtypeStruct((DB, G, rows, HD), F32),
        grid_spec=pltpu.PrefetchScalarGridSpec(
            num_scalar_prefetch=1,
            grid=(DB, G),
            in_specs=in_specs,
            out_specs=row_spec(HD),
            scratch_shapes=[pltpu.VMEM((2, past_len, HD), F32), pltpu.VMEM((2, past_len, HD), F32), pltpu.SemaphoreType.DMA((2,))],
        ),
        compiler_params=_cparams("arbitrary", "arbitrary"),
        name="sample_attend",
    )(page_table, *args)


def _sample_compress_kernel(pt_ref, cnew_ref, w1_ref, pe_ref, w2_ref, gain_ref, pool_ref, o_ref, cbuf, sem, *, n_pages, n_blocks):
    def copies(slot, g, p, page):
        return [pltpu.make_async_copy(_page_rows(pool_ref, page, 2 * g + c), cbuf.at[slot, c, pl.ds(p * PAGE_SIZE, PAGE_SIZE)], sem.at[slot])
                for c in range(2)]

    slot = _prefetched_pages(pt_ref, n_pages, copies)
    n_cached = n_pages * PAGE_SIZE
    n_tok = cbuf.shape[2]
    for c in range(2):
        cbuf[slot, c, n_cached:n_cached + CMP_STRIDE] = cnew_ref[0, 0, c]
        cbuf[slot, c, n_cached + CMP_STRIDE:n_tok] = jnp.zeros((n_tok - n_cached - CMP_STRIDE, HD), F32)

    nch = n_tok // CMP_STRIDE
    for c in range(2):
        acc = jnp.zeros((nch + 8, 2 * HD), F32)
        for sp in range(CMP_STRIDE // 2):
            rows = [cbuf.at[slot, c][pl.ds(2 * sp + d, nch, stride=CMP_STRIDE), :] for d in range(2)]
            lhs = jnp.concatenate([jnp.concatenate(rows, axis=1), pe_ref[c, sp]], axis=0).astype(MXU_DTYPE)
            acc = acc + _dot(lhs, w1_ref[c, sp])
        pe_term = acc[nch:nch + 1, 0:HD] + acc[nch + 1:nch + 2, HD:2 * HD]
        hid = pe_term + acc[0:nch, 0:HD] + pltpu.roll(acc[0:nch, HD:2 * HD], nch - 1, 0)
        out = _dot((hid * jax.nn.sigmoid(hid)).astype(MXU_DTYPE), w2_ref[c])[0:n_blocks]
        if c == 0:
            ms = jnp.mean(out * out, axis=-1, keepdims=True)
            out = (out * lax.rsqrt(ms + EPS)) * gain_ref[...]
        o_ref[0, c] = out


def sample_compress(page_table, c_new, pool, cmp_pe, cmp_w1, cmp_w2, gain_c):
    DB, n_pages = page_table.shape
    n_blocks = n_pages * (PAGE_SIZE // CMP_STRIDE)
    n_slots = _round_up(n_blocks + 1, 8)
    r = CMP_LEN // CMP_STRIDE
    assert r == 2
    half = CMP_STRIDE // 2
    w1 = cmp_w1.reshape(2, r, half, 2, HD, HD).transpose(0, 2, 3, 4, 1, 5).reshape(2, half, 2 * HD, r * HD).astype(MXU_DTYPE)
    pe = cmp_pe.reshape(r, half, 2, 2, HD).transpose(3, 1, 0, 2, 4).reshape(2, half, r, 2 * HD)
    pe = jnp.pad(pe, ((0, 0), (0, 0), (0, 8 - r), (0, 0)))
    full = lambda a: pl.BlockSpec(a.shape, lambda b, g, pt: (0,) * a.ndim)
    w2 = cmp_w2.astype(MXU_DTYPE)
    gain = gain_c.reshape(1, HD)
    return pl.pallas_call(
        functools.partial(_sample_compress_kernel, n_pages=n_pages, n_blocks=n_blocks),
        out_shape=jax.ShapeDtypeStruct((DB, G_B * 2, n_blocks, HD), F32),
        grid_spec=pltpu.PrefetchScalarGridSpec(
            num_scalar_prefetch=1,
            grid=(DB, G_B),
            in_specs=[pl.BlockSpec((1, 1, 2, CMP_STRIDE, HD), lambda b, g, pt: (b, g, 0, 0, 0)),
                      full(w1), full(pe), full(w2), full(gain), pl.BlockSpec(memory_space=pl.ANY)],
            out_specs=pl.BlockSpec((1, 2, n_blocks, HD), lambda b, g, pt: (b, g, 0, 0)),
            scratch_shapes=[pltpu.VMEM((2, 2, n_slots * CMP_STRIDE, HD), F32), pltpu.SemaphoreType.DMA((2,))],
        ),
        compiler_params=_cparams("arbitrary", "arbitrary"),
        name="sample_compress",
    )(page_table, c_new, w1, pe, w2, gain, pool)


def _sample_cmp_kernel(q_ref, kc_ref, vc_ref, map_ref, e_ref, gate_ref, oc_ref, mp_ref, mn_ref, *, past_len, n_real, n_sel):
    q = q_ref[0, 0].astype(MXU_DTYPE)
    kc = kc_ref[0, 0].astype(MXU_DTYPE)
    vc = vc_ref[0, 0].astype(MXU_DTYPE)
    rows = q.shape[0]
    nch = kc.shape[0]
    t = past_len + (lax.broadcasted_iota(jnp.int32, (rows, nch), 0) & (TOK_PAD - 1))
    n = lax.broadcasted_iota(jnp.int32, (rows, nch), 1)
    valid = (n * CMP_STRIDE + CMP_LEN - 1) <= t
    lm = jnp.where(valid, _dot_nt(q, kc) * (HD ** -0.5), NEG)
    m = jnp.max(lm, axis=-1, keepdims=True)
    p = jnp.where(valid, jnp.exp(lm - m), 0.0)
    den = jnp.sum(p, axis=-1, keepdims=True)
    pc = (p / jnp.where(den > 0.0, den, 1.0)).astype(MXU_DTYPE)
    oc_ref[0, 0] = _dot(pc, vc) * jax.nn.sigmoid(gate_ref[0, 0])
    imp_rows = _dot(pc, map_ref[...])
    imp = imp_rows[0:TOK_PAD]
    for r in range(1, N_REP):
        imp = imp + imp_rows[r * TOK_PAD:(r + 1) * TOK_PAD]

    nsb = imp.shape[1]
    tpos = past_len + lax.broadcasted_iota(jnp.int32, (TOK_PAD, nsb), 0)
    blk = lax.broadcasted_iota(jnp.int32, (TOK_PAD, nsb), 1)
    cur = tpos // SEL_LEN
    forced_or_imp = jnp.where(blk == 0, BIG, jnp.where(blk == cur, BIG, jnp.where(blk == cur - 1, BIG, imp)))
    score = jnp.where(blk <= cur, forced_or_imp, jnp.where(blk < n_real, NEG, -jnp.inf))
    blk_f = blk.astype(F32)
    sel = jnp.zeros((TOK_PAD, nsb), F32)
    for _ in range(n_sel):
        best = jnp.max(score, axis=-1, keepdims=True)
        first = jnp.min(jnp.where(score == best, blk_f, float(nsb)), axis=-1, keepdims=True)
        hit = blk_f == first
        sel = jnp.where(hit, 1.0, sel)
        score = jnp.where(hit, -jnp.inf, score)
    keep = _dot(sel.astype(MXU_DTYPE), e_ref[...])
    mp_ref[0, 0] = jnp.where(keep[:, 0:past_len] > 0.5, 1.0, 0.0)
    tok = lax.broadcasted_iota(jnp.int32, (TOK_PAD, NEW_PAD), 0)
    new = lax.broadcasted_iota(jnp.int32, (TOK_PAD, NEW_PAD), 1)
    mn_ref[0, 0] = jnp.where(new <= tok, jnp.where(keep[:, past_len:] > 0.5, 1.0, 0.0), 0.0)


def sample_cmp_attention(q, kcv, gate, past_len, n_tokens):
    DB, G, rows, _ = q.shape
    nc = kcv.shape[2]
    n_real = -(-(past_len + n_tokens) // SEL_LEN)
    nsb = _round_up(n_real, LANE)
    sel_map = cmp_to_sel(nc, nsb).astype(MXU_DTYPE)
    key = jnp.arange(past_len + NEW_PAD)[None, :]
    expand = ((key // SEL_LEN == jnp.arange(nsb)[:, None]) & (key < past_len + n_tokens)).astype(MXU_DTYPE)
    row_spec = lambda n: pl.BlockSpec((1, 1, rows, n), lambda b, g: (b, g, 0, 0))
    return pl.pallas_call(
        functools.partial(_sample_cmp_kernel, past_len=past_len, n_real=n_real, n_sel=min(N_SEL, n_real)),
        out_shape=(jax.ShapeDtypeStruct((DB, G, rows, HD), F32), jax.ShapeDtypeStruct((DB, G, TOK_PAD, past_len), F32),
                   jax.ShapeDtypeStruct((DB, G, TOK_PAD, NEW_PAD), F32)),
        grid=(DB, G),
        in_specs=[row_spec(HD),
                  pl.BlockSpec((1, 1, nc, HD), lambda b, g: (b, 2 * g, 0, 0)),
                  pl.BlockSpec((1, 1, nc, HD), lambda b, g: (b, 2 * g + 1, 0, 0)),
                  pl.BlockSpec(sel_map.shape, lambda b, g: (0, 0)),
                  pl.BlockSpec(expand.shape, lambda b, g: (0, 0)),
                  row_spec(1)],
        out_specs=(row_spec(HD), pl.BlockSpec((1, 1, TOK_PAD, past_len), lambda b, g: (b, g, 0, 0)),
                   pl.BlockSpec((1, 1, TOK_PAD, NEW_PAD), lambda b, g: (b, g, 0, 0))),
        compiler_params=_cparams("parallel", "parallel"),
        name="sample_cmp",
    )(q, kcv, kcv, sel_map, expand, gate)


def t5_bucket(dist):
    max_exact = N_BUCKETS // 2
    n = jnp.maximum(dist, 0)
    nf = jnp.maximum(n, 1).astype(jnp.float32)
    large = max_exact + (jnp.log(nf / max_exact) / math.log(MAX_DIST / max_exact) * (N_BUCKETS - max_exact)).astype(jnp.int32)
    return jnp.where(n < max_exact, n, jnp.minimum(large, N_BUCKETS - 1))


def cmp_to_sel(n_cmp, n_sel_blocks):
    cs = jnp.arange(n_cmp)[:, None] * CMP_STRIDE
    ss = jnp.arange(n_sel_blocks)[None, :] * SEL_LEN
    return ((cs < ss + SEL_LEN) & (cs + CMP_LEN > ss)).astype(jnp.float32)


def toeplitz_bias_tiles(bias_tab, tq):
    max_exact = N_BUCKETS // 2
    nearest_far = np.float32(tq + 1)
    assert max_exact + int(np.log(nearest_far / max_exact) / math.log(MAX_DIST / max_exact) * (N_BUCKETS - max_exact)) >= N_BUCKETS - 1
    rel = (bias_tab - bias_tab[N_BUCKETS - 1]) * LOG2E
    ii = jnp.arange(tq)[:, None]
    jj = jnp.arange(tq)[None, :]
    bucket = jnp.stack([t5_bucket(ii - jj), t5_bucket(tq + ii - jj)])
    onehot = (bucket[None] == jnp.arange(N_BUCKETS)[:, None, None, None]).astype(F32)
    tiles = jnp.einsum('bh,bctk->hctk', rel, onehot, precision=lax.Precision.HIGHEST)
    return tiles.reshape(G_A, N_REP, 2, tq, tq)


def block_expand_matrix(n_blocks, n_tiles, tk):
    b = jnp.arange(n_blocks)[None, :, None]
    key = (jnp.arange(n_tiles)[:, None, None] * tk + jnp.arange(tk)[None, None, :])
    return (key // SEL_LEN == b).astype(MXU_DTYPE)


def indexer_key_tiles(ki, tk):
    S = ki.shape[0]
    kt = ki.reshape(S // tk, tk, D_I).transpose(0, 2, 1).astype(MXU_DTYPE)
    zero = jnp.zeros_like(kt)
    return jnp.concatenate([jnp.concatenate([kt, zero], axis=2), jnp.concatenate([zero, kt], axis=2)], axis=1)


def _pad_cols(w, n):
    return jnp.pad(w, ((0, 0), (0, n - w.shape[1])))


def _seg(w, idx):
    return w[:, PROJ_OFFSETS[idx]:PROJ_OFFSETS[idx + 1]]


def prepare_weights(w_in, w_mem_kv, w_up_a, w_up_b, w_up_m, w_out, w_ffn_in, w_ffn_out):
    bf = MXU_DTYPE
    d_ff_pad = _round_up(D_FF, COL_TILE)
    order = (2, 0, 1, 5, 6, 7, 8, 10, 3, 4, 9)
    main = jnp.concatenate([_seg(w_in, idx).astype(bf) for idx in order] + [jnp.zeros((w_in.shape[0], N_MAIN - D_MAIN), bf)], axis=1)
    return dict(
        w_main=main,
        w_mg=_seg(w_in, 11).astype(bf),
        w_mem_kv=w_mem_kv.astype(bf),
        w_up_a=w_up_a.astype(bf), w_up_b=w_up_b.astype(bf), w_up_m=w_up_m.astype(bf),
        w_out=w_out.astype(bf),
        w_ffn_a=_pad_cols(w_ffn_in[:, :D_FF], d_ff_pad).astype(bf),
        w_ffn_u=_pad_cols(w_ffn_in[:, D_FF:], d_ff_pad).astype(bf),
        w_ffn_out=jnp.pad(w_ffn_out, ((0, d_ff_pad - D_FF), (0, 0))).astype(bf),
    )


def main_norm_vectors(qk_gain_a, qk_gain_b, qk_gain_m):
    one, zero = jnp.ones((HD,), F32), jnp.zeros((HD,), F32)

    def kv(gk):
        return jnp.tile(jnp.concatenate([gk, zero]), G_A), jnp.tile(jnp.concatenate([one, zero]), G_A)

    kva, fa = kv(qk_gain_a[1])
    slc, fs = kv(qk_gain_b[2])
    win, fw = kv(qk_gain_b[3])
    z = lambda n: jnp.zeros((n,), F32)
    gain = jnp.concatenate([z(H_I * D_I), jnp.tile(qk_gain_a[0], H_A), kva, jnp.tile(qk_gain_b[0], H_B), z(G_B * 2 * HD), slc, win,
                            jnp.tile(qk_gain_m[0], H_M), z(N_MAIN - COL_SMALL)])
    m128 = jnp.concatenate([z(H_I * D_I), jnp.ones((H_A * HD,), F32), fa, jnp.ones((H_B * HD,), F32), z(G_B * 2 * HD), fs, fw,
                            z(H_M * HD_M), z(N_MAIN - COL_SMALL)])
    m256 = jnp.concatenate([z(COL_QM), jnp.ones((H_M * HD_M,), F32), z(N_MAIN - COL_SMALL)])
    cols = np.arange(N_MAIN)
    has = ((cols >= COL_QA) & (cols < COL_CMP)) | ((cols >= COL_SLC) & (cols < COL_SMALL))
    tile_mode = jnp.asarray(has.reshape(-1, COL_TILE).any(axis=1).astype(np.int32))
    return gain.reshape(1, -1), m128.reshape(1, -1), m256.reshape(1, -1), tile_mode


def project_in(x2d, g_attn, W, norm_vecs, tm):
    h = rms_cast(x2d, g_attn, min(tm, 512))
    gain, m128, m256, tile_mode = norm_vecs
    z_main = matmul_headnorm(h, W['w_main'], gain, m128, m256, tile_mode, tm, COL_TILE, "in_proj_main")
    mg = matmul(h, W['w_mg'], tm, COL_TILE, name="in_proj_mg")
    return z_main, mg


def merge_ffn(x2d, o_a, o_b, o_m, mg, W, g_ffn, tm):
    mix = upmix(o_a, o_b, o_m, W['w_up_a'], W['w_up_b'], W['w_up_m'], mg, min(tm, 512), COL_TILE)
    x2 = matmul_residual(mix, W['w_out'], x2d, tm, COL_TILE)
    h2 = rms_cast(x2, g_ffn, min(tm, 512))
    act = swiglu_in(h2, W['w_ffn_a'], W['w_ffn_u'], tm, COL_TILE)
    tk = act.shape[1] // 4
    return matmul_residual_ksplit(act, W['w_ffn_out'], x2, tm, COL_TILE, tk)


def prompt_mixers(z, mkv, rel_bias, cmp_pe, cmp_w1, cmp_w2, gain_c, tq):
    T = z.shape[0]
    nk = T // tq
    bias_a, bias_b = rel_bias[:, :H_A], rel_bias[:, H_A:]
    near_a, near_b = toeplitz_bias_tiles(bias_a, tq), toeplitz_bias_tiles(bias_b, tq)
    ki = z[:, COL_SMALL + OFF_KI:COL_SMALL + OFF_KI + D_I]
    scores, tau = dsa_index(z, indexer_key_tiles(ki, tq), min(DSA_TOPK, T // 4), tq)
    o_a = prompt_attention('dsa', z, COL_QA, COL_KVA, near_a, tq, (scores, tau))
    nch = T // CMP_STRIDE
    nsb = T // SEL_LEN
    kcv = compress_blocks(z.reshape(nch, CMP_STRIDE, z.shape[1]), COL_CMP, cmp_pe, cmp_w1, cmp_w2, gain_c)
    sel_map = cmp_to_sel(nch, nsb).astype(MXU_DTYPE)
    oc, sel = nsa_cmp_attention(z, COL_QB, kcv, sel_map, tq)
    ocs = prompt_attention('slc', z, COL_QB, COL_SLC, near_b, tq, (sel, block_expand_matrix(nsb, nk, tq), oc))
    o_b = prompt_attention('win', z, COL_QB, COL_WIN, near_b, tq, (ocs,))
    o_m = mem_attention(z[None], COL_QM, mkv[None], tq)[0]
    return o_a, o_b, o_m


def sample_mixers(zs, page_table, pool_a_kv, pool_a_idx, pool_b_cmp, pool_b_slc, cache_win, cache_mem, rel_bias, cmp_pe, cmp_w1, cmp_w2, gain_c, DS):
    DB, n_pages = page_table.shape
    past_len = n_pages * PAGE_SIZE
    bias_a, bias_b = rel_bias[:, :H_A], rel_bias[:, H_A:]
    seg = lambda c, n: zs[:, c:c + n].reshape(DB, DS, n)
    pad_tok = lambda a: jnp.pad(a, ((0, 0), (0, TOK_PAD - DS)) + ((0, 0),) * (a.ndim - 2))
    rows = N_REP * TOK_PAD

    def q_rows(c):
        q = pad_tok(seg(c, H_A * HD)).reshape(DB, TOK_PAD, G_A, N_REP, HD)
        return q.transpose(0, 2, 3, 1, 4).reshape(DB, G_A, rows, HD)

    def kv_new(c):
        kv = seg(c, G_A * 2 * HD).reshape(DB, DS, G_A, 2, HD).transpose(0, 2, 3, 1, 4)
        return jnp.pad(kv, ((0, 0), (0, 0), (0, 0), (0, NEW_PAD - DS), (0, 0)))

    tok = jnp.arange(TOK_PAD)[:, None]
    new = jnp.arange(NEW_PAD)[None, :]

    def bias_tables(bias_tab, span):
        near = min(span, NEW_PAD)
        assert near == span or near + 1 >= MAX_DIST
        dist_near = near + tok - jnp.arange(near)[None, :]
        tab_near = bias_tab[t5_bucket(dist_near)].transpose(2, 0, 1)
        tab_far = jnp.broadcast_to(bias_tab[N_BUCKETS - 1][:, None, None], (bias_tab.shape[1], TOK_PAD, span - near))
        return jnp.concatenate([tab_far, tab_near], axis=2), bias_tab[t5_bucket(tok - new)].transpose(2, 0, 1)

    unrows = lambda o: o.reshape(DB, G_A, N_REP, TOK_PAD, HD)[:, :, :, :DS].transpose(0, 3, 1, 2, 4).reshape(DB * DS, H_A * HD)

    qi = pad_tok(seg(COL_QI, H_I * D_I)).reshape(DB, TOK_PAD * H_I, D_I)
    wi = pad_tok(seg(COL_SMALL + OFF_WI, H_I)).reshape(DB, TOK_PAD * H_I, 1)
    ki_new = jnp.pad(seg(COL_SMALL + OFF_KI, D_I), ((0, 0), (0, NEW_PAD - DS), (0, 0)))
    keep_p, keep_n = sample_index(page_table, qi, wi, ki_new, pool_a_idx, min(DSA_TOPK, (past_len + DS) // 4))
    tab_p, tab_n = bias_tables(bias_a, past_len)
    token_rows = lambda pool: pool.reshape(-1, G_A * 2, HD)
    o_a = sample_attend(page_table, q_rows(COL_QA), kv_new(COL_KVA), keep_p, keep_n, tab_p, tab_n, token_rows(pool_a_kv))

    c_new = seg(COL_CMP, G_B * 2 * HD).reshape(DB, DS, G_B, 2, HD).transpose(0, 2, 3, 1, 4)
    c_new = jnp.pad(c_new, ((0, 0), (0, 0), (0, 0), (0, CMP_STRIDE - DS), (0, 0)))
    kcv = sample_compress(page_table, c_new, token_rows(pool_b_cmp), cmp_pe, cmp_w1, cmp_w2, gain_c)
    gates = pad_tok(seg(COL_SMALL + OFF_GATE, 3 * H_B)).reshape(DB, TOK_PAD, G_B, N_REP, 3)
    gates = gates.transpose(4, 0, 2, 3, 1).reshape(3, DB, G_B, rows, 1)
    q_b = q_rows(COL_QB)
    oc, keep_p, keep_n = sample_cmp_attention(q_b, kcv, gates[0], past_len, DS)
    tab_p, tab_n = bias_tables(bias_b, past_len)
    ocs = sample_attend(page_table, q_b, kv_new(COL_SLC), keep_p, keep_n, tab_p, tab_n, token_rows(pool_b_slc), gate_prev=(gates[1], oc))
    w_eff = cache_win.shape[1]
    w_pages = w_eff // PAGE_SIZE
    win_pool = token_rows(cache_win)
    win_pt = jnp.arange(DB * w_pages, dtype=jnp.int32).reshape(DB, w_pages)
    keep_w = ((w_eff + tok - jnp.arange(w_eff)[None, :]) < WINDOW).astype(F32).reshape(1, 1, TOK_PAD, w_eff)
    keep_wn = ((new <= tok) & (new < DS)).astype(F32).reshape(1, 1, TOK_PAD, NEW_PAD)
    tab_p, tab_n = bias_tables(bias_b, w_eff)
    o_b = sample_attend(win_pt, q_b, kv_new(COL_WIN), keep_w, keep_wn, tab_p, tab_n, win_pool, gate_prev=(gates[2], ocs))

    n_mem = cache_mem.shape[1]
    o_m = mem_attention(zs.reshape(DB, DS, -1), COL_QM, cache_mem.reshape(DB, n_mem, H_M * 2 * HD_M), DS)
    return unrows(o_a), unrows(o_b), o_m.reshape(DB * DS, H_M * HD_M)


def kernel(x_prompt, x_sample, cache_a_kv, cache_a_idx, cache_b_cmp, cache_b_slc, cache_b_win, cache_mem_kv, page_table, mem_prompt, g_attn, w_in, qk_gain_a, qk_gain_b, qk_gain_m, cmp_pe, cmp_w1, cmp_w2, rel_bias, g_mem, w_mem_kv, w_up_a, w_up_b, w_up_m, w_out, g_ffn, w_ffn_in, w_ffn_out):
    l = 0
    B, T, D = x_prompt.shape
    assert B == 1
    DB, DS = x_sample.shape[:2]
    ki_cols = slice(COL_SMALL + OFF_KI, COL_SMALL + OFF_KI + D_I)
    W = prepare_weights(w_in[l], w_mem_kv[l], w_up_a[l], w_up_b[l], w_up_m[l], w_out[l], w_ffn_in[l], w_ffn_out[l])
    norm_vecs = main_norm_vectors(qk_gain_a[l], qk_gain_b[l], qk_gain_m[l])

    xp = x_prompt.reshape(T, D)
    z, mg = project_in(xp, g_attn[l], W, norm_vecs, 1024)
    n_mem = mem_prompt.shape[1]
    hm = rms_cast(mem_prompt.reshape(n_mem, D), g_mem[l], n_mem)
    kgain = jnp.tile(jnp.concatenate([qk_gain_m[l, 1], jnp.zeros((HD_M,), F32)]), H_M).reshape(1, -1)
    kflag = jnp.tile(jnp.concatenate([jnp.ones((HD_M,), F32), jnp.zeros((HD_M,), F32)]), H_M).reshape(1, -1)
    mkv = matmul_headnorm(hm, W['w_mem_kv'], kgain, jnp.zeros_like(kflag), kflag,
                          jnp.ones((kflag.shape[1] // COL_TILE,), jnp.int32), n_mem, COL_TILE, "mem_kv")
    o_a, o_b, o_m = prompt_mixers(z, mkv, rel_bias, cmp_pe[l], cmp_w1[l], cmp_w2[l], qk_gain_b[l, 1], ATTN_TILE)
    y_p = merge_ffn(xp, o_a, o_b, o_m, mg, W, g_ffn[l], 1024).reshape(B, T, D)
    kv5 = lambda c, rows: z[rows, c:c + G_A * 2 * HD].reshape(1, 1, -1, G_A, 2, HD)
    p_out = (kv5(COL_KVA, slice(None)), z[:, COL_SMALL + OFF_KI:COL_SMALL + OFF_KI + D_I].reshape(1, 1, T, D_I),
             kv5(COL_CMP, slice(None)), kv5(COL_SLC, slice(None)), kv5(COL_WIN, slice(T - min(WINDOW, T), T)),
             mkv.reshape(1, 1, n_mem, H_M, 2, HD_M))

    n_s = DB * DS
    xs = x_sample.reshape(n_s, D)
    zs, mg = project_in(xs, g_attn[l], W, norm_vecs, n_s)
    o_a, o_b, o_m = sample_mixers(zs, page_table, cache_a_kv[l], cache_a_idx[l], cache_b_cmp[l], cache_b_slc[l], cache_b_win[l],
                                  cache_mem_kv[l], rel_bias, cmp_pe[l], cmp_w1[l], cmp_w2[l], qk_gain_b[l, 1], DS)
    y_s = merge_ffn(xs, o_a, o_b, o_m, mg, W, g_ffn[l], n_s).reshape(DB, DS, D)
    new5 = lambda c: zs[:, c:c + G_A * 2 * HD].reshape(1, DB, DS, G_A, 2, HD)
    s_out = (new5(COL_KVA), zs[:, ki_cols].reshape(1, DB, DS, D_I), new5(COL_CMP), new5(COL_SLC),
             jnp.concatenate([cache_b_win[l], new5(COL_WIN)[0]], axis=1)[None, :, DS:])
    return (y_p, y_s) + p_out + s_out
```

```python
import functools
import math

import numpy as np
import jax
import jax.numpy as jnp
from jax import lax
from jax.experimental import pallas as pl
from jax.experimental.pallas import tpu as pltpu

D_MODEL = 4096
PAGE_SIZE = 128
HD = 128
H_A = 12
G_A = 4
H_I = 32
D_I = 64
DSA_TOPK = 256
H_B = 12
G_B = 4
CMP_LEN = 32
CMP_STRIDE = 16
SEL_LEN = 64
N_SEL = 16
WINDOW = 512
N_MEM = 256
H_M = 4
HD_M = 256
N_BUCKETS = 32
MAX_DIST = 128
D_FF = -(-8 * D_MODEL // (3 * 256)) * 256
Q_BLOCK = 128
EPS = 1e-6
NEG = -1e30
BIG = 1e30
PROJ_SIZES = (H_A * HD, G_A * 2 * HD, H_I * D_I, H_I, D_I, H_B * HD, G_B * 2 * HD, G_B * 2 * HD, G_B * 2 * HD, 3 * H_B, H_M * HD_M, 3 * D_MODEL)
PROJ_OFFSETS = tuple(int(o) for o in np.concatenate([[0], np.cumsum(PROJ_SIZES)]))
D_MAIN = PROJ_OFFSETS[-2]
N_REP = H_A // G_A

VMEM_LIMIT_BYTES = 56 * 1024 * 1024
LANE = 128
COL_TILE = 512
ATTN_TILE = 256
BISECT_ROWS = 128
FAR_WIDTH = 8
LOG2E = math.log2(math.e)
MXU_DTYPE = jnp.bfloat16
F32 = jnp.float32
INT_MIN = -2 ** 31

COL_QI = 0
COL_QA = COL_QI + H_I * D_I
COL_KVA = COL_QA + H_A * HD
COL_QB = COL_KVA + G_A * 2 * HD
COL_CMP = COL_QB + H_B * HD
COL_SLC = COL_CMP + G_B * 2 * HD
COL_WIN = COL_SLC + G_B * 2 * HD
COL_QM = COL_WIN + G_B * 2 * HD
COL_SMALL = COL_QM + H_M * HD_M
SMALL_W = 256
OFF_WI, OFF_KI, OFF_GATE = 0, H_I, H_I + D_I
N_MAIN = -(-(COL_SMALL + SMALL_W) // COL_TILE) * COL_TILE


def _round_up(n, m):
    return -(-n // m) * m


def _cparams(*sem):
    return pltpu.CompilerParams(dimension_semantics=sem, vmem_limit_bytes=VMEM_LIMIT_BYTES)


def _dot(a, b):
    return jnp.dot(a, b, preferred_element_type=F32)


def _dot_nt(a, b):
    return lax.dot_general(a, b, (((1,), (1,)), ((), ())), preferred_element_type=F32)


def _float_key(x):
    b = pltpu.bitcast(x, jnp.int32)
    return b ^ ((b >> 31) & jnp.int32(0x7FFFFFFF))


def _key_to_float(k):
    return pltpu.bitcast(k ^ ((k >> 31) & jnp.int32(0x7FFFFFFF)), F32)


def _np_float_key(v):
    b = int(np.array(v, np.float32).view(np.int32))
    return b ^ ((b >> 31) & 0x7FFFFFFF)


def _rms_cast_kernel(x_ref, g_ref, o_ref):
    x = x_ref[...]
    ms = jnp.mean(x * x, axis=-1, keepdims=True)
    o_ref[...] = ((x * lax.rsqrt(ms + EPS)) * g_ref[...]).astype(o_ref.dtype)


def rms_cast(x, g, tm):
    M, D = x.shape
    return pl.pallas_call(
        _rms_cast_kernel,
        out_shape=jax.ShapeDtypeStruct((M, D), MXU_DTYPE),
        grid=(M // tm,),
        in_specs=[pl.BlockSpec((tm, D), lambda i: (i, 0)), pl.BlockSpec((1, D), lambda i: (0, 0))],
        out_specs=pl.BlockSpec((tm, D), lambda i: (i, 0)),
        compiler_params=_cparams("parallel"),
        name="rms_cast",
    )(x, g.reshape(1, D))


def _mm_kernel(a_ref, w_ref, o_ref):
    o_ref[...] = _dot(a_ref[...], w_ref[...]).astype(o_ref.dtype)


def matmul(a, w, tm, tn, out_dtype=F32, name="matmul"):
    M, K = a.shape
    N = w.shape[1]
    return pl.pallas_call(
        _mm_kernel,
        out_shape=jax.ShapeDtypeStruct((M, N), out_dtype),
        grid=(M // tm, N // tn),
        in_specs=[pl.BlockSpec((tm, K), lambda i, j: (i, 0)), pl.BlockSpec((K, tn), lambda i, j: (0, j))],
        out_specs=pl.BlockSpec((tm, tn), lambda i, j: (i, j)),
        compiler_params=_cparams("parallel", "arbitrary"),
        name=name,
    )(a, w)


def _norm_cols(z, gain, m128, m256):
    tn = z.shape[1]
    sq = z * z
    ssq = [jnp.sum(sq[:, c * LANE:(c + 1) * LANE], axis=-1, keepdims=True) for c in range(tn // LANE)]
    cols = []
    for c in range(tn // LANE):
        sl = slice(c * LANE, (c + 1) * LANE)
        ms1 = ssq[c] * (1.0 / LANE)
        ms2 = (ssq[c - c % 2] + ssq[c - c % 2 + 1]) * (1.0 / (2 * LANE))
        ms = jnp.where(m256[:, sl] > 0, ms2, ms1)
        zc = z[:, sl]
        zn = (zc * lax.rsqrt(ms + EPS)) * gain[:, sl]
        cols.append(jnp.where((m128[:, sl] + m256[:, sl]) > 0, zn, zc))
    return jnp.concatenate(cols, axis=-1)


def _mm_norm_kernel(mode_ref, a_ref, w_ref, gain_ref, m128_ref, m256_ref, o_ref):
    j = pl.program_id(1)
    z = _dot(a_ref[...], w_ref[...])

    @pl.when(mode_ref[j] == 0)
    def _():
        o_ref[...] = z

    @pl.when(mode_ref[j] != 0)
    def _():
        o_ref[...] = _norm_cols(z, gain_ref[...], m128_ref[...], m256_ref[...])


def matmul_headnorm(a, w, gain, m128, m256, tile_mode, tm, tn, name):
    M, K = a.shape
    N = w.shape[1]
    vec = pl.BlockSpec((1, tn), lambda i, j, m: (0, j))
    return pl.pallas_call(
        _mm_norm_kernel,
        out_shape=jax.ShapeDtypeStruct((M, N), F32),
        grid_spec=pltpu.PrefetchScalarGridSpec(
            num_scalar_prefetch=1,
            grid=(M // tm, N // tn),
            in_specs=[pl.BlockSpec((tm, K), lambda i, j, m: (i, 0)), pl.BlockSpec((K, tn), lambda i, j, m: (0, j)), vec, vec, vec],
            out_specs=pl.BlockSpec((tm, tn), lambda i, j, m: (i, j)),
        ),
        compiler_params=_cparams("parallel", "arbitrary"),
        name=name,
    )(tile_mode, a, w, gain, m128, m256)


def _upmix_kernel(oa_ref, ob_ref, om_ref, wa_ref, wb_ref, wm_ref, g0_ref, g1_ref, g2_ref, o_ref):
    ya = _dot(oa_ref[...].astype(MXU_DTYPE), wa_ref[...])
    yb = _dot(ob_ref[...].astype(MXU_DTYPE), wb_ref[...])
    ym = _dot(om_ref[...].astype(MXU_DTYPE), wm_ref[...])
    mix = jax.nn.sigmoid(g0_ref[...]) * ya + jax.nn.sigmoid(g1_ref[...]) * yb + jax.nn.sigmoid(g2_ref[...]) * ym
    o_ref[...] = mix.astype(o_ref.dtype)


def upmix(o_a, o_b, o_m, w_a, w_b, w_m, mg, tm, tn):
    M = o_a.shape[0]
    D = w_a.shape[1]
    nj = D // tn
    a_spec = lambda k: pl.BlockSpec((tm, k), lambda i, j: (i, 0))
    w_spec = lambda k: pl.BlockSpec((k, tn), lambda i, j: (0, j))
    g_spec = lambda c: pl.BlockSpec((tm, tn), lambda i, j: (i, j + c * nj))
    return pl.pallas_call(
        _upmix_kernel,
        out_shape=jax.ShapeDtypeStruct((M, D), MXU_DTYPE),
        grid=(M // tm, nj),
        in_specs=[a_spec(o_a.shape[1]), a_spec(o_b.shape[1]), a_spec(o_m.shape[1]),
                  w_spec(w_a.shape[0]), w_spec(w_b.shape[0]), w_spec(w_m.shape[0]),
                  g_spec(0), g_spec(1), g_spec(2)],
        out_specs=pl.BlockSpec((tm, tn), lambda i, j: (i, j)),
        compiler_params=_cparams("parallel", "arbitrary"),
        name="upmix",
    )(o_a, o_b, o_m, w_a, w_b, w_m, mg, mg, mg)


def _mm_res_kernel(a_ref, w_ref, r_ref, o_ref):
    o_ref[...] = r_ref[...] + _dot(a_ref[...], w_ref[...])


def matmul_residual(a, w, r, tm, tn):
    M, K = a.shape
    N = w.shape[1]
    return pl.pallas_call(
        _mm_res_kernel,
        out_shape=jax.ShapeDtypeStruct((M, N), F32),
        grid=(M // tm, N // tn),
        in_specs=[pl.BlockSpec((tm, K), lambda i, j: (i, 0)), pl.BlockSpec((K, tn), lambda i, j: (0, j)),
                  pl.BlockSpec((tm, tn), lambda i, j: (i, j))],
        out_specs=pl.BlockSpec((tm, tn), lambda i, j: (i, j)),
        compiler_params=_cparams("parallel", "arbitrary"),
        name="matmul_residual",
    )(a, w, r)


def _swiglu_kernel(h_ref, wa_ref, wu_ref, o_ref):
    h = h_ref[...]
    a = _dot(h, wa_ref[...])
    u = _dot(h, wu_ref[...])
    o_ref[...] = (a * jax.nn.sigmoid(a) * u).astype(o_ref.dtype)


def swiglu_in(h, w_a, w_u, tm, tn):
    M, K = h.shape
    N = w_a.shape[1]
    return pl.pallas_call(
        _swiglu_kernel,
        out_shape=jax.ShapeDtypeStruct((M, N), MXU_DTYPE),
        grid=(M // tm, N // tn),
        in_specs=[pl.BlockSpec((tm, K), lambda i, j: (i, 0)), pl.BlockSpec((K, tn), lambda i, j: (0, j)),
                  pl.BlockSpec((K, tn), lambda i, j: (0, j))],
        out_specs=pl.BlockSpec((tm, tn), lambda i, j: (i, j)),
        compiler_params=_cparams("parallel", "arbitrary"),
        name="swiglu_in",
    )(h, w_a, w_u)


def _mm_res_acc_kernel(a_ref, w_ref, r_ref, o_ref, acc_ref):
    k = pl.program_id(2)

    @pl.when(k == 0)
    def _():
        acc_ref[...] = r_ref[...]

    acc_ref[...] += _dot(a_ref[...], w_ref[...])

    @pl.when(k == pl.num_programs(2) - 1)
    def _():
        o_ref[...] = acc_ref[...]


def matmul_residual_ksplit(a, w, r, tm, tn, tk):
    M, K = a.shape
    N = w.shape[1]
    return pl.pallas_call(
        _mm_res_acc_kernel,
        out_shape=jax.ShapeDtypeStruct((M, N), F32),
        grid=(M // tm, N // tn, K // tk),
        in_specs=[pl.BlockSpec((tm, tk), lambda i, j, k: (i, k)), pl.BlockSpec((tk, tn), lambda i, j, k: (k, j)),
                  pl.BlockSpec((tm, tn), lambda i, j, k: (i, j))],
        out_specs=pl.BlockSpec((tm, tn), lambda i, j, k: (i, j)),
        scratch_shapes=[pltpu.VMEM((tm, tn), F32)],
        compiler_params=_cparams("parallel", "arbitrary", "arbitrary"),
        name="matmul_residual_ksplit",
    )(a, w, r)


def _dsa_index_kernel(q_ref, small_ref, kd_ref, sc_ref, tau_ref, cmp_ref, *, topk, tq):
    i = pl.program_id(0)
    nk = sc_ref.shape[0]
    q = q_ref[...].astype(MXU_DTYPE)
    w = small_ref[:, OFF_WI:OFF_WI + H_I]
    row = lax.broadcasted_iota(jnp.int32, (tq, tq), 0)
    col = lax.broadcasted_iota(jnp.int32, (tq, tq), 1)

    def score_tile(j, _):
        kd = kd_ref[j]
        acc = jnp.zeros((tq, tq), F32)
        for p in range(H_I // 2):
            r = _dot(q[:, p * 2 * D_I:(p + 1) * 2 * D_I], kd)
            acc = acc + jnp.maximum(r[:, :tq], 0.0) * w[:, 2 * p:2 * p + 1] + jnp.maximum(r[:, tq:], 0.0) * w[:, 2 * p + 1:2 * p + 2]
        sc_ref[j] = acc
        causal = (j * tq + col) <= (i * tq + row)
        cmp_ref[j] = jnp.where(causal, acc, NEG)
        return 0

    lax.fori_loop(0, i + 1, score_tile, 0)

    def zero_tile(j, _):
        sc_ref[j] = jnp.zeros((tq, tq), F32)
        return 0

    lax.fori_loop(i + 1, nk, zero_tile, 0)

    for r0 in range(0, tq, BISECT_ROWS):
        strip = slice(r0, r0 + BISECT_ROWS)

        def bit_body(b, res_u):
            cand_u = res_u | jnp.left_shift(jnp.int32(1), 31 - b)
            cand = jnp.broadcast_to(_key_to_float(cand_u ^ jnp.int32(INT_MIN)), (BISECT_ROWS, LANE))

            def cnt_body(j, c):
                for t in range(tq // LANE):
                    c = c + jnp.where(cmp_ref[j, strip, t * LANE:(t + 1) * LANE] >= cand, 1.0, 0.0)
                return c

            cnt = lax.fori_loop(0, i + 1, cnt_body, jnp.zeros((BISECT_ROWS, LANE), F32))
            total = jnp.sum(cnt, axis=-1, keepdims=True)
            return jnp.where(total >= topk, cand_u, res_u)

        res_u = lax.fori_loop(0, 32, bit_body, jnp.zeros((BISECT_ROWS, 1), jnp.int32))
        tau_ref[strip, :] = jnp.where(res_u == 0, -jnp.inf, _key_to_float(res_u ^ jnp.int32(INT_MIN)))


def dsa_index(z, kd, topk, tq):
    T = z.shape[0]
    nk = kd.shape[0]
    return pl.pallas_call(
        functools.partial(_dsa_index_kernel, topk=topk, tq=tq),
        out_shape=(jax.ShapeDtypeStruct((nk, T, tq), F32), jax.ShapeDtypeStruct((T, 1), F32)),
        grid=(T // tq,),
        in_specs=[pl.BlockSpec((tq, H_I * D_I), lambda i: (i, COL_QI // (H_I * D_I))),
                  pl.BlockSpec((tq, SMALL_W), lambda i: (i, COL_SMALL // SMALL_W)),
                  pl.BlockSpec(kd.shape, lambda i: (0, 0, 0))],
        out_specs=(pl.BlockSpec((nk, tq, tq), lambda i: (0, i, 0)), pl.BlockSpec((tq, 1), lambda i: (i, 0))),
        scratch_shapes=[pltpu.VMEM((nk, tq, tq), F32)],
        compiler_params=_cparams("parallel"),
        name="dsa_index",
    )(z, z, kd)


def _flash_tile(q, k, v, bias, masks, state):
    m, l, acc = state
    s = _dot_nt(q, k)
    if bias is not None:
        s = s + bias
    for mask in masks:
        s = jnp.where(mask, s, NEG)
    m_new = jnp.maximum(m, jnp.max(s, axis=-1, keepdims=True))
    alpha = jnp.exp2(m - m_new)
    p = jnp.exp2(s - m_new)
    l = alpha * l + jnp.sum(p, axis=-1, keepdims=True)
    acc = alpha * acc + _dot(p.astype(MXU_DTYPE), v)
    return m_new, l, acc


def _pattn_kernel(*refs, mode, tq):
    q_refs = refs[0:N_REP]
    k_ref, v_ref, bias_ref = refs[N_REP:N_REP + 3]
    rest = refs[N_REP + 3:]
    if mode == 'dsa':
        sc_ref, tau_ref, o_ref = rest
    elif mode == 'slc':
        sel_ref, e_ref, small_ref, prev_ref, o_ref = rest
    else:
        small_ref, prev_ref, o_ref = rest
    g = pl.program_id(0)
    i = pl.program_id(1)
    q = [(r[...] * (HD ** -0.5 * LOG2E)).astype(MXU_DTYPE) for r in q_refs]
    row = lax.broadcasted_iota(jnp.int32, (tq, tq), 0)
    col = lax.broadcasted_iota(jnp.int32, (tq, tq), 1)
    if mode == 'dsa':
        tau = tau_ref[...]
    if mode == 'slc':
        sel = sel_ref[...].astype(MXU_DTYPE)

    def keep_mask(j):
        if mode == 'dsa':
            return sc_ref[j] >= tau
        return _dot(sel, e_ref[j]) > 0.5

    def process(j, state, kind, width=1):
        start = pl.multiple_of(j * tq, tq)
        kt = k_ref[pl.ds(start, width * tq), :].astype(MXU_DTYPE)
        vt = v_ref[pl.ds(start, width * tq), :].astype(MXU_DTYPE)
        masks = []
        if mode in ('dsa', 'slc'):
            parts = [keep_mask(j + w) for w in range(width)]
            masks.append(parts[0] if width == 1 else jnp.concatenate(parts, axis=1))
        elif kind == 'edge':
            masks.append(row < col)
        if kind == 'diag':
            masks.append(row >= col)
        if kind == 'near+diag':
            wide_row = lax.broadcasted_iota(jnp.int32, (tq, 2 * tq), 0)
            wide_col = lax.broadcasted_iota(jnp.int32, (tq, 2 * tq), 1)
            masks.append(wide_col <= wide_row + tq)
        out = []
        for r in range(N_REP):
            if kind in ('far', 'edge'):
                bias = None
            elif kind == 'near+diag':
                bias = jnp.concatenate([bias_ref[0, r, 1], bias_ref[0, r, 0]], axis=1)
            else:
                bias = bias_ref[0, r, 0 if kind == 'diag' else 1]
            out.append(_flash_tile(q[r], kt, vt, bias, masks, state[r]))
        return tuple(out)

    state = tuple((jnp.full((tq, 1), NEG, F32), jnp.zeros((tq, 1), F32), jnp.zeros((tq, HD), F32)) for _ in range(N_REP))
    if mode == 'win':
        n_win = WINDOW // tq
        state = lax.cond(i >= n_win, lambda s: process(i - n_win, s, 'edge'), lambda s: s, state)
        for d in range(n_win - 1, 1, -1):
            state = lax.cond(i >= d, functools.partial(lambda s, d: process(i - d, s, 'far'), d=d), lambda s: s, state)
    else:
        n_far = jnp.maximum(i - 1, 0)
        n_wide = n_far // FAR_WIDTH
        state = lax.fori_loop(0, n_wide, lambda jj, s: process(jj * FAR_WIDTH, s, 'far', FAR_WIDTH), state)
        done = n_wide * FAR_WIDTH
        width = FAR_WIDTH // 2
        while width >= 1:
            take = ((n_far - done) >= width).astype(jnp.int32)
            state = lax.cond(take == 1, functools.partial(lambda s, d, w: process(d, s, 'far', w), d=done, w=width), lambda s: s, state)
            done = done + take * width
            width //= 2
    if mode == 'win':
        state = lax.cond(i >= 1, lambda s: process(i - 1, s, 'near'), lambda s: s, state)
        state = process(i, state, 'diag')
    else:
        state = lax.cond(i >= 1, lambda s: process(i - 1, s, 'near+diag', 2), lambda s: process(i, s, 'diag'), state)

    if mode != 'dsa':
        branch = 1 if mode == 'slc' else 2
        gates = jax.nn.sigmoid(small_ref[...])
        lane = lax.broadcasted_iota(jnp.int32, gates.shape, 1)
    for r in range(N_REP):
        m, l, acc = state[r]
        o = acc / l
        if mode != 'dsa':
            gcol = OFF_GATE + (g * N_REP + r) * 3 + branch
            o = prev_ref[:, r * HD:(r + 1) * HD] + o * jnp.sum(jnp.where(lane == gcol, gates, 0.0), axis=-1, keepdims=True)
        o_ref[:, r * HD:(r + 1) * HD] = o


def prompt_attention(mode, z, q_col, kv_col, bias_near, tq, extra):
    T = z.shape[0]
    G = G_A
    qb = q_col // HD
    kb = kv_col // HD
    q_specs = [pl.BlockSpec((tq, HD), functools.partial(lambda g, i, r: (i, qb + g * N_REP + r), r=r)) for r in range(N_REP)]
    in_specs = q_specs + [
        pl.BlockSpec((T, HD), lambda g, i: (0, kb + 2 * g)),
        pl.BlockSpec((T, HD), lambda g, i: (0, kb + 2 * g + 1)),
        pl.BlockSpec((1, N_REP, 2, tq, tq), lambda g, i: (g, 0, 0, 0, 0)),
    ]
    args = [z] * N_REP + [z, z, bias_near]
    small_spec = pl.BlockSpec((tq, SMALL_W), lambda g, i: (i, COL_SMALL // SMALL_W))
    prev_spec = pl.BlockSpec((tq, N_REP * HD), lambda g, i: (i, g))
    if mode == 'dsa':
        scores, tau = extra
        nk = scores.shape[0]
        in_specs += [pl.BlockSpec((nk, tq, tq), lambda g, i: (0, i, 0)), pl.BlockSpec((tq, 1), lambda g, i: (i, 0))]
        args += [scores, tau]
    elif mode == 'slc':
        sel, e, prev = extra
        nsb = e.shape[1]
        in_specs += [pl.BlockSpec((tq, nsb), lambda g, i: (i, g)), pl.BlockSpec(e.shape, lambda g, i: (0, 0, 0)), small_spec, prev_spec]
        args += [sel, e, z, prev]
    else:
        (prev,) = extra
        in_specs += [small_spec, prev_spec]
        args += [z, prev]
    return pl.pallas_call(
        functools.partial(_pattn_kernel, mode=mode, tq=tq),
        out_shape=jax.ShapeDtypeStruct((T, G * N_REP * HD), F32),
        grid=(G, T // tq),
        in_specs=in_specs,
        out_specs=pl.BlockSpec((tq, N_REP * HD), lambda g, i: (i, g)),
        compiler_params=_cparams("parallel", "parallel"),
        name="attn_" + mode,
    )(*args)


def _compress_kernel(x_ref, w1_ref, pe_ref, w2_ref, gain_ref, o_ref):
    c = pl.program_id(0) % 2
    nch = x_ref.shape[0]
    w1 = w1_ref[0]
    hid0 = jnp.zeros((nch, HD), F32)
    hid1 = jnp.zeros((nch, HD), F32)
    for s in range(CMP_STRIDE):
        xs = x_ref[:, s, :].astype(MXU_DTYPE)
        hid0 = hid0 + _dot(xs, w1[0, s].astype(MXU_DTYPE))
        hid1 = hid1 + _dot(xs, w1[1, s].astype(MXU_DTYPE))
    pe_term = _dot(pe_ref[0].astype(MXU_DTYPE), w1.reshape(CMP_LEN * HD, HD).astype(MXU_DTYPE))
    hid = pe_term + hid0 + pltpu.roll(hid1, nch - 1, 0)
    out = _dot((hid * jax.nn.sigmoid(hid)).astype(MXU_DTYPE), w2_ref[0].astype(MXU_DTYPE))
    ms = jnp.mean(out * out, axis=-1, keepdims=True)
    normed = (out * lax.rsqrt(ms + EPS)) * gain_ref[...]
    o_ref[0] = jnp.where(c == 0, normed, out)


def compress_blocks(z3, col, cmp_pe, cmp_w1, cmp_w2, gain_c):
    nch = z3.shape[0]
    r = CMP_LEN // CMP_STRIDE
    w1 = cmp_w1.reshape(2, r, CMP_STRIDE, HD, HD)
    pe = cmp_pe.reshape(CMP_LEN, 2, HD).transpose(1, 0, 2).reshape(2, 1, CMP_LEN * HD)
    cb = col // HD
    return pl.pallas_call(
        _compress_kernel,
        out_shape=jax.ShapeDtypeStruct((G_B * 2, nch, HD), F32),
        grid=(G_B * 2,),
        in_specs=[pl.BlockSpec((nch, CMP_STRIDE, HD), lambda gc: (0, 0, cb + gc)),
                  pl.BlockSpec((1, r, CMP_STRIDE, HD, HD), lambda gc: (gc % 2, 0, 0, 0, 0)),
                  pl.BlockSpec((1, 1, CMP_LEN * HD), lambda gc: (gc % 2, 0, 0)),
                  pl.BlockSpec((1, HD, HD), lambda gc: (gc % 2, 0, 0)),
                  pl.BlockSpec((1, HD), lambda gc: (0, 0))],
        out_specs=pl.BlockSpec((1, nch, HD), lambda gc: (gc, 0, 0)),
        compiler_params=_cparams("parallel"),
        name="nsa_compress",
    )(z3, w1, pe, cmp_w2, gain_c.reshape(1, HD))


def _nsa_cmp_kernel(*refs, tq, n_sel):
    q_refs = refs[0:N_REP]
    kc_ref, vc_ref, map_ref, small_ref, oc_ref, sel_ref = refs[N_REP:]
    g = pl.program_id(0)
    i = pl.program_id(1)
    kc = kc_ref[0].astype(MXU_DTYPE)
    vc = vc_ref[0].astype(MXU_DTYPE)
    smap_t = map_ref[...]
    nch = kc.shape[0]
    nsb = smap_t.shape[0]
    t = i * tq + lax.broadcasted_iota(jnp.int32, (tq, nch), 0)
    n = lax.broadcasted_iota(jnp.int32, (tq, nch), 1)
    valid = (n * CMP_STRIDE + CMP_LEN - 1) <= t
    gates = jax.nn.sigmoid(small_ref[...])
    lane = lax.broadcasted_iota(jnp.int32, gates.shape, 1)
    imp = jnp.zeros((nsb, tq), F32)
    for r in range(N_REP):
        q = q_refs[r][...].astype(MXU_DTYPE)
        lm = jnp.where(valid, _dot_nt(q, kc) * (HD ** -0.5), NEG)
        m = jnp.max(lm, axis=-1, keepdims=True)
        p = jnp.where(valid, jnp.exp(lm - m), 0.0)
        den = jnp.sum(p, axis=-1, keepdims=True)
        pc = (p / jnp.where(den > 0.0, den, 1.0)).astype(MXU_DTYPE)
        imp = imp + _dot_nt(smap_t, pc)
        gcol = OFF_GATE + (g * N_REP + r) * 3
        gate = jnp.sum(jnp.where(lane == gcol, gates, 0.0), axis=-1, keepdims=True)
        oc_ref[:, r * HD:(r + 1) * HD] = _dot(pc, vc) * gate

    tpos = i * tq + lax.broadcasted_iota(jnp.int32, (nsb, tq), 1)
    blk = lax.broadcasted_iota(jnp.int32, (nsb, tq), 0)
    cur = tpos // SEL_LEN
    start_ok = blk <= cur
    forced_or_imp = jnp.where(blk == 0, BIG, jnp.where(blk == cur, BIG, jnp.where(blk == cur - 1, BIG, imp)))
    score = jnp.where(start_ok, forced_or_imp, NEG)
    blk_f = blk.astype(F32)
    sel = jnp.zeros((nsb, tq), F32)
    for _ in range(n_sel):
        best = jnp.max(score, axis=0, keepdims=True)
        first = jnp.min(jnp.where(score == best, blk_f, float(nsb)), axis=0, keepdims=True)
        hit = blk_f == first
        sel = jnp.where(hit, 1.0, sel)
        score = jnp.where(hit, -jnp.inf, score)
    sel_ref[...] = sel.T


def nsa_cmp_attention(z, q_col, kcv, sel_map, tq):
    T = z.shape[0]
    nch = kcv.shape[1]
    nsb = sel_map.shape[1]
    qb = q_col // HD
    q_specs = [pl.BlockSpec((tq, HD), functools.partial(lambda g, i, r: (i, qb + g * N_REP + r), r=r)) for r in range(N_REP)]
    return pl.pallas_call(
        functools.partial(_nsa_cmp_kernel, tq=tq, n_sel=min(N_SEL, nsb)),
        out_shape=(jax.ShapeDtypeStruct((T, H_B * HD), F32), jax.ShapeDtypeStruct((T, G_B * nsb), F32)),
        grid=(G_B, T // tq),
        in_specs=q_specs + [pl.BlockSpec((1, nch, HD), lambda g, i: (2 * g, 0, 0)),
                            pl.BlockSpec((1, nch, HD), lambda g, i: (2 * g + 1, 0, 0)),
                            pl.BlockSpec((nsb, nch), lambda g, i: (0, 0)),
                            pl.BlockSpec((tq, SMALL_W), lambda g, i: (i, COL_SMALL // SMALL_W))],
        out_specs=(pl.BlockSpec((tq, N_REP * HD), lambda g, i: (i, g)), pl.BlockSpec((tq, nsb), lambda g, i: (i, g))),
        compiler_params=_cparams("parallel", "parallel"),
        name="nsa_cmp",
    )(*([z] * N_REP), kcv, kcv, sel_map.T, z)


def _mem_attn_kernel(q_ref, k_ref, v_ref, o_ref):
    q = q_ref[0].astype(MXU_DTYPE)
    s = _dot_nt(q, k_ref[0].astype(MXU_DTYPE)) * (HD_M ** -0.5)
    p = jnp.exp(s - jnp.max(s, axis=-1, keepdims=True))
    p = p / jnp.sum(p, axis=-1, keepdims=True)
    o_ref[0] = _dot(p.astype(MXU_DTYPE), v_ref[0].astype(MXU_DTYPE))


def mem_attention(z3, q_col, mkv3, tq):
    B, Tq, _ = z3.shape
    n_mem = mkv3.shape[1]
    qb = q_col // HD_M
    return pl.pallas_call(
        _mem_attn_kernel,
        out_shape=jax.ShapeDtypeStruct((B, Tq, H_M * HD_M), F32),
        grid=(B, H_M, Tq // tq),
        in_specs=[pl.BlockSpec((1, tq, HD_M), lambda b, h, i: (b, i, qb + h)),
                  pl.BlockSpec((1, n_mem, HD_M), lambda b, h, i: (b, 0, 2 * h)),
                  pl.BlockSpec((1, n_mem, HD_M), lambda b, h, i: (b, 0, 2 * h + 1))],
        out_specs=pl.BlockSpec((1, tq, HD_M), lambda b, h, i: (b, i, h)),
        compiler_params=_cparams("parallel", "parallel", "arbitrary"),
        name="mem_attn",
    )(z3, mkv3, mkv3)


TOK_PAD = 8
NEW_PAD = 128


def _start_pages(pt_ref, b, n_pages, copies):
    def issue(p, _):
        for cp in copies(p, pt_ref[b, p]):
            cp.start()
        return 0

    lax.fori_loop(0, n_pages, issue, 0)


def _wait_pages(pt_ref, b, n_pages, copies):
    def wait(p, _):
        for cp in copies(p, pt_ref[b, p]):
            cp.wait()
        return 0

    lax.fori_loop(0, n_pages, wait, 0)


def _prefetched_pages(pt_ref, n_pages, copies):
    b, g = pl.program_id(0), pl.program_id(1)
    n_groups = pl.num_programs(1)
    n = b * n_groups + g
    slot = n % 2

    @pl.when(n == 0)
    def _():
        _start_pages(pt_ref, b, n_pages, functools.partial(copies, 0, g))

    @pl.when(n + 1 < pl.num_programs(0) * n_groups)
    def _():
        _start_pages(pt_ref, (n + 1) // n_groups, n_pages, functools.partial(copies, 1 - slot, (n + 1) % n_groups))

    _wait_pages(pt_ref, b, n_pages, functools.partial(copies, slot, g))
    return slot


def _page_rows(pool_ref, page, row):
    return pool_ref.at[pl.ds(page * PAGE_SIZE, PAGE_SIZE), row, :]


def _sample_index_kernel(pt_ref, q_ref, w_ref, knew_ref, pool_ref, mp_ref, mn_ref, kbuf, sc_ref, sem, *, topk, n_pages, past_len, chunk):
    b = pl.program_id(0)
    copies = lambda p, page: [pltpu.make_async_copy(pool_ref.at[page], kbuf.at[pl.ds(p * PAGE_SIZE, PAGE_SIZE)], sem)]
    _start_pages(pt_ref, b, n_pages, copies)
    _wait_pages(pt_ref, b, n_pages, copies)
    q = q_ref[0].astype(MXU_DTYPE)
    w = w_ref[0]

    def head_sum(keys):
        x = jnp.maximum(_dot_nt(q, keys.astype(MXU_DTYPE)), 0.0) * w
        return jnp.sum(x.reshape(TOK_PAD, H_I, x.shape[1]), axis=1)

    for c in range(past_len // chunk):
        sc_ref[:, c * chunk:(c + 1) * chunk] = head_sum(kbuf[c * chunk:(c + 1) * chunk, :])
    tok = lax.broadcasted_iota(jnp.int32, (TOK_PAD, NEW_PAD), 0)
    new = lax.broadcasted_iota(jnp.int32, (TOK_PAD, NEW_PAD), 1)
    sc_ref[:, past_len:past_len + NEW_PAD] = jnp.where(new <= tok, head_sum(knew_ref[0]), NEG)
    n_lane_tiles = (past_len + NEW_PAD) // LANE

    def bit_body(i, res_u):
        cand_u = res_u | jnp.left_shift(jnp.int32(1), 31 - i)
        cand = jnp.broadcast_to(_key_to_float(cand_u ^ jnp.int32(INT_MIN)), (TOK_PAD, LANE))
        part = jnp.zeros((TOK_PAD, LANE), F32)
        for t in range(n_lane_tiles):
            part = part + jnp.where(sc_ref[:, t * LANE:(t + 1) * LANE] >= cand, 1.0, 0.0)
        cnt = jnp.sum(part, axis=-1, keepdims=True)
        return jnp.where(cnt >= topk, cand_u, res_u)

    res_u = lax.fori_loop(0, 32, bit_body, jnp.zeros((TOK_PAD, 1), jnp.int32))
    tau = jnp.where(res_u == 0, -jnp.inf, _key_to_float(res_u ^ jnp.int32(INT_MIN)))
    mp_ref[0, 0] = jnp.where(sc_ref[:, 0:past_len] >= tau, 1.0, 0.0)
    mn_ref[0, 0] = jnp.where(new <= tok, jnp.where(sc_ref[:, past_len:past_len + NEW_PAD] >= tau, 1.0, 0.0), 0.0)


def sample_index(page_table, q, w, k_new, pool, topk):
    DB, n_pages = page_table.shape
    past_len = n_pages * PAGE_SIZE
    chunk = min(1024, past_len)
    return pl.pallas_call(
        functools.partial(_sample_index_kernel, topk=topk, n_pages=n_pages, past_len=past_len, chunk=chunk),
        out_shape=(jax.ShapeDtypeStruct((DB, 1, TOK_PAD, past_len), F32), jax.ShapeDtypeStruct((DB, 1, TOK_PAD, NEW_PAD), F32)),
        grid_spec=pltpu.PrefetchScalarGridSpec(
            num_scalar_prefetch=1,
            grid=(DB,),
            in_specs=[pl.BlockSpec((1, TOK_PAD * H_I, D_I), lambda b, pt: (b, 0, 0)),
                      pl.BlockSpec((1, TOK_PAD * H_I, 1), lambda b, pt: (b, 0, 0)),
                      pl.BlockSpec((1, NEW_PAD, D_I), lambda b, pt: (b, 0, 0)),
                      pl.BlockSpec(memory_space=pl.ANY)],
            out_specs=(pl.BlockSpec((1, 1, TOK_PAD, past_len), lambda b, pt: (b, 0, 0, 0)),
                       pl.BlockSpec((1, 1, TOK_PAD, NEW_PAD), lambda b, pt: (b, 0, 0, 0))),
            scratch_shapes=[pltpu.VMEM((past_len, D_I), F32), pltpu.VMEM((TOK_PAD, past_len + NEW_PAD), F32), pltpu.SemaphoreType.DMA(())],
        ),
        compiler_params=_cparams("arbitrary"),
        name="sample_index",
    )(page_table, q, w, k_new, pool)


def _sample_attend_kernel(pt_ref, q_ref, kvn_ref, mp_ref, mn_ref, tp_ref, tn_ref, *rest, gated, n_pages, past_len, chunk):
    if gated:
        gate_ref, prev_ref, pool_ref, o_ref, kbuf, vbuf, sem = rest
    else:
        pool_ref, o_ref, kbuf, vbuf, sem = rest
    def copies(slot, g, p, page):
        dst = lambda buf: buf.at[slot, pl.ds(p * PAGE_SIZE, PAGE_SIZE)]
        return [pltpu.make_async_copy(_page_rows(pool_ref, page, 2 * g), dst(kbuf), sem.at[slot]),
                pltpu.make_async_copy(_page_rows(pool_ref, page, 2 * g + 1), dst(vbuf), sem.at[slot])]

    slot = _prefetched_pages(pt_ref, n_pages, copies)
    q = q_ref[0, 0].astype(MXU_DTYPE)

    def logits(keys, bias_ref, mask_ref, sl):
        s = _dot_nt(q, keys.astype(MXU_DTYPE)) * (HD ** -0.5)
        bias = jnp.concatenate([bias_ref[r, :, sl] for r in range(N_REP)], axis=0)
        keep = mask_ref[0, 0, :, sl] > 0.5
        return jnp.where(jnp.concatenate([keep] * N_REP, axis=0), s + bias, NEG)

    n_chunks = past_len // chunk
    s_past = [logits(kbuf[slot, c * chunk:(c + 1) * chunk, :], tp_ref, mp_ref, slice(c * chunk, (c + 1) * chunk)) for c in range(n_chunks)]
    s_new = logits(kvn_ref[0, 0, 0], tn_ref, mn_ref, slice(0, NEW_PAD))
    m = jnp.max(s_new, axis=-1, keepdims=True)
    for s in s_past:
        m = jnp.maximum(m, jnp.max(s, axis=-1, keepdims=True))
    p_new = jnp.exp(s_new - m)
    l = jnp.sum(p_new, axis=-1, keepdims=True)
    acc = _dot(p_new.astype(MXU_DTYPE), kvn_ref[0, 0, 1].astype(MXU_DTYPE))
    for c, s in enumerate(s_past):
        p = jnp.exp(s - m)
        l = l + jnp.sum(p, axis=-1, keepdims=True)
        acc = acc + _dot(p.astype(MXU_DTYPE), vbuf[slot, c * chunk:(c + 1) * chunk, :].astype(MXU_DTYPE))
    o = acc / l
    if gated:
        o = prev_ref[0, 0] + o * jax.nn.sigmoid(gate_ref[0, 0])
    o_ref[0, 0] = o


def sample_attend(page_table, q, kv_new, mask_past, mask_new, tab_past, tab_new, pool, gate_prev=None):
    DB, n_pages = page_table.shape
    G = q.shape[1]
    past_len = n_pages * PAGE_SIZE
    chunk = min(1024, past_len)
    rows = N_REP * TOK_PAD
    mb, mg = mask_past.shape[0] > 1, mask_past.shape[1] > 1
    mask_map = lambda b, g, pt: (b if mb else 0, g if mg else 0, 0, 0)
    row_spec = lambda n: pl.BlockSpec((1, 1, rows, n), lambda b, g, pt: (b, g, 0, 0))
    in_specs = [row_spec(HD),
                pl.BlockSpec((1, 1, 2, NEW_PAD, HD), lambda b, g, pt: (b, g, 0, 0, 0)),
                pl.BlockSpec((1, 1, TOK_PAD, past_len), mask_map),
                pl.BlockSpec((1, 1, TOK_PAD, NEW_PAD), mask_map),
                pl.BlockSpec((N_REP, TOK_PAD, past_len), lambda b, g, pt: (g, 0, 0)),
                pl.BlockSpec((N_REP, TOK_PAD, NEW_PAD), lambda b, g, pt: (g, 0, 0))]
    args = [q, kv_new, mask_past, mask_new, tab_past, tab_new]
    if gate_prev is not None:
        in_specs += [row_spec(1), row_spec(HD)]
        args += list(gate_prev)
    in_specs.append(pl.BlockSpec(memory_space=pl.ANY))
    args.append(pool)
    return pl.pallas_call(
        functools.partial(_sample_attend_kernel, gated=gate_prev is not None, n_pages=n_pages, past_len=past_len, chunk=chunk),
        out_shape=jax.ShapeDtypeStruct((DB, G, rows, HD), F32),
        grid_spec=pltpu.PrefetchScalarGridSpec(
            num_scalar_prefetch=1,
            grid=(DB, G),
            in_specs=in_specs,
            out_specs=row_spec(HD),
            scratch_shapes=[pltpu.VMEM((2, past_len, HD), F32), pltpu.VMEM((2, past_len, HD), F32), pltpu.SemaphoreType.DMA((2,))],
        ),
        compiler_params=_cparams("arbitrary", "arbitrary"),
        name="sample_attend",
    )(page_table, *args)


def _sample_compress_kernel(pt_ref, cnew_ref, w1_ref, pe_ref, w2_ref, gain_ref, pool_ref, o_ref, cbuf, sem, *, n_pages, n_blocks):
    def copies(slot, g, p, page):
        return [pltpu.make_async_copy(_page_rows(pool_ref, page, 2 * g + c), cbuf.at[slot, c, pl.ds(p * PAGE_SIZE, PAGE_SIZE)], sem.at[slot])
                for c in range(2)]

    slot = _prefetched_pages(pt_ref, n_pages, copies)
    n_cached = n_pages * PAGE_SIZE
    n_tok = cbuf.shape[2]
    for c in range(2):
        cbuf[slot, c, n_cached:n_cached + CMP_STRIDE] = cnew_ref[0, 0, c]
        cbuf[slot, c, n_cached + CMP_STRIDE:n_tok] = jnp.zeros((n_tok - n_cached - CMP_STRIDE, HD), F32)

    nch = n_tok // CMP_STRIDE
    for c in range(2):
        acc = jnp.zeros((nch + 8, 2 * HD), F32)
        for sp in range(CMP_STRIDE // 2):
            rows = [cbuf.at[slot, c][pl.ds(2 * sp + d, nch, stride=CMP_STRIDE), :] for d in range(2)]
            lhs = jnp.concatenate([jnp.concatenate(rows, axis=1), pe_ref[c, sp]], axis=0).astype(MXU_DTYPE)
            acc = acc + _dot(lhs, w1_ref[c, sp])
        pe_term = acc[nch:nch + 1, 0:HD] + acc[nch + 1:nch + 2, HD:2 * HD]
        hid = pe_term + acc[0:nch, 0:HD] + pltpu.roll(acc[0:nch, HD:2 * HD], nch - 1, 0)
        out = _dot((hid * jax.nn.sigmoid(hid)).astype(MXU_DTYPE), w2_ref[c])[0:n_blocks]
        if c == 0:
            ms = jnp.mean(out * out, axis=-1, keepdims=True)
            out = (out * lax.rsqrt(ms + EPS)) * gain_ref[...]
        o_ref[0, c] = out


def sample_compress(page_table, c_new, pool, cmp_pe, cmp_w1, cmp_w2, gain_c):
    DB, n_pages = page_table.shape
    n_blocks = n_pages * (PAGE_SIZE // CMP_STRIDE)
    n_slots = _round_up(n_blocks + 1, 8)
    r = CMP_LEN // CMP_STRIDE
    assert r == 2
    half = CMP_STRIDE // 2
    w1 = cmp_w1.reshape(2, r, half, 2, HD, HD).transpose(0, 2, 3, 4, 1, 5).reshape(2, half, 2 * HD, r * HD).astype(MXU_DTYPE)
    pe = cmp_pe.reshape(r, half, 2, 2, HD).transpose(3, 1, 0, 2, 4).reshape(2, half, r, 2 * HD)
    pe = jnp.pad(pe, ((0, 0), (0, 0), (0, 8 - r), (0, 0)))
    full = lambda a: pl.BlockSpec(a.shape, lambda b, g, pt: (0,) * a.ndim)
    w2 = cmp_w2.astype(MXU_DTYPE)
    gain = gain_c.reshape(1, HD)
    return pl.pallas_call(
        functools.partial(_sample_compress_kernel, n_pages=n_pages, n_blocks=n_blocks),
        out_shape=jax.ShapeDtypeStruct((DB, G_B * 2, n_blocks, HD), F32),
        grid_spec=pltpu.PrefetchScalarGridSpec(
            num_scalar_prefetch=1,
            grid=(DB, G_B),
            in_specs=[pl.BlockSpec((1, 1, 2, CMP_STRIDE, HD), lambda b, g, pt: (b, g, 0, 0, 0)),
                      full(w1), full(pe), full(w2), full(gain), pl.BlockSpec(memory_space=pl.ANY)],
            out_specs=pl.BlockSpec((1, 2, n_blocks, HD), lambda b, g, pt: (b, g, 0, 0)),
            scratch_shapes=[pltpu.VMEM((2, 2, n_slots * CMP_STRIDE, HD), F32), pltpu.SemaphoreType.DMA((2,))],
        ),
        compiler_params=_cparams("arbitrary", "arbitrary"),
        name="sample_compress",
    )(page_table, c_new, w1, pe, w2, gain, pool)


def _sample_cmp_kernel(q_ref, kc_ref, vc_ref, map_ref, e_ref, gate_ref, oc_ref, mp_ref, mn_ref, *, past_len, n_real, n_sel):
    q = q_ref[0, 0].astype(MXU_DTYPE)
    kc = kc_ref[0, 0].astype(MXU_DTYPE)
    vc = vc_ref[0, 0].astype(MXU_DTYPE)
    rows = q.shape[0]
    nch = kc.shape[0]
    t = past_len + (lax.broadcasted_iota(jnp.int32, (rows, nch), 0) & (TOK_PAD - 1))
    n = lax.broadcasted_iota(jnp.int32, (rows, nch), 1)
    valid = (n * CMP_STRIDE + CMP_LEN - 1) <= t
    lm = jnp.where(valid, _dot_nt(q, kc) * (HD ** -0.5), NEG)
    m = jnp.max(lm, axis=-1, keepdims=True)
    p = jnp.where(valid, jnp.exp(lm - m), 0.0)
    den = jnp.sum(p, axis=-1, keepdims=True)
    pc = (p / jnp.where(den > 0.0, den, 1.0)).astype(MXU_DTYPE)
    oc_ref[0, 0] = _dot(pc, vc) * jax.nn.sigmoid(gate_ref[0, 0])
    smap_t = map_ref[...]
    nsb = smap_t.shape[0]
    pc_lanes = jnp.concatenate([pc, jnp.zeros((LANE - rows, nch), MXU_DTYPE)], axis=0)
    imp_t = _dot_nt(smap_t, pc_lanes)
    imp = imp_t
    for r in range(1, N_REP):
        imp = imp + pltpu.roll(imp_t, LANE - r * TOK_PAD, 1)
    tpos = past_len + lax.broadcasted_iota(jnp.int32, (nsb, LANE), 1)
    blk = lax.broadcasted_iota(jnp.int32, (nsb, LANE), 0)
    cur = tpos // SEL_LEN
    forced_or_imp = jnp.where(blk == 0, BIG, jnp.where(blk == cur, BIG, jnp.where(blk == cur - 1, BIG, imp)))
    score = jnp.where(blk <= cur, forced_or_imp, jnp.where(blk < n_real, NEG, -jnp.inf))
    blk_f = blk.astype(F32)
    sel_t = jnp.zeros((nsb, LANE), F32)
    for _ in range(n_sel):
        best = jnp.max(score, axis=0, keepdims=True)
        first = jnp.min(jnp.where(score == best, blk_f, float(nsb)), axis=0, keepdims=True)
        hit = blk_f == first
        sel_t = jnp.where(hit, 1.0, sel_t)
        score = jnp.where(hit, -jnp.inf, score)
    sel = sel_t.T[0:TOK_PAD]
    keep = _dot(sel.astype(MXU_DTYPE), e_ref[...])
    mp_ref[0, 0] = jnp.where(keep[:, 0:past_len] > 0.5, 1.0, 0.0)
    tok = lax.broadcasted_iota(jnp.int32, (TOK_PAD, NEW_PAD), 0)
    new = lax.broadcasted_iota(jnp.int32, (TOK_PAD, NEW_PAD), 1)
    mn_ref[0, 0] = jnp.where(new <= tok, jnp.where(keep[:, past_len:] > 0.5, 1.0, 0.0), 0.0)


def sample_cmp_attention(q, kcv, gate, past_len, n_tokens):
    DB, G, rows, _ = q.shape
    nc = kcv.shape[2]
    n_real = -(-(past_len + n_tokens) // SEL_LEN)
    nsb = _round_up(n_real, LANE)
    sel_map = cmp_to_sel(nc, nsb).astype(MXU_DTYPE).T
    key = jnp.arange(past_len + NEW_PAD)[None, :]
    expand = ((key // SEL_LEN == jnp.arange(nsb)[:, None]) & (key < past_len + n_tokens)).astype(MXU_DTYPE)
    row_spec = lambda n: pl.BlockSpec((1, 1, rows, n), lambda b, g: (b, g, 0, 0))
    return pl.pallas_call(
        functools.partial(_sample_cmp_kernel, past_len=past_len, n_real=n_real, n_sel=min(N_SEL, n_real)),
        out_shape=(jax.ShapeDtypeStruct((DB, G, rows, HD), F32), jax.ShapeDtypeStruct((DB, G, TOK_PAD, past_len), F32),
                   jax.ShapeDtypeStruct((DB, G, TOK_PAD, NEW_PAD), F32)),
        grid=(DB, G),
        in_specs=[row_spec(HD),
                  pl.BlockSpec((1, 1, nc, HD), lambda b, g: (b, 2 * g, 0, 0)),
                  pl.BlockSpec((1, 1, nc, HD), lambda b, g: (b, 2 * g + 1, 0, 0)),
                  pl.BlockSpec(sel_map.shape, lambda b, g: (0, 0)),
                  pl.BlockSpec(expand.shape, lambda b, g: (0, 0)),
                  row_spec(1)],
        out_specs=(row_spec(HD), pl.BlockSpec((1, 1, TOK_PAD, past_len), lambda b, g: (b, g, 0, 0)),
                   pl.BlockSpec((1, 1, TOK_PAD, NEW_PAD), lambda b, g: (b, g, 0, 0))),
        compiler_params=_cparams("parallel", "parallel"),
        name="sample_cmp",
    )(q, kcv, kcv, sel_map, expand, gate)


def t5_bucket(dist):
    max_exact = N_BUCKETS // 2
    n = jnp.maximum(dist, 0)
    nf = jnp.maximum(n, 1).astype(jnp.float32)
    large = max_exact + (jnp.log(nf / max_exact) / math.log(MAX_DIST / max_exact) * (N_BUCKETS - max_exact)).astype(jnp.int32)
    return jnp.where(n < max_exact, n, jnp.minimum(large, N_BUCKETS - 1))


def cmp_to_sel(n_cmp, n_sel_blocks):
    cs = jnp.arange(n_cmp)[:, None] * CMP_STRIDE
    ss = jnp.arange(n_sel_blocks)[None, :] * SEL_LEN
    return ((cs < ss + SEL_LEN) & (cs + CMP_LEN > ss)).astype(jnp.float32)


def toeplitz_bias_tiles(bias_tab, tq):
    max_exact = N_BUCKETS // 2
    nearest_far = np.float32(tq + 1)
    assert max_exact + int(np.log(nearest_far / max_exact) / math.log(MAX_DIST / max_exact) * (N_BUCKETS - max_exact)) >= N_BUCKETS - 1
    rel = (bias_tab - bias_tab[N_BUCKETS - 1]) * LOG2E
    ii = jnp.arange(tq)[:, None]
    jj = jnp.arange(tq)[None, :]
    bucket = jnp.stack([t5_bucket(ii - jj), t5_bucket(tq + ii - jj)])
    onehot = (bucket[None] == jnp.arange(N_BUCKETS)[:, None, None, None]).astype(F32)
    tiles = jnp.einsum('bh,bctk->hctk', rel, onehot, precision=lax.Precision.HIGHEST)
    return tiles.reshape(G_A, N_REP, 2, tq, tq)


def block_expand_matrix(n_blocks, n_tiles, tk):
    b = jnp.arange(n_blocks)[None, :, None]
    key = (jnp.arange(n_tiles)[:, None, None] * tk + jnp.arange(tk)[None, None, :])
    return (key // SEL_LEN == b).astype(MXU_DTYPE)


def indexer_key_tiles(ki, tk):
    S = ki.shape[0]
    kt = ki.reshape(S // tk, tk, D_I).transpose(0, 2, 1).astype(MXU_DTYPE)
    zero = jnp.zeros_like(kt)
    return jnp.concatenate([jnp.concatenate([kt, zero], axis=2), jnp.concatenate([zero, kt], axis=2)], axis=1)


def _pad_cols(w, n):
    return jnp.pad(w, ((0, 0), (0, n - w.shape[1])))


def _seg(w, idx):
    return w[:, PROJ_OFFSETS[idx]:PROJ_OFFSETS[idx + 1]]


def prepare_weights(w_in, w_mem_kv, w_up_a, w_up_b, w_up_m, w_out, w_ffn_in, w_ffn_out):
    bf = MXU_DTYPE
    d_ff_pad = _round_up(D_FF, COL_TILE)
    order = (2, 0, 1, 5, 6, 7, 8, 10, 3, 4, 9)
    main = jnp.concatenate([_seg(w_in, idx).astype(bf) for idx in order] + [jnp.zeros((w_in.shape[0], N_MAIN - D_MAIN), bf)], axis=1)
    return dict(
        w_main=main,
        w_mg=_seg(w_in, 11).astype(bf),
        w_mem_kv=w_mem_kv.astype(bf),
        w_up_a=w_up_a.astype(bf), w_up_b=w_up_b.astype(bf), w_up_m=w_up_m.astype(bf),
        w_out=w_out.astype(bf),
        w_ffn_a=_pad_cols(w_ffn_in[:, :D_FF], d_ff_pad).astype(bf),
        w_ffn_u=_pad_cols(w_ffn_in[:, D_FF:], d_ff_pad).astype(bf),
        w_ffn_out=jnp.pad(w_ffn_out, ((0, d_ff_pad - D_FF), (0, 0))).astype(bf),
    )


def main_norm_vectors(qk_gain_a, qk_gain_b, qk_gain_m):
    one, zero = jnp.ones((HD,), F32), jnp.zeros((HD,), F32)

    def kv(gk):
        return jnp.tile(jnp.concatenate([gk, zero]), G_A), jnp.tile(jnp.concatenate([one, zero]), G_A)

    kva, fa = kv(qk_gain_a[1])
    slc, fs = kv(qk_gain_b[2])
    win, fw = kv(qk_gain_b[3])
    z = lambda n: jnp.zeros((n,), F32)
    gain = jnp.concatenate([z(H_I * D_I), jnp.tile(qk_gain_a[0], H_A), kva, jnp.tile(qk_gain_b[0], H_B), z(G_B * 2 * HD), slc, win,
                            jnp.tile(qk_gain_m[0], H_M), z(N_MAIN - COL_SMALL)])
    m128 = jnp.concatenate([z(H_I * D_I), jnp.ones((H_A * HD,), F32), fa, jnp.ones((H_B * HD,), F32), z(G_B * 2 * HD), fs, fw,
                            z(H_M * HD_M), z(N_MAIN - COL_SMALL)])
    m256 = jnp.concatenate([z(COL_QM), jnp.ones((H_M * HD_M,), F32), z(N_MAIN - COL_SMALL)])
    cols = np.arange(N_MAIN)
    has = ((cols >= COL_QA) & (cols < COL_CMP)) | ((cols >= COL_SLC) & (cols < COL_SMALL))
    tile_mode = jnp.asarray(has.reshape(-1, COL_TILE).any(axis=1).astype(np.int32))
    return gain.reshape(1, -1), m128.reshape(1, -1), m256.reshape(1, -1), tile_mode


def project_in(x2d, g_attn, W, norm_vecs, tm):
    h = rms_cast(x2d, g_attn, min(tm, 512))
    gain, m128, m256, tile_mode = norm_vecs
    z_main = matmul_headnorm(h, W['w_main'], gain, m128, m256, tile_mode, tm, COL_TILE, "in_proj_main")
    mg = matmul(h, W['w_mg'], tm, COL_TILE, name="in_proj_mg")
    return z_main, mg


def merge_ffn(x2d, o_a, o_b, o_m, mg, W, g_ffn, tm):
    mix = upmix(o_a, o_b, o_m, W['w_up_a'], W['w_up_b'], W['w_up_m'], mg, min(tm, 512), COL_TILE)
    x2 = matmul_residual(mix, W['w_out'], x2d, tm, COL_TILE)
    h2 = rms_cast(x2, g_ffn, min(tm, 512))
    act = swiglu_in(h2, W['w_ffn_a'], W['w_ffn_u'], tm, COL_TILE)
    tk = act.shape[1] // 4
    return matmul_residual_ksplit(act, W['w_ffn_out'], x2, tm, COL_TILE, tk)


def prompt_mixers(z, mkv, rel_bias, cmp_pe, cmp_w1, cmp_w2, gain_c, tq):
    T = z.shape[0]
    nk = T // tq
    bias_a, bias_b = rel_bias[:, :H_A], rel_bias[:, H_A:]
    near_a, near_b = toeplitz_bias_tiles(bias_a, tq), toeplitz_bias_tiles(bias_b, tq)
    ki = z[:, COL_SMALL + OFF_KI:COL_SMALL + OFF_KI + D_I]
    scores, tau = dsa_index(z, indexer_key_tiles(ki, tq), min(DSA_TOPK, T // 4), tq)
    o_a = prompt_attention('dsa', z, COL_QA, COL_KVA, near_a, tq, (scores, tau))
    nch = T // CMP_STRIDE
    nsb = T // SEL_LEN
    kcv = compress_blocks(z.reshape(nch, CMP_STRIDE, z.shape[1]), COL_CMP, cmp_pe, cmp_w1, cmp_w2, gain_c)
    sel_map = cmp_to_sel(nch, nsb).astype(MXU_DTYPE)
    oc, sel = nsa_cmp_attention(z, COL_QB, kcv, sel_map, tq)
    ocs = prompt_attention('slc', z, COL_QB, COL_SLC, near_b, tq, (sel, block_expand_matrix(nsb, nk, tq), oc))
    o_b = prompt_attention('win', z, COL_QB, COL_WIN, near_b, tq, (ocs,))
    o_m = mem_attention(z[None], COL_QM, mkv[None], tq)[0]
    return o_a, o_b, o_m


def sample_mixers(zs, page_table, pool_a_kv, pool_a_idx, pool_b_cmp, pool_b_slc, cache_win, cache_mem, rel_bias, cmp_pe, cmp_w1, cmp_w2, gain_c, DS):
    DB, n_pages = page_table.shape
    past_len = n_pages * PAGE_SIZE
    bias_a, bias_b = rel_bias[:, :H_A], rel_bias[:, H_A:]
    seg = lambda c, n: zs[:, c:c + n].reshape(DB, DS, n)
    pad_tok = lambda a: jnp.pad(a, ((0, 0), (0, TOK_PAD - DS)) + ((0, 0),) * (a.ndim - 2))
    rows = N_REP * TOK_PAD

    def q_rows(c):
        q = pad_tok(seg(c, H_A * HD)).reshape(DB, TOK_PAD, G_A, N_REP, HD)
        return q.transpose(0, 2, 3, 1, 4).reshape(DB, G_A, rows, HD)

    def kv_new(c):
        kv = seg(c, G_A * 2 * HD).reshape(DB, DS, G_A, 2, HD).transpose(0, 2, 3, 1, 4)
        return jnp.pad(kv, ((0, 0), (0, 0), (0, 0), (0, NEW_PAD - DS), (0, 0)))

    tok = jnp.arange(TOK_PAD)[:, None]
    new = jnp.arange(NEW_PAD)[None, :]

    def bias_tables(bias_tab, span):
        near = min(span, NEW_PAD)
        assert near == span or near + 1 >= MAX_DIST
        dist_near = near + tok - jnp.arange(near)[None, :]
        tab_near = bias_tab[t5_bucket(dist_near)].transpose(2, 0, 1)
        tab_far = jnp.broadcast_to(bias_tab[N_BUCKETS - 1][:, None, None], (bias_tab.shape[1], TOK_PAD, span - near))
        return jnp.concatenate([tab_far, tab_near], axis=2), bias_tab[t5_bucket(tok - new)].transpose(2, 0, 1)

    unrows = lambda o: o.reshape(DB, G_A, N_REP, TOK_PAD, HD)[:, :, :, :DS].transpose(0, 3, 1, 2, 4).reshape(DB * DS, H_A * HD)

    qi = pad_tok(seg(COL_QI, H_I * D_I)).reshape(DB, TOK_PAD * H_I, D_I)
    wi = pad_tok(seg(COL_SMALL + OFF_WI, H_I)).reshape(DB, TOK_PAD * H_I, 1)
    ki_new = jnp.pad(seg(COL_SMALL + OFF_KI, D_I), ((0, 0), (0, NEW_PAD - DS), (0, 0)))
    keep_p, keep_n = sample_index(page_table, qi, wi, ki_new, pool_a_idx, min(DSA_TOPK, (past_len + DS) // 4))
    tab_p, tab_n = bias_tables(bias_a, past_len)
    token_rows = lambda pool: pool.reshape(-1, G_A * 2, HD)
    o_a = sample_attend(page_table, q_rows(COL_QA), kv_new(COL_KVA), keep_p, keep_n, tab_p, tab_n, token_rows(pool_a_kv))

    c_new = seg(COL_CMP, G_B * 2 * HD).reshape(DB, DS, G_B, 2, HD).transpose(0, 2, 3, 1, 4)
    c_new = jnp.pad(c_new, ((0, 0), (0, 0), (0, 0), (0, CMP_STRIDE - DS), (0, 0)))
    kcv = sample_compress(page_table, c_new, token_rows(pool_b_cmp), cmp_pe, cmp_w1, cmp_w2, gain_c)
    gates = pad_tok(seg(COL_SMALL + OFF_GATE, 3 * H_B)).reshape(DB, TOK_PAD, G_B, N_REP, 3)
    gates = gates.transpose(4, 0, 2, 3, 1).reshape(3, DB, G_B, rows, 1)
    q_b = q_rows(COL_QB)
    oc, keep_p, keep_n = sample_cmp_attention(q_b, kcv, gates[0], past_len, DS)
    tab_p, tab_n = bias_tables(bias_b, past_len)
    ocs = sample_attend(page_table, q_b, kv_new(COL_SLC), keep_p, keep_n, tab_p, tab_n, token_rows(pool_b_slc), gate_prev=(gates[1], oc))
    w_eff = cache_win.shape[1]
    w_pages = w_eff // PAGE_SIZE
    win_pool = token_rows(cache_win)
    win_pt = jnp.arange(DB * w_pages, dtype=jnp.int32).reshape(DB, w_pages)
    keep_w = ((w_eff + tok - jnp.arange(w_eff)[None, :]) < WINDOW).astype(F32).reshape(1, 1, TOK_PAD, w_eff)
    keep_wn = ((new <= tok) & (new < DS)).astype(F32).reshape(1, 1, TOK_PAD, NEW_PAD)
    tab_p, tab_n = bias_tables(bias_b, w_eff)
    o_b = sample_attend(win_pt, q_b, kv_new(COL_WIN), keep_w, keep_wn, tab_p, tab_n, win_pool, gate_prev=(gates[2], ocs))

    n_mem = cache_mem.shape[1]
    o_m = mem_attention(zs.reshape(DB, DS, -1), COL_QM, cache_mem.reshape(DB, n_mem, H_M * 2 * HD_M), DS)
    return unrows(o_a), unrows(o_b), o_m.reshape(DB * DS, H_M * HD_M)


def kernel(x_prompt, x_sample, cache_a_kv, cache_a_idx, cache_b_cmp, cache_b_slc, cache_b_win, cache_mem_kv, page_table, mem_prompt, g_attn, w_in, qk_gain_a, qk_gain_b, qk_gain_m, cmp_pe, cmp_w1, cmp_w2, rel_bias, g_mem, w_mem_kv, w_up_a, w_up_b, w_up_m, w_out, g_ffn, w_ffn_in, w_ffn_out):
    l = 0
    B, T, D = x_prompt.shape
    assert B == 1
    DB, DS = x_sample.shape[:2]
    ki_cols = slice(COL_SMALL + OFF_KI, COL_SMALL + OFF_KI + D_I)
    W = prepare_weights(w_in[l], w_mem_kv[l], w_up_a[l], w_up_b[l], w_up_m[l], w_out[l], w_ffn_in[l], w_ffn_out[l])
    norm_vecs = main_norm_vectors(qk_gain_a[l], qk_gain_b[l], qk_gain_m[l])

    xp = x_prompt.reshape(T, D)
    z, mg = project_in(xp, g_attn[l], W, norm_vecs, 1024)
    n_mem = mem_prompt.shape[1]
    hm = rms_cast(mem_prompt.reshape(n_mem, D), g_mem[l], n_mem)
    kgain = jnp.tile(jnp.concatenate([qk_gain_m[l, 1], jnp.zeros((HD_M,), F32)]), H_M).reshape(1, -1)
    kflag = jnp.tile(jnp.concatenate([jnp.ones((HD_M,), F32), jnp.zeros((HD_M,), F32)]), H_M).reshape(1, -1)
    mkv = matmul_headnorm(hm, W['w_mem_kv'], kgain, jnp.zeros_like(kflag), kflag,
                          jnp.ones((kflag.shape[1] // COL_TILE,), jnp.int32), n_mem, COL_TILE, "mem_kv")
    o_a, o_b, o_m = prompt_mixers(z, mkv, rel_bias, cmp_pe[l], cmp_w1[l], cmp_w2[l], qk_gain_b[l, 1], ATTN_TILE)
    y_p = merge_ffn(xp, o_a, o_b, o_m, mg, W, g_ffn[l], 1024).reshape(B, T, D)
    kv5 = lambda c, rows: z[rows, c:c + G_A * 2 * HD].reshape(1, 1, -1, G_A, 2, HD)
    p_out = (kv5(COL_KVA, slice(None)), z[:, COL_SMALL + OFF_KI:COL_SMALL + OFF_KI + D_I].reshape(1, 1, T, D_I),
             kv5(COL_CMP, slice(None)), kv5(COL_SLC, slice(None)), kv5(COL_WIN, slice(T - min(WINDOW, T), T)),
             mkv.reshape(1, 1, n_mem, H_M, 2, HD_M))

    n_s = DB * DS
    xs = x_sample.reshape(n_s, D)
    zs, mg = project_in(xs, g_attn[l], W, norm_vecs, n_s)
    o_a, o_b, o_m = sample_mixers(zs, page_table, cache_a_kv[l], cache_a_idx[l], cache_b_cmp[l], cache_b_slc[l], cache_b_win[l],
                                  cache_mem_kv[l], rel_bias, cmp_pe[l], cmp_w1[l], cmp_w2[l], qk_gain_b[l, 1], DS)
    y_s = merge_ffn(xs, o_a, o_b, o_m, mg, W, g_ffn[l], n_s).reshape(DB, DS, D)
    new5 = lambda c: zs[:, c:c + G_A * 2 * HD].reshape(1, DB, DS, G_A, 2, HD)
    s_out = (new5(COL_KVA), zs[:, ki_cols].reshape(1, DB, DS, D_I), new5(COL_CMP), new5(COL_SLC),
             jnp.concatenate([cache_b_win[l], new5(COL_WIN)[0]], axis=1)[None, :, DS:])
    return (y_p, y_s) + p_out + s_out
```
